```python
import jax, jax.numpy as jnp
from jax import lax
import numpy as np

D_MODEL = 2048
BATCH = 4
SEQ = 2048
DEPTH = 1

MIX_WIDTH = D_MODEL
DIFF_WIDTH = MIX_WIDTH // 2
MLA_WIDTH = MIX_WIDTH - DIFF_WIDTH
DIFF_HEAD_DIM = 64
DIFF_V_DIM = 2 * DIFF_HEAD_DIM
DIFF_HEADS = DIFF_WIDTH // DIFF_V_DIM
DIFF_ROT_DIM = DIFF_HEAD_DIM // 4
ROPE_THETA = 500000.0
LAMBDA_INIT_STD = 0.1
MLA_V_DIM = 128
MLA_HEADS = MLA_WIDTH // MLA_V_DIM
MLA_NOPE_DIM = 128
MLA_ROPE_DIM = 64
MLA_ROPE_THETA = 10000.0
Q_LORA_RANK = D_MODEL // 4
KV_LORA_RANK = D_MODEL // 8
D_FF = 4 * D_MODEL
Q_BLOCK = 128
NORM_EPS = 1e-6
N_MOD = 6

DIFF_Q_COLS = DIFF_HEADS * 2 * DIFF_HEAD_DIM
DIFF_K_COLS = DIFF_HEADS * 2 * DIFF_HEAD_DIM
DIFF_V_COLS = DIFF_HEADS * DIFF_V_DIM
IN_COLS = DIFF_Q_COLS + DIFF_K_COLS + DIFF_V_COLS + Q_LORA_RANK + KV_LORA_RANK + MLA_ROPE_DIM
IN_SPLITS = (DIFF_Q_COLS,
             DIFF_Q_COLS + DIFF_K_COLS,
             DIFF_Q_COLS + DIFF_K_COLS + DIFF_V_COLS,
             DIFF_Q_COLS + DIFF_K_COLS + DIFF_V_COLS + Q_LORA_RANK,
             DIFF_Q_COLS + DIFF_K_COLS + DIFF_V_COLS + Q_LORA_RANK + KV_LORA_RANK)

kernel_name = "hybrid_diffattn_mla_sqrelu_adaln_encoder"


def rmsnorm(x, g):
    x32 = x.astype(jnp.float32)
    y = x32 * lax.rsqrt(jnp.mean(x32 * x32, axis=-1, keepdims=True) + NORM_EPS)
    return (y * g.astype(jnp.float32)).astype(x.dtype)


def modulate(h, shift, scale):
    return h * (1.0 + scale[:, None, :]) + shift[:, None, :]


def rope_cos_sin(positions, dim, theta):
    inv = 1.0 / (theta ** (jnp.arange(0, dim, 2, dtype=jnp.float32) / dim))
    ang = positions.astype(jnp.float32)[..., None] * inv
    ang = jnp.concatenate([ang, ang], axis=-1)
    return jnp.cos(ang), jnp.sin(ang)


def apply_rope(x, cos, sin):
    half = x.shape[-1] // 2
    x1, x2 = x[..., :half], x[..., half:]
    rot = jnp.concatenate([-x2, x1], axis=-1)
    return (x.astype(jnp.float32) * cos + rot.astype(jnp.float32) * sin).astype(x.dtype)


def partial_rope(x, cos, sin, rot_dim):
    return jnp.concatenate([apply_rope(x[..., :rot_dim], cos, sin), x[..., rot_dim:]], axis=-1)


def to_blocks(a):
    b, s = a.shape[0], a.shape[1]
    a = a.reshape((b, s // Q_BLOCK, Q_BLOCK) + a.shape[2:])
    return jnp.moveaxis(a, 1, 0)


def from_blocks(a):
    a = jnp.moveaxis(a, 0, 1)
    return a.reshape((a.shape[0], a.shape[1] * a.shape[2]) + a.shape[3:])


def diff_attention(q, k, v, lam):
    scale = DIFF_HEAD_DIM ** -0.5
    lam32 = lam.astype(jnp.float32)

    def block(qb):
        s = jnp.einsum('bqhcd,bkhcd->bhcqk', qb * scale, k).astype(jnp.float32)
        p = jax.nn.softmax(s, axis=-1)
        a = p[:, :, 0] - lam32 * p[:, :, 1]
        return jnp.einsum('bhqk,bkhe->bqhe', a.astype(v.dtype), v)

    return from_blocks(lax.map(block, to_blocks(q)))


def mla_attention(q_nope, q_rope, k_nope, k_rope, v):
    scale = (MLA_NOPE_DIM + MLA_ROPE_DIM) ** -0.5

    def block(args):
        qn, qr = args
        s = (jnp.einsum('bqhd,bkhd->bhqk', qn, k_nope)
             + jnp.einsum('bqhr,bkr->bhqk', qr, k_rope)).astype(jnp.float32) * scale
        p = jax.nn.softmax(s, axis=-1)
        return jnp.einsum('bhqk,bkhe->bqhe', p.astype(v.dtype), v)

    return from_blocks(lax.map(block, (to_blocks(q_nope), to_blocks(q_rope))))


def setup_inputs(seed: int = 0) -> dict:
    key = jax.random.key(seed)
    ks = jax.random.split(key, 24)
    f32 = jnp.float32

    def nrm(k, shape, fan_in):
        return jax.random.normal(k, shape, f32) * (fan_in ** -0.5)

    def gain(k, shape):
        return 1.0 + 0.02 * jax.random.normal(k, shape, f32)

    x = jax.random.normal(ks[0], (BATCH, SEQ, D_MODEL), f32)
    c = jax.random.normal(ks[1], (BATCH, D_MODEL), f32)
    positions = (jnp.arange(SEQ, dtype=jnp.int32)[None, :]
                 + jax.random.randint(ks[2], (BATCH, 1), 0, 1024, dtype=jnp.int32))
    return {
        "x": x,
        "c": c,
        "positions": positions,
        "w_ada": nrm(ks[3], (DEPTH, D_MODEL, N_MOD * D_MODEL), D_MODEL),
        "b_ada": 0.02 * jax.random.normal(ks[4], (DEPTH, N_MOD * D_MODEL), f32),
        "g_norm_mix": gain(ks[5], (DEPTH, D_MODEL)),
        "w_in": nrm(ks[6], (DEPTH, D_MODEL, IN_COLS), D_MODEL),
        "lambda_q1": LAMBDA_INIT_STD * jax.random.normal(ks[7], (DEPTH, DIFF_HEAD_DIM), f32),
        "lambda_k1": LAMBDA_INIT_STD * jax.random.normal(ks[8], (DEPTH, DIFF_HEAD_DIM), f32),
        "lambda_q2": LAMBDA_INIT_STD * jax.random.normal(ks[9], (DEPTH, DIFF_HEAD_DIM), f32),
        "lambda_k2": LAMBDA_INIT_STD * jax.random.normal(ks[10], (DEPTH, DIFF_HEAD_DIM), f32),
        "g_diff_sub": gain(ks[11], (DEPTH, DIFF_V_DIM)),
        "g_q_a": gain(ks[12], (DEPTH, Q_LORA_RANK)),
        "w_q_b": nrm(ks[13], (DEPTH, Q_LORA_RANK, MLA_HEADS * (MLA_NOPE_DIM + MLA_ROPE_DIM)), Q_LORA_RANK),
        "g_kv_a": gain(ks[14], (DEPTH, KV_LORA_RANK)),
        "w_kv_b": nrm(ks[15], (DEPTH, KV_LORA_RANK, MLA_HEADS * (MLA_NOPE_DIM + MLA_V_DIM)), KV_LORA_RANK),
        "w_out": nrm(ks[16], (DEPTH, MIX_WIDTH, D_MODEL), MIX_WIDTH),
        "g_norm_ffn": gain(ks[17], (DEPTH, D_MODEL)),
        "w_ff1": nrm(ks[18], (DEPTH, D_MODEL, D_FF), D_MODEL),
        "w_ff2": nrm(ks[19], (DEPTH, D_FF, D_MODEL), D_FF),
        "g_final": gain(ks[20], (D_MODEL,)),
    }


def reference(x, c, positions, w_ada, b_ada, g_norm_mix, w_in, lambda_q1, lambda_k1,
              lambda_q2, lambda_k2, g_diff_sub, g_q_a, w_q_b, g_kv_a, w_kv_b, w_out,
              g_norm_ffn, w_ff1, w_ff2, g_final):
    b, s, _ = x.shape
    cos_d, sin_d = rope_cos_sin(positions, DIFF_ROT_DIM, ROPE_THETA)
    cos_d, sin_d = cos_d[:, :, None, None, :], sin_d[:, :, None, None, :]
    cos_m, sin_m = rope_cos_sin(positions, MLA_ROPE_DIM, MLA_ROPE_THETA)
    silu_c = jax.nn.silu(c)

    for l in range(DEPTH):
        lambda_init = 0.8 - 0.6 * float(np.exp(-0.3 * l))
        mod = silu_c @ w_ada[l] + b_ada[l]
        shift_m, scale_m, gate_m, shift_f, scale_f, gate_f = jnp.split(mod, N_MOD, axis=-1)

        h = modulate(rmsnorm(x, g_norm_mix[l]), shift_m, scale_m)
        proj = h @ w_in[l]
        dq, dk, dv, q_lat, kv_lat, k_pe = jnp.split(proj, IN_SPLITS, axis=-1)

        dq = partial_rope(dq.reshape(b, s, DIFF_HEADS, 2, DIFF_HEAD_DIM), cos_d, sin_d, DIFF_ROT_DIM)
        dk = partial_rope(dk.reshape(b, s, DIFF_HEADS, 2, DIFF_HEAD_DIM), cos_d, sin_d, DIFF_ROT_DIM)
        dv = dv.reshape(b, s, DIFF_HEADS, DIFF_V_DIM)
        lam = (jnp.exp(jnp.sum(lambda_q1[l].astype(jnp.float32) * lambda_k1[l].astype(jnp.float32)))
               - jnp.exp(jnp.sum(lambda_q2[l].astype(jnp.float32) * lambda_k2[l].astype(jnp.float32)))
               + lambda_init)
        o_diff = diff_attention(dq, dk, dv, lam)
        o_diff = rmsnorm(o_diff, g_diff_sub[l]) * (1.0 - lambda_init)
        o_diff = o_diff.reshape(b, s, DIFF_WIDTH)

        cq = rmsnorm(q_lat, g_q_a[l])
        q = (cq @ w_q_b[l]).reshape(b, s, MLA_HEADS, MLA_NOPE_DIM + MLA_ROPE_DIM)
        q_nope = q[..., :MLA_NOPE_DIM]
        q_rope = apply_rope(q[..., MLA_NOPE_DIM:], cos_m[:, :, None, :], sin_m[:, :, None, :])
        ckv = rmsnorm(kv_lat, g_kv_a[l])
        kv = (ckv @ w_kv_b[l]).reshape(b, s, MLA_HEADS, MLA_NOPE_DIM + MLA_V_DIM)
        k_nope, v_mla = kv[..., :MLA_NOPE_DIM], kv[..., MLA_NOPE_DIM:]
        k_rope = apply_rope(k_pe, cos_m, sin_m)
        o_mla = mla_attention(q_nope, q_rope, k_nope, k_rope, v_mla).reshape(b, s, MLA_WIDTH)

        mix = jnp.concatenate([o_diff, o_mla], axis=-1) @ w_out[l]
        x = x + gate_m[:, None, :] * mix

        h = modulate(rmsnorm(x, g_norm_ffn[l]), shift_f, scale_f)
        f = jnp.square(jax.nn.relu(h @ w_ff1[l])) @ w_ff2[l]
        x = x + gate_f[:, None, :] * f

    return rmsnorm(x, g_final)
```

```python
import functools
import math

import jax
import jax.numpy as jnp
import numpy as np
from jax import lax
from jax.experimental import pallas as pl
from jax.experimental.pallas import tpu as pltpu

F32 = jnp.float32
BF16 = jnp.bfloat16

LANES = 128
NORM_EPS = 1e-6
N_MOD = 6

DIFF_HEAD_DIM = 64
DIFF_V_DIM = 2 * DIFF_HEAD_DIM
DIFF_ROT_DIM = DIFF_HEAD_DIM // 4
ROPE_THETA = 500000.0
MLA_V_DIM = 128
MLA_NOPE_DIM = 128
MLA_ROPE_DIM = 64
MLA_ROPE_THETA = 10000.0
MLA_QK_PAD = 256

VMEM_LIMIT = 56 * 1024 * 1024


def _cparams(sem):
    return pltpu.CompilerParams(dimension_semantics=sem, vmem_limit_bytes=VMEM_LIMIT)


def _mod_kernel(c_ref, w_ref, b_ref, o_ref):
    c = c_ref[...]
    sc = c / (1.0 + jnp.exp(-c))
    o_ref[...] = jnp.dot(sc.astype(BF16), w_ref[...].astype(BF16),
                         preferred_element_type=F32) + b_ref[...]


def _mod_call(c_pad, w_ada, b_ada, tn=1024):
    m, d = c_pad.shape
    n = w_ada.shape[1]
    return pl.pallas_call(
        _mod_kernel,
        out_shape=jax.ShapeDtypeStruct((m, n), F32),
        grid=(n // tn,),
        in_specs=[pl.BlockSpec((m, d), lambda j: (0, 0)),
                  pl.BlockSpec((d, tn), lambda j: (0, j)),
                  pl.BlockSpec((1, tn), lambda j: (0, j))],
        out_specs=pl.BlockSpec((m, tn), lambda j: (0, j)),
        compiler_params=_cparams(("arbitrary",)),
        name="mod",
    )(c_pad, w_ada, b_ada)


def _norm_modulate_rows(x_ref, g_ref, mod_ref, shift_row, scale_row, h_ref, rows):
    tm = x_ref.shape[0]
    g = g_ref[...]
    shift = mod_ref[0, shift_row:shift_row + 1, :]
    scale1 = 1.0 + mod_ref[0, scale_row:scale_row + 1, :]

    def body(r, carry):
        sl = pl.ds(pl.multiple_of(r * rows, rows), rows)
        x = x_ref[sl, :]
        ms = jnp.mean(x * x, axis=-1, keepdims=True)
        y = x * lax.rsqrt(ms + NORM_EPS) * g
        h_ref[sl, :] = (y * scale1 + shift).astype(BF16)
        return carry

    lax.fori_loop(0, tm // rows, body, 0)


def _inproj_kernel(x_ref, mod_ref, g_ref, w_ref, wpe_ref, o_ref, kpe_ref, h_ref):
    @pl.when(pl.program_id(1) == 0)
    def _():
        _norm_modulate_rows(x_ref, g_ref, mod_ref, 0, 1, h_ref, 16)
        kpe_ref[...] = jnp.dot(h_ref[...], wpe_ref[...], preferred_element_type=F32)

    o_ref[...] = jnp.dot(h_ref[...], w_ref[...].astype(BF16), preferred_element_type=F32)


def _inproj_call(x2, mod3, g, w_in, w_pe, n_main, seq, tm=1024, tn=768):
    t, d = x2.shape
    tpb = seq // tm
    return pl.pallas_call(
        _inproj_kernel,
        out_shape=(jax.ShapeDtypeStruct((t, n_main), F32),
                   jax.ShapeDtypeStruct((t, LANES), F32)),
        grid=(t // tm, n_main // tn),
        in_specs=[pl.BlockSpec((tm, d), lambda i, j: (i, 0)),
                  pl.BlockSpec((1, N_MOD, d), lambda i, j: (i // tpb, 0, 0)),
                  pl.BlockSpec((1, d), lambda i, j: (0, 0)),
                  pl.BlockSpec((d, tn), lambda i, j: (0, j)),
                  pl.BlockSpec((d, LANES), lambda i, j: (0, 0))],
        out_specs=(pl.BlockSpec((tm, tn), lambda i, j: (i, j)),
                   pl.BlockSpec((tm, LANES), lambda i, j: (i, 0))),
        scratch_shapes=[pltpu.VMEM((tm, d), BF16)],
        compiler_params=_cparams(("arbitrary", "arbitrary")),
        name="inproj",
    )(x2, mod3, g, w_in, w_pe)


def _rope_slab(x, c, s_lo, s_hi, half):
    return x * c + pltpu.roll(x, half, 1) * s_hi + pltpu.roll(x, LANES - half, 1) * s_lo


def _rms_rows(x, g):
    ms = jnp.mean(x * x, axis=-1, keepdims=True)
    return x * lax.rsqrt(ms + NORM_EPS) * g


def _post_kernel(dq_ref, dk_ref, dv_ref, ql_ref, kvl_ref, kpe_ref,
                 cd_ref, sdlo_ref, sdhi_ref, cm_ref, smlo_ref, smhi_ref,
                 gq_ref, gkv_ref, wq_ref, wkv_ref,
                 dqo_ref, dko_ref, dvo_ref, qmo_ref, kmo_ref, vmo_ref, *, n_heads):
    cd, sdlo, sdhi = cd_ref[...], sdlo_ref[...], sdhi_ref[...]
    cm, smlo, smhi = cm_ref[...], smlo_ref[...], smhi_ref[...]
    d_scale = DIFF_HEAD_DIM ** -0.5
    m_scale = (MLA_NOPE_DIM + MLA_ROPE_DIM) ** -0.5
    half_d = DIFF_ROT_DIM // 2
    half_m = MLA_ROPE_DIM // 2

    for k in range(dq_ref.shape[1] // LANES):
        sl = slice(k * LANES, (k + 1) * LANES)
        dqo_ref[:, sl] = (_rope_slab(dq_ref[:, sl], cd, sdlo, sdhi, half_d) * d_scale).astype(BF16)
        dko_ref[:, sl] = _rope_slab(dk_ref[:, sl], cd, sdlo, sdhi, half_d).astype(BF16)
    dvo_ref[...] = dv_ref[...].astype(BF16)

    cq = _rms_rows(ql_ref[...], gq_ref[...]).astype(BF16)
    q = jnp.dot(cq, wq_ref[...], preferred_element_type=F32)
    ckv = _rms_rows(kvl_ref[...], gkv_ref[...]).astype(BF16)
    kv = jnp.dot(ckv, wkv_ref[...], preferred_element_type=F32)
    kr = _rope_slab(kpe_ref[...], cm, smlo, smhi, half_m).astype(BF16)

    for h in range(n_heads):
        base = h * MLA_QK_PAD
        nope = slice(base, base + LANES)
        rope = slice(base + LANES, base + 2 * LANES)
        qmo_ref[:, nope] = (q[:, nope] * m_scale).astype(BF16)
        qmo_ref[:, rope] = (_rope_slab(q[:, rope], cm, smlo, smhi, half_m) * m_scale).astype(BF16)
        kmo_ref[:, nope] = kv[:, nope].astype(BF16)
        kmo_ref[:, rope] = kr
        vmo_ref[:, h * LANES:(h + 1) * LANES] = kv[:, rope].astype(BF16)


def _post_call(proj, kpe, tabs, gq, gkv, wq_pad, wkv, n_heads, tm=256):
    t = proj.shape[0]
    dw = n_heads * DIFF_V_DIM
    q_rank = gq.shape[1]
    kv_rank = gkv.shape[1]
    qk_w = n_heads * MLA_QK_PAD
    v_w = n_heads * MLA_V_DIM
    row = lambda w, c: pl.BlockSpec((tm, w), lambda i, c=c: (i, c))
    full = lambda a: pl.BlockSpec(a.shape, lambda i: (0, 0))
    in_specs = [row(dw, 0), row(dw, 1), row(dw, 2),
                row(q_rank, 3 * dw // q_rank),
                row(kv_rank, (3 * dw + q_rank) // kv_rank),
                row(LANES, 0)]
    in_specs += [row(LANES, 0)] * 6
    in_specs += [full(gq), full(gkv), full(wq_pad), full(wkv)]
    out_shape = (jax.ShapeDtypeStruct((t, dw), BF16),) * 3 + (
        jax.ShapeDtypeStruct((t, qk_w), BF16),
        jax.ShapeDtypeStruct((t, qk_w), BF16),
        jax.ShapeDtypeStruct((t, v_w), BF16))
    out_specs = (row(dw, 0),) * 3 + (row(qk_w, 0), row(qk_w, 0), row(v_w, 0))
    return pl.pallas_call(
        functools.partial(_post_kernel, n_heads=n_heads),
        out_shape=out_shape,
        grid=(t // tm,),
        in_specs=in_specs,
        out_specs=out_specs,
        compiler_params=_cparams(("arbitrary",)),
        name="post",
    )(proj, proj, proj, proj, proj, kpe, *tabs, gq, gkv, wq_pad, wkv)


def _softmax_pv(s, v_ref):
    m = jnp.max(s, axis=-1, keepdims=True)
    e = jnp.exp(s - m).astype(BF16)
    v = v_ref[0]
    v1 = jnp.concatenate([v, jnp.ones_like(v)], axis=1)
    o = jnp.dot(e, v1, preferred_element_type=F32)
    dv = v.shape[1]
    return o[:, :dv], o[:, dv:dv + 1]


def _diffattn_kernel(lq1_ref, lk1_ref, lq2_ref, lk2_ref, g_ref, q_ref, k_ref, v_ref, o_ref,
                     *, lambda_init):
    lam = (jnp.exp(jnp.sum(lq1_ref[...] * lk1_ref[...], axis=-1, keepdims=True))
           - jnp.exp(jnp.sum(lq2_ref[...] * lk2_ref[...], axis=-1, keepdims=True))
           + lambda_init)
    q = q_ref[0]
    tq = q.shape[0]
    lane = lax.broadcasted_iota(jnp.int32, q.shape, 1)
    zero = jnp.zeros_like(q)
    q12 = jnp.concatenate([jnp.where(lane < DIFF_HEAD_DIM, q, zero),
                           jnp.where(lane >= DIFF_HEAD_DIM, q, zero)], axis=0)
    s = lax.dot_general(q12, k_ref[0], (((1,), (1,)), ((), ())),
                        preferred_element_type=F32)
    o, l = _softmax_pv(s, v_ref)
    out = o[:tq] / l[:tq] - lam * (o[tq:] / l[tq:])
    ms = jnp.mean(out * out, axis=-1, keepdims=True)
    out = out * lax.rsqrt(ms + NORM_EPS) * g_ref[...] * (1.0 - lambda_init)
    o_ref[0] = out.astype(o_ref.dtype)


def _diffattn_call(lams, g, dq, dk, dv, n_heads, lambda_init, tq=256):
    b, s, _ = dq.shape
    vec = lambda a: pl.BlockSpec(a.shape, lambda bi, h, qi: (0, 0))
    return pl.pallas_call(
        functools.partial(_diffattn_kernel, lambda_init=lambda_init),
        out_shape=jax.ShapeDtypeStruct(dv.shape, BF16),
        grid=(b, n_heads, s // tq),
        in_specs=[vec(lams[0]), vec(lams[1]), vec(lams[2]), vec(lams[3]), vec(g),
                  pl.BlockSpec((1, tq, LANES), lambda bi, h, qi: (bi, qi, h)),
                  pl.BlockSpec((1, s, LANES), lambda bi, h, qi: (bi, 0, h)),
                  pl.BlockSpec((1, s, LANES), lambda bi, h, qi: (bi, 0, h))],
        out_specs=pl.BlockSpec((1, tq, LANES), lambda bi, h, qi: (bi, qi, h)),
        compiler_params=_cparams(("arbitrary", "arbitrary", "arbitrary")),
        name="diffattn",
    )(*lams, g, dq, dk, dv)


def _mlaattn_kernel(q_ref, k_ref, v_ref, o_ref):
    s = lax.dot_general(q_ref[0], k_ref[0], (((1,), (1,)), ((), ())),
                        preferred_element_type=F32)
    o, l = _softmax_pv(s, v_ref)
    o_ref[0] = (o / l).astype(o_ref.dtype)


def _mlaattn_call(qm, km, vm, n_heads, tq=256):
    b, s, _ = qm.shape
    return pl.pallas_call(
        _mlaattn_kernel,
        out_shape=jax.ShapeDtypeStruct(vm.shape, BF16),
        grid=(b, n_heads, s // tq),
        in_specs=[pl.BlockSpec((1, tq, MLA_QK_PAD), lambda bi, h, qi: (bi, qi, h)),
                  pl.BlockSpec((1, s, MLA_QK_PAD), lambda bi, h, qi: (bi, 0, h)),
                  pl.BlockSpec((1, s, MLA_V_DIM), lambda bi, h, qi: (bi, 0, h))],
        out_specs=pl.BlockSpec((1, tq, MLA_V_DIM), lambda bi, h, qi: (bi, qi, h)),
        compiler_params=_cparams(("arbitrary", "arbitrary", "arbitrary")),
        name="mlaattn",
    )(qm, km, vm)


def _outproj_kernel(od_ref, om_ref, wt_ref, wb_ref, x_ref, mod_ref, o_ref):
    acc = jnp.dot(od_ref[...], wt_ref[...].astype(BF16), preferred_element_type=F32)
    acc += jnp.dot(om_ref[...], wb_ref[...].astype(BF16), preferred_element_type=F32)
    o_ref[...] = x_ref[...] + mod_ref[0, 2:3, :] * acc


def _outproj_call(od, om, w_out, x2, mod3, seq, tm=1024, tn=512):
    t, d = x2.shape
    kd = od.shape[1]
    tpb = seq // tm
    return pl.pallas_call(
        _outproj_kernel,
        out_shape=jax.ShapeDtypeStruct((t, d), F32),
        grid=(t // tm, d // tn),
        in_specs=[pl.BlockSpec((tm, kd), lambda i, j: (i, 0)),
                  pl.BlockSpec((tm, om.shape[1]), lambda i, j: (i, 0)),
                  pl.BlockSpec((kd, tn), lambda i, j: (0, j)),
                  pl.BlockSpec((om.shape[1], tn), lambda i, j: (kd // om.shape[1], j)),
                  pl.BlockSpec((tm, tn), lambda i, j: (i, j)),
                  pl.BlockSpec((1, N_MOD, tn), lambda i, j: (i // tpb, 0, j))],
        out_specs=pl.BlockSpec((tm, tn), lambda i, j: (i, j)),
        compiler_params=_cparams(("arbitrary", "arbitrary")),
        name="outproj",
    )(od, om, w_out, w_out, x2, mod3)


def _ffn_kernel(x_ref, mod_ref, g_ref, w1_ref, w2_ref, gf_ref, o_ref, h_ref, *, rows):
    j = pl.program_id(1)
    nj = pl.num_programs(1)

    @pl.when(j == 0)
    def _():
        _norm_modulate_rows(x_ref, g_ref, mod_ref, 3, 4, h_ref, rows)
        o_ref[...] = jnp.zeros_like(o_ref)

    u = jnp.dot(h_ref[...], w1_ref[...].astype(BF16), preferred_element_type=F32)
    r = jnp.square(jnp.maximum(u, 0.0)).astype(BF16)
    o_ref[...] += jnp.dot(r, w2_ref[...].astype(BF16), preferred_element_type=F32)

    @pl.when(j == nj - 1)
    def _():
        gate = mod_ref[0, 5:6, :]
        gf = gf_ref[...]

        def body(r_i, carry):
            sl = pl.ds(pl.multiple_of(r_i * rows, rows), rows)
            y = x_ref[sl, :] + gate * o_ref[sl, :]
            o_ref[sl, :] = _rms_rows(y, gf)
            return carry

        lax.fori_loop(0, x_ref.shape[0] // rows, body, 0)


def _ffn_call(x1, mod3, g, w1, w2, gf, seq, tm=1024, tf=256):
    t, d = x1.shape
    dff = w1.shape[1]
    tpb = seq // tm
    return pl.pallas_call(
        functools.partial(_ffn_kernel, rows=16),
        out_shape=jax.ShapeDtypeStruct((t, d), F32),
        grid=(t // tm, dff // tf),
        in_specs=[pl.BlockSpec((tm, d), lambda i, j: (i, 0)),
                  pl.BlockSpec((1, N_MOD, d), lambda i, j: (i // tpb, 0, 0)),
                  pl.BlockSpec((1, d), lambda i, j: (0, 0)),
                  pl.BlockSpec((d, tf), lambda i, j: (0, j)),
                  pl.BlockSpec((tf, d), lambda i, j: (j, 0)),
                  pl.BlockSpec((1, d), lambda i, j: (0, 0))],
        out_specs=pl.BlockSpec((tm, d), lambda i, j: (i, 0)),
        scratch_shapes=[pltpu.VMEM((tm, d), BF16)],
        compiler_params=_cparams(("arbitrary", "arbitrary")),
        name="ffn",
    )(x1, mod3, g, w1, w2, gf)


def _rope_tables(positions, rot_dim, theta, group):
    half = rot_dim // 2
    lane = np.arange(LANES)
    g = lane % group
    inv = 1.0 / (theta ** (jnp.arange(0, rot_dim, 2, dtype=F32) / rot_dim))
    inv_lane = inv[g % half]
    ang = positions.astype(F32).reshape(-1, 1) * inv_lane[None, :]
    cos, sin = jnp.cos(ang), jnp.sin(ang)
    rot = jnp.asarray(g < rot_dim)[None, :]
    lo = jnp.asarray(g < half)[None, :]
    hi = jnp.asarray((g >= half) & (g < rot_dim))[None, :]
    return (jnp.where(rot, cos, 1.0), jnp.where(lo, -sin, 0.0), jnp.where(hi, sin, 0.0))


def kernel(x, c, positions, w_ada, b_ada, g_norm_mix, w_in, lambda_q1, lambda_k1, lambda_q2,
           lambda_k2, g_diff_sub, g_q_a, w_q_b, g_kv_a, w_kv_b, w_out, g_norm_ffn, w_ff1, w_ff2,
           g_final):
    b, s, d = x.shape
    depth = w_ada.shape[0]
    t = b * s
    q_rank = g_q_a.shape[1]
    kv_rank = g_kv_a.shape[1]
    n_mla = w_kv_b.shape[2] // (MLA_NOPE_DIM + MLA_V_DIM)
    n_diff = (w_in.shape[2] - q_rank - kv_rank - MLA_ROPE_DIM) // (3 * DIFF_V_DIM)
    assert n_diff == n_mla
    n_main = w_in.shape[2] - MLA_ROPE_DIM

    tabs = (_rope_tables(positions, DIFF_ROT_DIM, ROPE_THETA, DIFF_HEAD_DIM)
            + _rope_tables(positions, MLA_ROPE_DIM, MLA_ROPE_THETA, LANES))
    c_pad = jnp.pad(c, ((0, 8 - b), (0, 0)))
    x2 = x.reshape(t, d)

    for l in range(depth):
        lambda_init = 0.8 - 0.6 * float(np.exp(-0.3 * l))
        mod = _mod_call(c_pad, w_ada[l], b_ada[l][None, :])
        mod3 = mod[:b].reshape(b, N_MOD, d)

        w_pe = jnp.pad(w_in[l][:, n_main:], ((0, 0), (0, LANES - MLA_ROPE_DIM))).astype(BF16)
        proj, kpe = _inproj_call(x2, mod3, g_norm_mix[l][None, :], w_in[l], w_pe, n_main, s)

        wq_pad = jnp.pad(w_q_b[l].reshape(q_rank, n_mla, MLA_NOPE_DIM + MLA_ROPE_DIM),
                         ((0, 0), (0, 0), (0, MLA_QK_PAD - MLA_NOPE_DIM - MLA_ROPE_DIM))
                         ).reshape(q_rank, n_mla * MLA_QK_PAD).astype(BF16)
        dq, dk, dv, qm, km, vm = _post_call(
            proj, kpe, tabs, g_q_a[l][None, :], g_kv_a[l][None, :], wq_pad,
            w_kv_b[l].astype(BF16), n_mla)

        shp = lambda a: a.reshape(b, s, a.shape[1])
        lams = tuple(v[l][None, :] for v in (lambda_q1, lambda_k1, lambda_q2, lambda_k2))
        o_diff = _diffattn_call(lams, g_diff_sub[l][None, :], shp(dq), shp(dk), shp(dv),
                                n_diff, lambda_init)
        o_mla = _mlaattn_call(shp(qm), shp(km), shp(vm), n_mla)

        x2 = _outproj_call(o_diff.reshape(t, -1), o_mla.reshape(t, -1), w_out[l], x2, mod3, s)

        last = l == depth - 1
        assert last, "final rmsnorm is fused into the last layer's MLP kernel"
        x2 = _ffn_call(x2, mod3, g_norm_ffn[l][None, :], w_ff1[l], w_ff2[l], g_final[None, :], s)

    return x2.reshape(b, s, d)
```

```python
import functools
import math

import jax
import jax.numpy as jnp
import numpy as np
from jax import lax
from jax.experimental import pallas as pl
from jax.experimental.pallas import tpu as pltpu

F32 = jnp.float32
BF16 = jnp.bfloat16

LANES = 128
NORM_EPS = 1e-6
N_MOD = 6

DIFF_HEAD_DIM = 64
DIFF_V_DIM = 2 * DIFF_HEAD_DIM
DIFF_ROT_DIM = DIFF_HEAD_DIM // 4
ROPE_THETA = 500000.0
MLA_V_DIM = 128
MLA_NOPE_DIM = 128
MLA_ROPE_DIM = 64
MLA_ROPE_THETA = 10000.0
MLA_QK_PAD = 256

VMEM_LIMIT = 56 * 1024 * 1024


def _cparams(sem):
    return pltpu.CompilerParams(dimension_semantics=sem, vmem_limit_bytes=VMEM_LIMIT)


def _mod_kernel(c_ref, w_ref, b_ref, o_ref):
    c = c_ref[...]
    sc = c / (1.0 + jnp.exp(-c))
    o_ref[...] = jnp.dot(sc.astype(BF16), w_ref[...].astype(BF16),
                         preferred_element_type=F32) + b_ref[...]


def _mod_call(c_pad, w_ada, b_ada, tn=1024):
    m, d = c_pad.shape
    n = w_ada.shape[1]
    return pl.pallas_call(
        _mod_kernel,
        out_shape=jax.ShapeDtypeStruct((m, n), F32),
        grid=(n // tn,),
        in_specs=[pl.BlockSpec((m, d), lambda j: (0, 0)),
                  pl.BlockSpec((d, tn), lambda j: (0, j)),
                  pl.BlockSpec((1, tn), lambda j: (0, j))],
        out_specs=pl.BlockSpec((m, tn), lambda j: (0, j)),
        compiler_params=_cparams(("arbitrary",)),
        name="mod",
    )(c_pad, w_ada, b_ada)


def _two_stage_rows(n_rows, rows, sumsq, finish):
    chunk = lambda r: pl.ds(pl.multiple_of(r * rows, rows), rows)

    def body(r, ss_prev):
        ss = sumsq(chunk(r))
        finish(chunk(r - 1), ss_prev)
        return ss

    n = n_rows // rows
    ss_last = lax.fori_loop(1, n, body, sumsq(chunk(0)))
    finish(chunk(n - 1), ss_last)


def _norm_modulate_rows(x_ref, g_ref, mod_ref, shift_row, scale_row, h_ref, rows):
    tm, d = x_ref.shape
    shift = mod_ref[0, shift_row:shift_row + 1, :]
    gs = g_ref[...] * (1.0 + mod_ref[0, scale_row:scale_row + 1, :])

    def sumsq(sl):
        x = x_ref[sl, :]
        return jnp.sum(x * x, axis=-1, keepdims=True)

    def finish(sl, ss):
        rs = lax.rsqrt(ss * (1.0 / d) + NORM_EPS)
        h_ref[sl, :] = (x_ref[sl, :] * rs * gs + shift).astype(BF16)

    _two_stage_rows(tm, rows, sumsq, finish)


def _inproj_kernel(x_ref, mod_ref, g_ref, w_ref, wpe_ref, o_ref, kpe_ref, h_ref):
    @pl.when(pl.program_id(1) == 0)
    def _():
        _norm_modulate_rows(x_ref, g_ref, mod_ref, 0, 1, h_ref, 16)
        kpe_ref[...] = jnp.dot(h_ref[...], wpe_ref[...], preferred_element_type=F32)

    o_ref[...] = jnp.dot(h_ref[...], w_ref[...].astype(BF16), preferred_element_type=F32)


def _inproj_call(x2, mod3, g, w_in, w_pe, n_main, seq, tm=1024, tn=768):
    t, d = x2.shape
    tpb = seq // tm
    return pl.pallas_call(
        _inproj_kernel,
        out_shape=(jax.ShapeDtypeStruct((t, n_main), F32),
                   jax.ShapeDtypeStruct((t, LANES), F32)),
        grid=(t // tm, n_main // tn),
        in_specs=[pl.BlockSpec((tm, d), lambda i, j: (i, 0)),
                  pl.BlockSpec((1, N_MOD, d), lambda i, j: (i // tpb, 0, 0)),
                  pl.BlockSpec((1, d), lambda i, j: (0, 0)),
                  pl.BlockSpec((d, tn), lambda i, j: (0, j)),
                  pl.BlockSpec((d, LANES), lambda i, j: (0, 0))],
        out_specs=(pl.BlockSpec((tm, tn), lambda i, j: (i, j)),
                   pl.BlockSpec((tm, LANES), lambda i, j: (i, 0))),
        scratch_shapes=[pltpu.VMEM((tm, d), BF16)],
        compiler_params=_cparams(("arbitrary", "arbitrary")),
        name="inproj",
    )(x2, mod3, g, w_in, w_pe)


def _rope_slab(x, c, s_lo, s_hi, half):
    return x * c + pltpu.roll(x, half, 1) * s_hi + pltpu.roll(x, LANES - half, 1) * s_lo


def _rms_rows(x, g):
    ms = jnp.mean(x * x, axis=-1, keepdims=True)
    return x * lax.rsqrt(ms + NORM_EPS) * g


def _post_kernel(dq_ref, dk_ref, dv_ref, ql_ref, kvl_ref, kpe_ref,
                 cd_ref, sdlo_ref, sdhi_ref, cm_ref, smlo_ref, smhi_ref,
                 gq_ref, gkv_ref, wq_ref, wkv_ref,
                 dqo_ref, dko_ref, dvo_ref, qmo_ref, kmo_ref, vmo_ref, *, n_heads):
    cd, sdlo, sdhi = cd_ref[...], sdlo_ref[...], sdhi_ref[...]
    cm, smlo, smhi = cm_ref[...], smlo_ref[...], smhi_ref[...]
    d_scale = DIFF_HEAD_DIM ** -0.5
    m_scale = (MLA_NOPE_DIM + MLA_ROPE_DIM) ** -0.5
    half_d = DIFF_ROT_DIM // 2
    half_m = MLA_ROPE_DIM // 2

    for k in range(dq_ref.shape[1] // LANES):
        sl = slice(k * LANES, (k + 1) * LANES)
        dqo_ref[:, sl] = (_rope_slab(dq_ref[:, sl], cd, sdlo, sdhi, half_d) * d_scale).astype(BF16)
        dko_ref[:, sl] = _rope_slab(dk_ref[:, sl], cd, sdlo, sdhi, half_d).astype(BF16)
    dvo_ref[...] = dv_ref[...].astype(BF16)

    cq = _rms_rows(ql_ref[...], gq_ref[...]).astype(BF16)
    q = jnp.dot(cq, wq_ref[...], preferred_element_type=F32)
    ckv = _rms_rows(kvl_ref[...], gkv_ref[...]).astype(BF16)
    kv = jnp.dot(ckv, wkv_ref[...], preferred_element_type=F32)
    kr = _rope_slab(kpe_ref[...], cm, smlo, smhi, half_m).astype(BF16)

    for h in range(n_heads):
        base = h * MLA_QK_PAD
        nope = slice(base, base + LANES)
        rope = slice(base + LANES, base + 2 * LANES)
        qmo_ref[:, nope] = (q[:, nope] * m_scale).astype(BF16)
        qmo_ref[:, rope] = (_rope_slab(q[:, rope], cm, smlo, smhi, half_m) * m_scale).astype(BF16)
        kmo_ref[:, nope] = kv[:, nope].astype(BF16)
        kmo_ref[:, rope] = kr
        vmo_ref[:, h * LANES:(h + 1) * LANES] = kv[:, rope].astype(BF16)


def _post_call(proj, kpe, tabs, gq, gkv, wq_pad, wkv, n_heads, tm=256):
    t = proj.shape[0]
    dw = n_heads * DIFF_V_DIM
    q_rank = gq.shape[1]
    kv_rank = gkv.shape[1]
    qk_w = n_heads * MLA_QK_PAD
    v_w = n_heads * MLA_V_DIM
    row = lambda w, c: pl.BlockSpec((tm, w), lambda i, c=c: (i, c))
    full = lambda a: pl.BlockSpec(a.shape, lambda i: (0, 0))
    in_specs = [row(dw, 0), row(dw, 1), row(dw, 2),
                row(q_rank, 3 * dw // q_rank),
                row(kv_rank, (3 * dw + q_rank) // kv_rank),
                row(LANES, 0)]
    in_specs += [row(LANES, 0)] * 6
    in_specs += [full(gq), full(gkv), full(wq_pad), full(wkv)]
    out_shape = (jax.ShapeDtypeStruct((t, dw), BF16),) * 3 + (
        jax.ShapeDtypeStruct((t, qk_w), BF16),
        jax.ShapeDtypeStruct((t, qk_w), BF16),
        jax.ShapeDtypeStruct((t, v_w), BF16))
    out_specs = (row(dw, 0),) * 3 + (row(qk_w, 0), row(qk_w, 0), row(v_w, 0))
    return pl.pallas_call(
        functools.partial(_post_kernel, n_heads=n_heads),
        out_shape=out_shape,
        grid=(t // tm,),
        in_specs=in_specs,
        out_specs=out_specs,
        compiler_params=_cparams(("arbitrary",)),
        name="post",
    )(proj, proj, proj, proj, proj, kpe, *tabs, gq, gkv, wq_pad, wkv)


def _softmax_pv(s, v_ref):
    m = jnp.max(s, axis=-1, keepdims=True)
    e = jnp.exp(s - m).astype(BF16)
    v = v_ref[0]
    v1 = jnp.concatenate([v, jnp.ones_like(v)], axis=1)
    o = jnp.dot(e, v1, preferred_element_type=F32)
    dv = v.shape[1]
    return o[:, :dv], o[:, dv:dv + 1]


def _diffattn_kernel(lq1_ref, lk1_ref, lq2_ref, lk2_ref, g_ref, q_ref, k_ref, v_ref, o_ref,
                     *, lambda_init):
    lam = (jnp.exp(jnp.sum(lq1_ref[...] * lk1_ref[...], axis=-1, keepdims=True))
           - jnp.exp(jnp.sum(lq2_ref[...] * lk2_ref[...], axis=-1, keepdims=True))
           + lambda_init)
    q = q_ref[0]
    tq = q.shape[0]
    lane = lax.broadcasted_iota(jnp.int32, q.shape, 1)
    zero = jnp.zeros_like(q)
    q12 = jnp.concatenate([jnp.where(lane < DIFF_HEAD_DIM, q, zero),
                           jnp.where(lane >= DIFF_HEAD_DIM, q, zero)], axis=0)
    s = lax.dot_general(q12, k_ref[0], (((1,), (1,)), ((), ())),
                        preferred_element_type=F32)
    o, l = _softmax_pv(s, v_ref)
    out = o[:tq] / l[:tq] - lam * (o[tq:] / l[tq:])
    ms = jnp.mean(out * out, axis=-1, keepdims=True)
    out = out * lax.rsqrt(ms + NORM_EPS) * g_ref[...] * (1.0 - lambda_init)
    o_ref[0] = out.astype(o_ref.dtype)


def _diffattn_call(lams, g, dq, dk, dv, n_heads, lambda_init, tq=256):
    b, s, _ = dq.shape
    vec = lambda a: pl.BlockSpec(a.shape, lambda bi, h, qi: (0, 0))
    return pl.pallas_call(
        functools.partial(_diffattn_kernel, lambda_init=lambda_init),
        out_shape=jax.ShapeDtypeStruct(dv.shape, BF16),
        grid=(b, n_heads, s // tq),
        in_specs=[vec(lams[0]), vec(lams[1]), vec(lams[2]), vec(lams[3]), vec(g),
                  pl.BlockSpec((1, tq, LANES), lambda bi, h, qi: (bi, qi, h)),
                  pl.BlockSpec((1, s, LANES), lambda bi, h, qi: (bi, 0, h)),
                  pl.BlockSpec((1, s, LANES), lambda bi, h, qi: (bi, 0, h))],
        out_specs=pl.BlockSpec((1, tq, LANES), lambda bi, h, qi: (bi, qi, h)),
        compiler_params=_cparams(("arbitrary", "arbitrary", "arbitrary")),
        name="diffattn",
    )(*lams, g, dq, dk, dv)


def _mlaattn_kernel(q_ref, k_ref, v_ref, o_ref):
    s = lax.dot_general(q_ref[0], k_ref[0], (((1,), (1,)), ((), ())),
                        preferred_element_type=F32)
    o, l = _softmax_pv(s, v_ref)
    o_ref[0] = (o / l).astype(o_ref.dtype)


def _mlaattn_call(qm, km, vm, n_heads, tq=256):
    b, s, _ = qm.shape
    return pl.pallas_call(
        _mlaattn_kernel,
        out_shape=jax.ShapeDtypeStruct(vm.shape, BF16),
        grid=(b, n_heads, s // tq),
        in_specs=[pl.BlockSpec((1, tq, MLA_QK_PAD), lambda bi, h, qi: (bi, qi, h)),
                  pl.BlockSpec((1, s, MLA_QK_PAD), lambda bi, h, qi: (bi, 0, h)),
                  pl.BlockSpec((1, s, MLA_V_DIM), lambda bi, h, qi: (bi, 0, h))],
        out_specs=pl.BlockSpec((1, tq, MLA_V_DIM), lambda bi, h, qi: (bi, qi, h)),
        compiler_params=_cparams(("arbitrary", "arbitrary", "arbitrary")),
        name="mlaattn",
    )(qm, km, vm)


def _outproj_kernel(od_ref, om_ref, wt_ref, wb_ref, x_ref, mod_ref, o_ref):
    acc = jnp.dot(od_ref[...], wt_ref[...].astype(BF16), preferred_element_type=F32)
    acc += jnp.dot(om_ref[...], wb_ref[...].astype(BF16), preferred_element_type=F32)
    o_ref[...] = x_ref[...] + mod_ref[0, 2:3, :] * acc


def _outproj_call(od, om, w_out, x2, mod3, seq, tm=1024, tn=512):
    t, d = x2.shape
    kd = od.shape[1]
    tpb = seq // tm
    return pl.pallas_call(
        _outproj_kernel,
        out_shape=jax.ShapeDtypeStruct((t, d), F32),
        grid=(t // tm, d // tn),
        in_specs=[pl.BlockSpec((tm, kd), lambda i, j: (i, 0)),
                  pl.BlockSpec((tm, om.shape[1]), lambda i, j: (i, 0)),
                  pl.BlockSpec((kd, tn), lambda i, j: (0, j)),
                  pl.BlockSpec((om.shape[1], tn), lambda i, j: (kd // om.shape[1], j)),
                  pl.BlockSpec((tm, tn), lambda i, j: (i, j)),
                  pl.BlockSpec((1, N_MOD, tn), lambda i, j: (i // tpb, 0, j))],
        out_specs=pl.BlockSpec((tm, tn), lambda i, j: (i, j)),
        compiler_params=_cparams(("arbitrary", "arbitrary")),
        name="outproj",
    )(od, om, w_out, w_out, x2, mod3)


def _ffn_kernel(x_ref, mod_ref, g_ref, w1a_ref, w1b_ref, w2p_ref, w2a_ref, gf_ref, o_ref,
                h_ref, ra_ref, rb_ref, *, rows):
    j = pl.program_id(1)
    nj = pl.num_programs(1) - 1
    act = lambda u: jnp.square(jnp.maximum(u, 0.0)).astype(BF16)
    dot = functools.partial(jnp.dot, preferred_element_type=F32)

    @pl.when(j == 0)
    def _():
        _norm_modulate_rows(x_ref, g_ref, mod_ref, 3, 4, h_ref, rows)
        o_ref[...] = jnp.zeros_like(o_ref)
        rb_ref[...] = jnp.zeros_like(rb_ref)

    @pl.when(j < nj)
    def _():
        u0 = dot(h_ref[...], w1a_ref[...].astype(BF16))
        o_ref[...] += dot(rb_ref[...], w2p_ref[...].astype(BF16))
        ra_ref[...] = act(u0)
        u1 = dot(h_ref[...], w1b_ref[...].astype(BF16))
        o_ref[...] += dot(ra_ref[...], w2a_ref[...].astype(BF16))
        rb_ref[...] = act(u1)

    @pl.when(j == nj)
    def _():
        o_ref[...] += dot(rb_ref[...], w2p_ref[...].astype(BF16))
        d = x_ref.shape[1]
        gate = mod_ref[0, 5:6, :]
        gf = gf_ref[...]
        resid = lambda sl: x_ref[sl, :] + gate * o_ref[sl, :]

        def sumsq(sl):
            y = resid(sl)
            return jnp.sum(y * y, axis=-1, keepdims=True)

        def finish(sl, ss):
            o_ref[sl, :] = resid(sl) * lax.rsqrt(ss * (1.0 / d) + NORM_EPS) * gf

        _two_stage_rows(x_ref.shape[0], rows, sumsq, finish)


def _ffn_call(x1, mod3, g, w1, w2, gf, seq, tm=1024, tf=256):
    t, d = x1.shape
    n_chunks = w1.shape[1] // tf
    nj = n_chunks // 2
    tpb = seq // tm
    last_a, last_b = n_chunks - 2, n_chunks - 1
    return pl.pallas_call(
        functools.partial(_ffn_kernel, rows=16),
        out_shape=jax.ShapeDtypeStruct((t, d), F32),
        grid=(t // tm, nj + 1),
        in_specs=[pl.BlockSpec((tm, d), lambda i, j: (i, 0), pipeline_mode=pl.Buffered(1)),
                  pl.BlockSpec((1, N_MOD, d), lambda i, j: (i // tpb, 0, 0)),
                  pl.BlockSpec((1, d), lambda i, j: (0, 0)),
                  pl.BlockSpec((d, tf), lambda i, j: (0, jnp.minimum(2 * j, last_a))),
                  pl.BlockSpec((d, tf), lambda i, j: (0, jnp.minimum(2 * j + 1, last_b))),
                  pl.BlockSpec((tf, d), lambda i, j: (jnp.maximum(2 * j - 1, 0), 0)),
                  pl.BlockSpec((tf, d), lambda i, j: (jnp.minimum(2 * j, last_a), 0)),
                  pl.BlockSpec((1, d), lambda i, j: (0, 0))],
        out_specs=pl.BlockSpec((tm, d), lambda i, j: (i, 0)),
        scratch_shapes=[pltpu.VMEM((tm, d), BF16),
                        pltpu.VMEM((tm, tf), BF16),
                        pltpu.VMEM((tm, tf), BF16)],
        compiler_params=_cparams(("arbitrary", "arbitrary")),
        name="ffn",
    )(x1, mod3, g, w1, w1, w2, w2, gf)


def _rope_tables(positions, rot_dim, theta, group):
    half = rot_dim // 2
    lane = np.arange(LANES)
    g = lane % group
    inv = 1.0 / (theta ** (jnp.arange(0, rot_dim, 2, dtype=F32) / rot_dim))
    inv_lane = inv[g % half]
    ang = positions.astype(F32).reshape(-1, 1) * inv_lane[None, :]
    cos, sin = jnp.cos(ang), jnp.sin(ang)
    rot = jnp.asarray(g < rot_dim)[None, :]
    lo = jnp.asarray(g < half)[None, :]
    hi = jnp.asarray((g >= half) & (g < rot_dim))[None, :]
    return (jnp.where(rot, cos, 1.0), jnp.where(lo, -sin, 0.0), jnp.where(hi, sin, 0.0))


def kernel(x, c, positions, w_ada, b_ada, g_norm_mix, w_in, lambda_q1, lambda_k1, lambda_q2,
           lambda_k2, g_diff_sub, g_q_a, w_q_b, g_kv_a, w_kv_b, w_out, g_norm_ffn, w_ff1, w_ff2,
           g_final):
    b, s, d = x.shape
    depth = w_ada.shape[0]
    t = b * s
    q_rank = g_q_a.shape[1]
    kv_rank = g_kv_a.shape[1]
    n_mla = w_kv_b.shape[2] // (MLA_NOPE_DIM + MLA_V_DIM)
    n_diff = (w_in.shape[2] - q_rank - kv_rank - MLA_ROPE_DIM) // (3 * DIFF_V_DIM)
    assert n_diff == n_mla
    n_main = w_in.shape[2] - MLA_ROPE_DIM

    tabs = (_rope_tables(positions, DIFF_ROT_DIM, ROPE_THETA, DIFF_HEAD_DIM)
            + _rope_tables(positions, MLA_ROPE_DIM, MLA_ROPE_THETA, LANES))
    c_pad = jnp.pad(c, ((0, 8 - b), (0, 0)))
    x2 = x.reshape(t, d)

    for l in range(depth):
        lambda_init = 0.8 - 0.6 * float(np.exp(-0.3 * l))
        mod = _mod_call(c_pad, w_ada[l], b_ada[l][None, :])
        mod3 = mod[:b].reshape(b, N_MOD, d)

        w_pe = jnp.pad(w_in[l][:, n_main:], ((0, 0), (0, LANES - MLA_ROPE_DIM))).astype(BF16)
        proj, kpe = _inproj_call(x2, mod3, g_norm_mix[l][None, :], w_in[l], w_pe, n_main, s)

        wq_pad = jnp.pad(w_q_b[l].reshape(q_rank, n_mla, MLA_NOPE_DIM + MLA_ROPE_DIM),
                         ((0, 0), (0, 0), (0, MLA_QK_PAD - MLA_NOPE_DIM - MLA_ROPE_DIM))
                         ).reshape(q_rank, n_mla * MLA_QK_PAD).astype(BF16)
        dq, dk, dv, qm, km, vm = _post_call(
            proj, kpe, tabs, g_q_a[l][None, :], g_kv_a[l][None, :], wq_pad,
            w_kv_b[l].astype(BF16), n_mla)

        shp = lambda a: a.reshape(b, s, a.shape[1])
        lams = tuple(v[l][None, :] for v in (lambda_q1, lambda_k1, lambda_q2, lambda_k2))
        o_diff = _diffattn_call(lams, g_diff_sub[l][None, :], shp(dq), shp(dk), shp(dv),
                                n_diff, lambda_init)
        o_mla = _mlaattn_call(shp(qm), shp(km), shp(vm), n_mla)

        x2 = _outproj_call(o_diff.reshape(t, -1), o_mla.reshape(t, -1), w_out[l], x2, mod3, s)

        last = l == depth - 1
        assert last, "final rmsnorm is fused into the last layer's MLP kernel"
        x2 = _ffn_call(x2, mod3, g_norm_ffn[l][None, :], w_ff1[l], w_ff2[l], g_final[None, :], s)

    return x2.reshape(b, s, d)
```

```python
import functools
import math

import jax
import jax.numpy as jnp
import numpy as np
from jax import lax
from jax.experimental import pallas as pl
from jax.experimental.pallas import tpu as pltpu

F32 = jnp.float32
BF16 = jnp.bfloat16

LANES = 128
NORM_EPS = 1e-6
N_MOD = 6

DIFF_HEAD_DIM = 64
DIFF_V_DIM = 2 * DIFF_HEAD_DIM
DIFF_ROT_DIM = DIFF_HEAD_DIM // 4
ROPE_THETA = 500000.0
MLA_V_DIM = 128
MLA_NOPE_DIM = 128
MLA_ROPE_DIM = 64
MLA_ROPE_THETA = 10000.0
MLA_QK_PAD = 256

DIFF_SCALE = DIFF_HEAD_DIM ** -0.5 * math.log2(math.e)
MLA_SCALE = (MLA_NOPE_DIM + MLA_ROPE_DIM) ** -0.5 * math.log2(math.e)

VMEM_LIMIT = 56 * 1024 * 1024


def _cparams(sem):
    return pltpu.CompilerParams(dimension_semantics=sem, vmem_limit_bytes=VMEM_LIMIT)


def _dot(a, b):
    return jnp.dot(a, b, preferred_element_type=F32)


def _dot_nt(a, b):
    return lax.dot_general(a, b, (((1,), (1,)), ((), ())), preferred_element_type=F32)


def _mod_kernel(c_ref, w_ref, b_ref, o_ref):
    c = c_ref[...]
    sc = c / (1.0 + jnp.exp(-c))
    o_ref[...] = _dot(sc.astype(BF16), w_ref[...].astype(BF16)) + b_ref[...]


def _mod_call(c_pad, w_ada, b_ada, tn=1024):
    m, d = c_pad.shape
    n = w_ada.shape[1]
    return pl.pallas_call(
        _mod_kernel,
        out_shape=jax.ShapeDtypeStruct((m, n), F32),
        grid=(n // tn,),
        in_specs=[pl.BlockSpec((m, d), lambda j: (0, 0)),
                  pl.BlockSpec((d, tn), lambda j: (0, j)),
                  pl.BlockSpec((1, tn), lambda j: (0, j))],
        out_specs=pl.BlockSpec((m, tn), lambda j: (0, j)),
        compiler_params=_cparams(("arbitrary",)),
        name="mod",
    )(c_pad, w_ada, b_ada)


def _two_stage_rows(n_rows, rows, sumsq, finish):
    chunk = lambda r: pl.ds(pl.multiple_of(r * rows, rows), rows)

    def body(r, ss_prev):
        ss = sumsq(chunk(r))
        finish(chunk(r - 1), ss_prev)
        return ss

    n = n_rows // rows
    ss_last = lax.fori_loop(1, n, body, sumsq(chunk(0)))
    finish(chunk(n - 1), ss_last)


def _norm_modulate_rows(x_ref, g_ref, mod_ref, shift_row, scale_row, h_ref, rows, also=None):
    tm, d = x_ref.shape
    shift = mod_ref[0, shift_row:shift_row + 1, :]
    gs = g_ref[...] * (1.0 + mod_ref[0, scale_row:scale_row + 1, :])

    def sumsq(sl):
        x = x_ref[sl, :]
        return jnp.sum(x * x, axis=-1, keepdims=True)

    def finish(sl, ss):
        rs = lax.rsqrt(ss * (1.0 / d) + NORM_EPS)
        h_ref[sl, :] = (x_ref[sl, :] * rs * gs + shift).astype(BF16)
        if also is not None:
            also(sl)

    _two_stage_rows(tm, rows, sumsq, finish)


def _rope_slab(x, c, s_lo, s_hi, half):
    return x * c + pltpu.roll(x, half, 1) * s_hi + pltpu.roll(x, LANES - half, 1) * s_lo


def _rms_rows(x, g):
    ms = jnp.mean(x * x, axis=-1, keepdims=True)
    return x * lax.rsqrt(ms + NORM_EPS) * g


def _inproj_kernel(x_ref, mod_ref, g_ref, w_ref, wql_ref, wkvl_ref, wpe_ref,
                   cd_ref, slo_ref, shi_ref, qkv_ref, lat_ref, h_ref, *, rows, nb):
    j = pl.program_id(1)

    @pl.when(j == 0)
    def _():
        _norm_modulate_rows(x_ref, g_ref, mod_ref, 0, 1, h_ref, rows)
        h = h_ref[...]
        q_rank, kv_rank, pe = wql_ref.shape[0], wkvl_ref.shape[0], wpe_ref.shape[0]
        lat_ref[:, :q_rank] = _dot_nt(h, wql_ref[...].astype(BF16))
        lat_ref[:, q_rank:q_rank + kv_rank] = _dot_nt(h, wkvl_ref[...].astype(BF16))
        lat_ref[:, q_rank + kv_rank:q_rank + kv_rank + pe] = _dot_nt(h, wpe_ref[...].astype(BF16))
        pad = lat_ref.shape[1] - (q_rank + kv_rank + pe)
        lat_ref[:, q_rank + kv_rank + pe:] = jnp.zeros((lat_ref.shape[0], pad), F32)

    r = _dot_nt(h_ref[...], w_ref[...].astype(BF16))
    rot = (j < 2 * nb).astype(F32)
    c = 1.0 + rot * (cd_ref[...] - 1.0)
    s_lo = rot * slo_ref[...]
    s_hi = rot * shi_ref[...]
    scale = jnp.where(j < nb, DIFF_SCALE, 1.0)
    for k in range(r.shape[1] // LANES):
        sl = slice(k * LANES, (k + 1) * LANES)
        qkv_ref[:, sl] = (_rope_slab(r[:, sl], c, s_lo, s_hi, DIFF_ROT_DIM // 2) * scale).astype(BF16)


def _inproj_call(x2, mod3, g, w_t, tabs_d, dw, q_rank, kv_rank, seq, tm=1024, tn=512):
    t, d = x2.shape
    tpb = seq // tm
    pe = MLA_ROPE_DIM
    lat_w = q_rank + kv_rank + LANES
    nb = dw // tn
    assert dw % tn == 0 and seq % tm == 0
    assert (3 * dw) % q_rank == 0 and (3 * dw + q_rank) % kv_rank == 0
    assert (3 * dw + q_rank + kv_rank) % pe == 0 and w_t.shape[0] == 3 * dw + q_rank + kv_rank + pe
    once = lambda rows_, idx: pl.BlockSpec((rows_, d), lambda i, j: (idx, 0),
                                           pipeline_mode=pl.Buffered(1))
    tab = pl.BlockSpec((tm, LANES), lambda i, j: (i, 0))
    return pl.pallas_call(
        functools.partial(_inproj_kernel, rows=16, nb=nb),
        out_shape=(jax.ShapeDtypeStruct((t, 3 * dw), BF16),
                   jax.ShapeDtypeStruct((t, lat_w), F32)),
        grid=(t // tm, 3 * nb),
        in_specs=[pl.BlockSpec((tm, d), lambda i, j: (i, 0), pipeline_mode=pl.Buffered(1)),
                  pl.BlockSpec((1, N_MOD, d), lambda i, j: (i // tpb, 0, 0)),
                  pl.BlockSpec((1, d), lambda i, j: (0, 0)),
                  pl.BlockSpec((tn, d), lambda i, j: (j, 0)),
                  once(q_rank, 3 * dw // q_rank),
                  once(kv_rank, (3 * dw + q_rank) // kv_rank),
                  once(pe, (3 * dw + q_rank + kv_rank) // pe),
                  tab, tab, tab],
        out_specs=(pl.BlockSpec((tm, tn), lambda i, j: (i, j)),
                   pl.BlockSpec((tm, lat_w), lambda i, j: (i, 0))),
        scratch_shapes=[pltpu.VMEM((tm, d), BF16)],
        compiler_params=_cparams(("arbitrary", "arbitrary")),
        name="inproj",
    )(x2, mod3, g, w_t, w_t, w_t, w_t, *tabs_d)


def _latent_kernel(ql_ref, kvl_ref, kpe_ref, cm_ref, smlo_ref, smhi_ref,
                   gq_ref, gkv_ref, wq_ref, wkv_ref, qmo_ref, kmo_ref, vmo_ref, *, n_heads):
    cm, smlo, smhi = cm_ref[...], smlo_ref[...], smhi_ref[...]
    half_m = MLA_ROPE_DIM // 2

    q = _dot(_rms_rows(ql_ref[...], gq_ref[...]).astype(BF16), wq_ref[...])
    kv = _dot(_rms_rows(kvl_ref[...], gkv_ref[...]).astype(BF16), wkv_ref[...])
    kr = _rope_slab(kpe_ref[...], cm, smlo, smhi, half_m).astype(BF16)

    for h in range(n_heads):
        base = h * MLA_QK_PAD
        nope = slice(base, base + LANES)
        rope = slice(base + LANES, base + 2 * LANES)
        qmo_ref[:, nope] = (q[:, nope] * MLA_SCALE).astype(BF16)
        qmo_ref[:, rope] = (_rope_slab(q[:, rope], cm, smlo, smhi, half_m) * MLA_SCALE).astype(BF16)
        kmo_ref[:, nope] = kv[:, nope].astype(BF16)
        kmo_ref[:, rope] = kr
        vmo_ref[:, h * LANES:(h + 1) * LANES] = kv[:, rope].astype(BF16)


def _latent_call(lat, tabs_m, gq, gkv, wq_pad, wkv, n_heads, tm=256):
    t = lat.shape[0]
    q_rank = gq.shape[1]
    kv_rank = gkv.shape[1]
    qk_w = n_heads * MLA_QK_PAD
    v_w = n_heads * MLA_V_DIM
    assert q_rank % kv_rank == 0 and (q_rank + kv_rank) % LANES == 0
    row = lambda w, c: pl.BlockSpec((tm, w), lambda i, c=c: (i, c))
    full = lambda a: pl.BlockSpec(a.shape, lambda i: (0, 0))
    in_specs = [row(q_rank, 0), row(kv_rank, q_rank // kv_rank),
                row(LANES, (q_rank + kv_rank) // LANES)]
    in_specs += [row(LANES, 0)] * 3
    in_specs += [full(gq), full(gkv), full(wq_pad), full(wkv)]
    out_shape = (jax.ShapeDtypeStruct((t, qk_w), BF16),
                 jax.ShapeDtypeStruct((t, qk_w), BF16),
                 jax.ShapeDtypeStruct((t, v_w), BF16))
    return pl.pallas_call(
        functools.partial(_latent_kernel, n_heads=n_heads),
        out_shape=out_shape,
        grid=(t // tm,),
        in_specs=in_specs,
        out_specs=(row(qk_w, 0), row(qk_w, 0), row(v_w, 0)),
        compiler_params=_cparams(("arbitrary",)),
        name="latent",
    )(lat, lat, lat, *tabs_m, gq, gkv, wq_pad, wkv)


def _numerators(s_ref, e_ref):
    s = s_ref[...]
    e_ref[...] = jnp.exp2(s - jnp.max(s, axis=-1, keepdims=True)).astype(BF16)


def _weighted_values(e_ref, v1):
    o = _dot(e_ref[...], v1)
    dv = v1.shape[1] // 2
    return o[:, :dv], o[:, dv:dv + 1]


def _attn_step(q_ref, k_ref, v_ref, o_ref, s_refs, e_refs, make_lhs, finish):
    tq = o_ref.shape[1] // 2

    @pl.when(pl.program_id(0) == 0)
    def _():
        s_refs[1][...] = jnp.zeros_like(s_refs[1])
        e_refs[0][...] = jnp.zeros_like(e_refs[0])

    v = v_ref[0]
    v1 = jnp.concatenate([v, jnp.ones_like(v)], axis=1)
    s_refs[0][...] = _dot_nt(make_lhs(q_ref[0, :tq]), k_ref[0])
    _numerators(s_refs[1], e_refs[1])
    o_ref[0, :tq] = finish(*_weighted_values(e_refs[0], v1)).astype(o_ref.dtype)
    s_refs[1][...] = _dot_nt(make_lhs(q_ref[0, tq:]), k_ref[0])
    _numerators(s_refs[0], e_refs[0])
    o_ref[0, tq:] = finish(*_weighted_values(e_refs[1], v1)).astype(o_ref.dtype)


def _attn_specs(b, n_heads, seq, tq, qk_width, v_width, q_col=0, k_col=0, v_col=0):
    n_pairs = seq // (2 * tq)
    total = b * n_heads * n_pairs

    def decode(g):
        return g // (n_heads * n_pairs), (g // n_pairs) % n_heads, g % n_pairs

    cur = lambda g: decode(jnp.minimum(g, total - 1))
    prev = lambda g: decode(jnp.maximum(g - 1, 0))
    q_spec = pl.BlockSpec((1, 2 * tq, qk_width),
                          lambda g: (cur(g)[0], cur(g)[2], q_col + cur(g)[1]))
    k_spec = pl.BlockSpec((1, seq, qk_width), lambda g: (cur(g)[0], 0, k_col + cur(g)[1]))
    v_spec = pl.BlockSpec((1, seq, v_width), lambda g: (prev(g)[0], 0, v_col + prev(g)[1]))
    o_spec = pl.BlockSpec((1, 2 * tq, v_width), lambda g: (prev(g)[0], prev(g)[2], prev(g)[1]))
    return total + 1, q_spec, k_spec, v_spec, o_spec


def _attn_scratch(rows, seq):
    return [pltpu.VMEM((rows, seq), F32), pltpu.VMEM((rows, seq), F32),
            pltpu.VMEM((rows, seq), BF16), pltpu.VMEM((rows, seq), BF16)]


def _diffattn_kernel(lq1_ref, lk1_ref, lq2_ref, lk2_ref, g_ref, q_ref, k_ref, v_ref, o_ref,
                     s0_ref, s1_ref, e0_ref, e1_ref, *, lambda_init):
    lam = (jnp.exp(jnp.sum(lq1_ref[...] * lk1_ref[...], axis=-1, keepdims=True))
           - jnp.exp(jnp.sum(lq2_ref[...] * lk2_ref[...], axis=-1, keepdims=True))
           + lambda_init)
    gain = g_ref[...] * (1.0 - lambda_init)

    def make_lhs(q):
        lane = lax.broadcasted_iota(jnp.int32, q.shape, 1)
        zero = jnp.zeros_like(q)
        return jnp.concatenate([jnp.where(lane < DIFF_HEAD_DIM, q, zero),
                                jnp.where(lane >= DIFF_HEAD_DIM, q, zero)], axis=0)

    def finish(o, l):
        tq = o.shape[0] // 2
        out = o[:tq] / l[:tq] - lam * (o[tq:] / l[tq:])
        ms = jnp.mean(out * out, axis=-1, keepdims=True)
        return out * lax.rsqrt(ms + NORM_EPS) * gain

    _attn_step(q_ref, k_ref, v_ref, o_ref, (s0_ref, s1_ref), (e0_ref, e1_ref), make_lhs, finish)


def _diffattn_call(lams, g, qkv, n_heads, lambda_init, tq=256):
    b, s, _ = qkv.shape
    steps, q_spec, k_spec, v_spec, o_spec = _attn_specs(
        b, n_heads, s, tq, LANES, LANES, q_col=0, k_col=n_heads, v_col=2 * n_heads)
    vec = lambda a: pl.BlockSpec(a.shape, lambda g: (0, 0))
    return pl.pallas_call(
        functools.partial(_diffattn_kernel, lambda_init=lambda_init),
        out_shape=jax.ShapeDtypeStruct((b, s, n_heads * DIFF_V_DIM), BF16),
        grid=(steps,),
        in_specs=[vec(lams[0]), vec(lams[1]), vec(lams[2]), vec(lams[3]), vec(g),
                  q_spec, k_spec, v_spec],
        out_specs=o_spec,
        scratch_shapes=_attn_scratch(2 * tq, s),
        compiler_params=_cparams(("arbitrary",)),
        name="diffattn",
    )(*lams, g, qkv, qkv, qkv)


def _mlaattn_kernel(q_ref, k_ref, v_ref, o_ref, s0_ref, s1_ref, e0_ref, e1_ref):
    _attn_step(q_ref, k_ref, v_ref, o_ref, (s0_ref, s1_ref), (e0_ref, e1_ref),
               lambda q: q, lambda o, l: o / l)


def _mlaattn_call(qm, km, vm, n_heads, tq=512):
    b, s, _ = qm.shape
    steps, q_spec, k_spec, v_spec, o_spec = _attn_specs(b, n_heads, s, tq, MLA_QK_PAD, MLA_V_DIM)
    return pl.pallas_call(
        _mlaattn_kernel,
        out_shape=jax.ShapeDtypeStruct(vm.shape, BF16),
        grid=(steps,),
        in_specs=[q_spec, k_spec, v_spec],
        out_specs=o_spec,
        scratch_shapes=_attn_scratch(tq, s),
        compiler_params=_cparams(("arbitrary",)),
        name="mlaattn",
    )(qm, km, vm)


def _outproj_kernel(od_ref, om_ref, wt_ref, wb_ref, x_ref, mod_ref, o_ref):
    acc = _dot(od_ref[...], wt_ref[...].astype(BF16))
    acc += _dot(om_ref[...], wb_ref[...].astype(BF16))
    o_ref[...] = x_ref[...] + mod_ref[0, 2:3, :] * acc


def _outproj_call(od, om, w_out, x2, mod3, seq, tm=2048, tn=512):
    t, d = x2.shape
    kd = od.shape[1]
    tpb = seq // tm
    return pl.pallas_call(
        _outproj_kernel,
        out_shape=jax.ShapeDtypeStruct((t, d), F32),
        grid=(t // tm, d // tn),
        in_specs=[pl.BlockSpec((tm, kd), lambda i, j: (i, 0)),
                  pl.BlockSpec((tm, om.shape[1]), lambda i, j: (i, 0)),
                  pl.BlockSpec((kd, tn), lambda i, j: (0, j)),
                  pl.BlockSpec((om.shape[1], tn), lambda i, j: (kd // om.shape[1], j)),
                  pl.BlockSpec((tm, tn), lambda i, j: (i, j)),
                  pl.BlockSpec((1, N_MOD, tn), lambda i, j: (i // tpb, 0, j))],
        out_specs=pl.BlockSpec((tm, tn), lambda i, j: (i, j)),
        compiler_params=_cparams(("arbitrary", "arbitrary")),
        name="outproj",
    )(od, om, w_out, w_out, x2, mod3)


def _ffn_kernel(x_ref, mod_ref, g_ref, w1a_ref, w1b_ref, w2p_ref, w2a_ref, gf_ref, o_ref,
                h_ref, ra_ref, rb_ref, *, rows):
    j = pl.program_id(1)
    nj = pl.num_programs(1) - 1
    act = lambda u: jnp.square(jnp.maximum(u, 0.0)).astype(BF16)

    @pl.when(j == 0)
    def _():
        def clear(sl):
            o_ref[sl, :] = jnp.zeros((rows, o_ref.shape[1]), o_ref.dtype)
            rb_ref[sl, :] = jnp.zeros((rows, rb_ref.shape[1]), rb_ref.dtype)

        _norm_modulate_rows(x_ref, g_ref, mod_ref, 3, 4, h_ref, rows, also=clear)

    @pl.when(j < nj)
    def _():
        u0 = _dot(h_ref[...], w1a_ref[...].astype(BF16))
        o_ref[...] += _dot(rb_ref[...], w2p_ref[...].astype(BF16))
        ra_ref[...] = act(u0)
        u1 = _dot(h_ref[...], w1b_ref[...].astype(BF16))
        o_ref[...] += _dot(ra_ref[...], w2a_ref[...].astype(BF16))
        rb_ref[...] = act(u1)

    @pl.when(j == nj)
    def _():
        o_ref[...] += _dot(rb_ref[...], w2p_ref[...].astype(BF16))
        d = x_ref.shape[1]
        gate = mod_ref[0, 5:6, :]
        gf = gf_ref[...]
        resid = lambda sl: x_ref[sl, :] + gate * o_ref[sl, :]

        def sumsq(sl):
            y = resid(sl)
            return jnp.sum(y * y, axis=-1, keepdims=True)

        def finish(sl, ss):
            o_ref[sl, :] = resid(sl) * lax.rsqrt(ss * (1.0 / d) + NORM_EPS) * gf

        _two_stage_rows(x_ref.shape[0], rows, sumsq, finish)


def _ffn_call(x1, mod3, g, w1, w2, gf, seq, tm=1024, tf=256):
    t, d = x1.shape
    n_chunks = w1.shape[1] // tf
    nj = n_chunks // 2
    tpb = seq // tm
    last_a, last_b = n_chunks - 2, n_chunks - 1
    return pl.pallas_call(
        functools.partial(_ffn_kernel, rows=16),
        out_shape=jax.ShapeDtypeStruct((t, d), F32),
        grid=(t // tm, nj + 1),
        in_specs=[pl.BlockSpec((tm, d), lambda i, j: (i, 0), pipeline_mode=pl.Buffered(1)),
                  pl.BlockSpec((1, N_MOD, d), lambda i, j: (i // tpb, 0, 0)),
                  pl.BlockSpec((1, d), lambda i, j: (0, 0)),
                  pl.BlockSpec((d, tf), lambda i, j: (0, jnp.minimum(2 * j, last_a))),
                  pl.BlockSpec((d, tf), lambda i, j: (0, jnp.minimum(2 * j + 1, last_b))),
                  pl.BlockSpec((tf, d), lambda i, j: (jnp.maximum(2 * j - 1, 0), 0)),
                  pl.BlockSpec((tf, d), lambda i, j: (jnp.minimum(2 * j, last_a), 0)),
                  pl.BlockSpec((1, d), lambda i, j: (0, 0))],
        out_specs=pl.BlockSpec((tm, d), lambda i, j: (i, 0)),
        scratch_shapes=[pltpu.VMEM((tm, d), BF16),
                        pltpu.VMEM((tm, tf), BF16),
                        pltpu.VMEM((tm, tf), BF16)],
        compiler_params=_cparams(("arbitrary", "arbitrary")),
        name="ffn",
    )(x1, mod3, g, w1, w1, w2, w2, gf)


def _rope_tables(positions, rot_dim, theta, group):
    half = rot_dim // 2
    inv = 1.0 / (theta ** (jnp.arange(0, rot_dim, 2, dtype=F32) / rot_dim))
    ang = positions.astype(F32).reshape(-1, 1) * inv[None, :]
    cos, sin = jnp.cos(ang), jnp.sin(ang)
    t = ang.shape[0]
    rest = group - rot_dim
    widen = lambda lo, hi, fill: jnp.tile(
        jnp.concatenate([lo, hi, jnp.full((t, rest), fill, F32)], axis=1), (1, LANES // group))
    zeros = jnp.zeros_like(sin)
    return (widen(cos, cos, 1.0), widen(-sin, zeros, 0.0), widen(zeros, sin, 0.0))


def kernel(x, c, positions, w_ada, b_ada, g_norm_mix, w_in, lambda_q1, lambda_k1, lambda_q2,
           lambda_k2, g_diff_sub, g_q_a, w_q_b, g_kv_a, w_kv_b, w_out, g_norm_ffn, w_ff1, w_ff2,
           g_final):
    b, s, d = x.shape
    depth = w_ada.shape[0]
    t = b * s
    q_rank = g_q_a.shape[1]
    kv_rank = g_kv_a.shape[1]
    n_mla = w_kv_b.shape[2] // (MLA_NOPE_DIM + MLA_V_DIM)
    n_diff = (w_in.shape[2] - q_rank - kv_rank - MLA_ROPE_DIM) // (3 * DIFF_V_DIM)
    assert n_diff == n_mla

    tabs_d = _rope_tables(positions, DIFF_ROT_DIM, ROPE_THETA, DIFF_HEAD_DIM)
    tabs_m = _rope_tables(positions, MLA_ROPE_DIM, MLA_ROPE_THETA, LANES)
    c_pad = jnp.pad(c, ((0, 8 - b), (0, 0)))
    x2 = x.reshape(t, d)

    for l in range(depth):
        lambda_init = 0.8 - 0.6 * float(np.exp(-0.3 * l))
        mod = _mod_call(c_pad, w_ada[l], b_ada[l][None, :])
        mod3 = mod[:b].reshape(b, N_MOD, d)

        qkv, lat = _inproj_call(x2, mod3, g_norm_mix[l][None, :], jnp.swapaxes(w_in[l], 0, 1),
                                tabs_d, n_diff * DIFF_V_DIM, q_rank, kv_rank, s)

        wq_pad = jnp.pad(w_q_b[l].reshape(q_rank, n_mla, MLA_NOPE_DIM + MLA_ROPE_DIM),
                         ((0, 0), (0, 0), (0, MLA_QK_PAD - MLA_NOPE_DIM - MLA_ROPE_DIM))
                         ).reshape(q_rank, n_mla * MLA_QK_PAD).astype(BF16)
        qm, km, vm = _latent_call(lat, tabs_m, g_q_a[l][None, :], g_kv_a[l][None, :], wq_pad,
                                  w_kv_b[l].astype(BF16), n_mla)

        shp = lambda a: a.reshape(b, s, a.shape[1])
        lams = tuple(v[l][None, :] for v in (lambda_q1, lambda_k1, lambda_q2, lambda_k2))
        o_diff = _diffattn_call(lams, g_diff_sub[l][None, :], shp(qkv), n_diff, lambda_init)
        o_mla = _mlaattn_call(shp(qm), shp(km), shp(vm), n_mla)

        x2 = _outproj_call(o_diff.reshape(t, -1), o_mla.reshape(t, -1), w_out[l], x2, mod3, s)

        last = l == depth - 1
        assert last, "final rmsnorm is fused into the last layer's MLP kernel"
        x2 = _ffn_call(x2, mod3, g_norm_ffn[l][None, :], w_ff1[l], w_ff2[l], g_final[None, :], s)

    return x2.reshape(b, s, d)
```

```python
import functools
import math

import jax
import jax.numpy as jnp
import numpy as np
from jax import lax
from jax.experimental import pallas as pl
from jax.experimental.pallas import tpu as pltpu

F32 = jnp.float32
BF16 = jnp.bfloat16

LANES = 128
NORM_EPS = 1e-6
N_MOD = 6

DIFF_HEAD_DIM = 64
DIFF_V_DIM = 2 * DIFF_HEAD_DIM
DIFF_ROT_DIM = DIFF_HEAD_DIM // 4
ROPE_THETA = 500000.0
MLA_V_DIM = 128
MLA_NOPE_DIM = 128
MLA_ROPE_DIM = 64
MLA_ROPE_THETA = 10000.0
MLA_QK_PAD = 256

DIFF_SCALE = DIFF_HEAD_DIM ** -0.5 * math.log2(math.e)
MLA_SCALE = (MLA_NOPE_DIM + MLA_ROPE_DIM) ** -0.5 * math.log2(math.e)

VMEM_LIMIT = 56 * 1024 * 1024


def _cparams(sem):
    return pltpu.CompilerParams(dimension_semantics=sem, vmem_limit_bytes=VMEM_LIMIT)


def _dot(a, b):
    return jnp.dot(a, b, preferred_element_type=F32)


def _dot_nt(a, b):
    return lax.dot_general(a, b, (((1,), (1,)), ((), ())), preferred_element_type=F32)


def _mod_kernel(c_ref, w_ref, b_ref, o_ref):
    c = c_ref[...]
    sc = c / (1.0 + jnp.exp(-c))
    o_ref[...] = _dot(sc.astype(BF16), w_ref[...].astype(BF16)) + b_ref[...]


def _mod_call(c_pad, w_ada, b_ada, tn=1024):
    m, d = c_pad.shape
    n = w_ada.shape[1]
    return pl.pallas_call(
        _mod_kernel,
        out_shape=jax.ShapeDtypeStruct((m, n), F32),
        grid=(n // tn,),
        in_specs=[pl.BlockSpec((m, d), lambda j: (0, 0)),
                  pl.BlockSpec((d, tn), lambda j: (0, j)),
                  pl.BlockSpec((1, tn), lambda j: (0, j))],
        out_specs=pl.BlockSpec((m, tn), lambda j: (0, j)),
        compiler_params=_cparams(("arbitrary",)),
        name="mod",
    )(c_pad, w_ada, b_ada)


def _two_stage_rows(n_rows, rows, sumsq, finish):
    chunk = lambda r: pl.ds(pl.multiple_of(r * rows, rows), rows)

    def body(r, ss_prev):
        ss = sumsq(chunk(r))
        finish(chunk(r - 1), ss_prev)
        return ss

    n = n_rows // rows
    ss_last = lax.fori_loop(1, n, body, sumsq(chunk(0)))
    finish(chunk(n - 1), ss_last)


def _norm_modulate_rows(x_ref, g_ref, mod_ref, shift_row, scale_row, h_ref, rows, also=None):
    tm, d = x_ref.shape
    shift = mod_ref[0, shift_row:shift_row + 1, :]
    gs = g_ref[...] * (1.0 + mod_ref[0, scale_row:scale_row + 1, :])

    def sumsq(sl):
        x = x_ref[sl, :]
        return jnp.sum(x * x, axis=-1, keepdims=True)

    def finish(sl, ss):
        rs = lax.rsqrt(ss * (1.0 / d) + NORM_EPS)
        h_ref[sl, :] = (x_ref[sl, :] * rs * gs + shift).astype(BF16)
        if also is not None:
            also(sl)

    _two_stage_rows(tm, rows, sumsq, finish)


def _rope_slab(x, c, s_lo, s_hi, half):
    return x * c + pltpu.roll(x, half, 1) * s_hi + pltpu.roll(x, LANES - half, 1) * s_lo


def _rms_rows(x, g):
    ms = jnp.mean(x * x, axis=-1, keepdims=True)
    return x * lax.rsqrt(ms + NORM_EPS) * g


def _inproj_kernel(x_ref, mod_ref, g_ref, w_ref, wql_ref, wkvl_ref, wpe_ref,
                   cd_ref, slo_ref, shi_ref, qkv_ref, lat_ref, h_ref, *, rows, nb):
    j = pl.program_id(1)

    @pl.when(j == 0)
    def _():
        _norm_modulate_rows(x_ref, g_ref, mod_ref, 0, 1, h_ref, rows)
        h = h_ref[...]
        q_rank, kv_rank, pe = wql_ref.shape[0], wkvl_ref.shape[0], wpe_ref.shape[0]
        lat_ref[:, :q_rank] = _dot_nt(h, wql_ref[...].astype(BF16))
        lat_ref[:, q_rank:q_rank + kv_rank] = _dot_nt(h, wkvl_ref[...].astype(BF16))
        lat_ref[:, q_rank + kv_rank:q_rank + kv_rank + pe] = _dot_nt(h, wpe_ref[...].astype(BF16))
        pad = lat_ref.shape[1] - (q_rank + kv_rank + pe)
        lat_ref[:, q_rank + kv_rank + pe:] = jnp.zeros((lat_ref.shape[0], pad), F32)

    r = _dot_nt(h_ref[...], w_ref[...].astype(BF16))
    rot = (j < 2 * nb).astype(F32)
    c = 1.0 + rot * (cd_ref[...] - 1.0)
    s_lo = rot * slo_ref[...]
    s_hi = rot * shi_ref[...]
    scale = jnp.where(j < nb, DIFF_SCALE, 1.0)
    for k in range(r.shape[1] // LANES):
        sl = slice(k * LANES, (k + 1) * LANES)
        qkv_ref[:, sl] = (_rope_slab(r[:, sl], c, s_lo, s_hi, DIFF_ROT_DIM // 2) * scale).astype(BF16)


def _inproj_call(x2, mod3, g, w_t, tabs_d, dw, q_rank, kv_rank, seq, tm=1024, tn=512):
    t, d = x2.shape
    tpb = seq // tm
    pe = MLA_ROPE_DIM
    lat_w = q_rank + kv_rank + LANES
    nb = dw // tn
    assert dw % tn == 0 and seq % tm == 0
    assert (3 * dw) % q_rank == 0 and (3 * dw + q_rank) % kv_rank == 0
    assert (3 * dw + q_rank + kv_rank) % pe == 0 and w_t.shape[0] == 3 * dw + q_rank + kv_rank + pe
    once = lambda rows_, idx: pl.BlockSpec((rows_, d), lambda i, j: (idx, 0),
                                           pipeline_mode=pl.Buffered(1))
    tab = pl.BlockSpec((tm, LANES), lambda i, j: (i, 0))
    return pl.pallas_call(
        functools.partial(_inproj_kernel, rows=32, nb=nb),
        out_shape=(jax.ShapeDtypeStruct((t, 3 * dw), BF16),
                   jax.ShapeDtypeStruct((t, lat_w), F32)),
        grid=(t // tm, 3 * nb),
        in_specs=[pl.BlockSpec((tm, d), lambda i, j: (i, 0), pipeline_mode=pl.Buffered(1)),
                  pl.BlockSpec((1, N_MOD, d), lambda i, j: (i // tpb, 0, 0)),
                  pl.BlockSpec((1, d), lambda i, j: (0, 0)),
                  pl.BlockSpec((tn, d), lambda i, j: (j, 0)),
                  once(q_rank, 3 * dw // q_rank),
                  once(kv_rank, (3 * dw + q_rank) // kv_rank),
                  once(pe, (3 * dw + q_rank + kv_rank) // pe),
                  tab, tab, tab],
        out_specs=(pl.BlockSpec((tm, tn), lambda i, j: (i, j)),
                   pl.BlockSpec((tm, lat_w), lambda i, j: (i, 0))),
        scratch_shapes=[pltpu.VMEM((tm, d), BF16)],
        compiler_params=_cparams(("arbitrary", "arbitrary")),
        name="inproj",
    )(x2, mod3, g, w_t, w_t, w_t, w_t, *tabs_d)


def _latent_kernel(ql_ref, kvl_ref, kpe_ref, cm_ref, smlo_ref, smhi_ref,
                   gq_ref, gkv_ref, wq_ref, wkv_ref, qmo_ref, kmo_ref, vmo_ref, *, n_heads):
    cm, smlo, smhi = cm_ref[...], smlo_ref[...], smhi_ref[...]
    half_m = MLA_ROPE_DIM // 2

    q = _dot(_rms_rows(ql_ref[...], gq_ref[...]).astype(BF16), wq_ref[...])
    kv = _dot(_rms_rows(kvl_ref[...], gkv_ref[...]).astype(BF16), wkv_ref[...])
    kr = _rope_slab(kpe_ref[...], cm, smlo, smhi, half_m).astype(BF16)

    for h in range(n_heads):
        base = h * MLA_QK_PAD
        nope = slice(base, base + LANES)
        rope = slice(base + LANES, base + 2 * LANES)
        qmo_ref[:, nope] = (q[:, nope] * MLA_SCALE).astype(BF16)
        qmo_ref[:, rope] = (_rope_slab(q[:, rope], cm, smlo, smhi, half_m) * MLA_SCALE).astype(BF16)
        kmo_ref[:, nope] = kv[:, nope].astype(BF16)
        kmo_ref[:, rope] = kr
        vmo_ref[:, h * LANES:(h + 1) * LANES] = kv[:, rope].astype(BF16)


def _latent_call(lat, tabs_m, gq, gkv, wq_pad, wkv, n_heads, tm=256):
    t = lat.shape[0]
    q_rank = gq.shape[1]
    kv_rank = gkv.shape[1]
    qk_w = n_heads * MLA_QK_PAD
    v_w = n_heads * MLA_V_DIM
    assert q_rank % kv_rank == 0 and (q_rank + kv_rank) % LANES == 0
    row = lambda w, c: pl.BlockSpec((tm, w), lambda i, c=c: (i, c))
    full = lambda a: pl.BlockSpec(a.shape, lambda i: (0, 0))
    in_specs = [row(q_rank, 0), row(kv_rank, q_rank // kv_rank),
                row(LANES, (q_rank + kv_rank) // LANES)]
    in_specs += [row(LANES, 0)] * 3
    in_specs += [full(gq), full(gkv), full(wq_pad), full(wkv)]
    out_shape = (jax.ShapeDtypeStruct((t, qk_w), BF16),
                 jax.ShapeDtypeStruct((t, qk_w), BF16),
                 jax.ShapeDtypeStruct((t, v_w), BF16))
    return pl.pallas_call(
        functools.partial(_latent_kernel, n_heads=n_heads),
        out_shape=out_shape,
        grid=(t // tm,),
        in_specs=in_specs,
        out_specs=(row(qk_w, 0), row(qk_w, 0), row(v_w, 0)),
        compiler_params=_cparams(("arbitrary",)),
        name="latent",
    )(lat, lat, lat, *tabs_m, gq, gkv, wq_pad, wkv)


def _numerators(s_ref, e_ref, rows=16):
    for r in range(0, s_ref.shape[0], rows):
        s = s_ref[r:r + rows, :]
        e_ref[r:r + rows, :] = jnp.exp2(s - jnp.max(s, axis=-1, keepdims=True)).astype(BF16)


def _weighted_values(e_ref, v1):
    o = _dot(e_ref[...], v1)
    dv = v1.shape[1] // 2
    return o[:, :dv], o[:, dv:dv + 1]


ATTN_TILES = 4
NUMER_LAG = 2


def _attn_step(q_ref, k_ref, v_ref, o_ref, s_refs, e_refs, make_lhs, finish):
    n = ATTN_TILES
    tq = o_ref.shape[1] // n

    @pl.when(pl.program_id(0) == 0)
    def _():
        for t in range(n - NUMER_LAG, n):
            s_refs[t][...] = jnp.zeros_like(s_refs[t])
        for t in range(n - NUMER_LAG):
            e_refs[t][...] = jnp.zeros_like(e_refs[t])

    v = v_ref[0]
    v1 = jnp.concatenate([v, jnp.ones_like(v)], axis=1)
    for k in range(n):
        rows = slice(k * tq, (k + 1) * tq)
        behind = (k - NUMER_LAG) % n
        s_refs[k][...] = _dot_nt(make_lhs(q_ref[0, rows]), k_ref[0])
        _numerators(s_refs[behind], e_refs[behind])
        o_ref[0, rows] = finish(*_weighted_values(e_refs[k], v1)).astype(o_ref.dtype)


def _attn_specs(b, n_heads, seq, tq, qk_width, v_width, q_col=0, k_col=0, v_col=0):
    rows = ATTN_TILES * tq
    n_groups = seq // rows
    total = b * n_heads * n_groups
    assert seq % rows == 0

    def decode(g):
        return g // (n_heads * n_groups), (g // n_groups) % n_heads, g % n_groups

    cur = lambda g: decode(jnp.minimum(g, total - 1))
    prev = lambda g: decode(jnp.maximum(g - 1, 0))
    q_spec = pl.BlockSpec((1, rows, qk_width),
                          lambda g: (cur(g)[0], cur(g)[2], q_col + cur(g)[1]))
    k_spec = pl.BlockSpec((1, seq, qk_width), lambda g: (cur(g)[0], 0, k_col + cur(g)[1]))
    v_spec = pl.BlockSpec((1, seq, v_width), lambda g: (prev(g)[0], 0, v_col + prev(g)[1]))
    o_spec = pl.BlockSpec((1, rows, v_width), lambda g: (prev(g)[0], prev(g)[2], prev(g)[1]))
    return total + 1, q_spec, k_spec, v_spec, o_spec


def _attn_scratch(rows, seq):
    return ([pltpu.VMEM((rows, seq), F32)] * ATTN_TILES
            + [pltpu.VMEM((rows, seq), BF16)] * ATTN_TILES)


def _diffattn_kernel(lq1_ref, lk1_ref, lq2_ref, lk2_ref, g_ref, q_ref, k_ref, v_ref, o_ref,
                     *scratch, lambda_init):
    lam = (jnp.exp(jnp.sum(lq1_ref[...] * lk1_ref[...], axis=-1, keepdims=True))
           - jnp.exp(jnp.sum(lq2_ref[...] * lk2_ref[...], axis=-1, keepdims=True))
           + lambda_init)
    gain = g_ref[...] * (1.0 - lambda_init)

    def make_lhs(q):
        lane = lax.broadcasted_iota(jnp.int32, q.shape, 1)
        zero = jnp.zeros_like(q)
        return jnp.concatenate([jnp.where(lane < DIFF_HEAD_DIM, q, zero),
                                jnp.where(lane >= DIFF_HEAD_DIM, q, zero)], axis=0)

    def finish(o, l):
        tq = o.shape[0] // 2
        out = o[:tq] / l[:tq] - lam * (o[tq:] / l[tq:])
        ms = jnp.mean(out * out, axis=-1, keepdims=True)
        return out * lax.rsqrt(ms + NORM_EPS) * gain

    _attn_step(q_ref, k_ref, v_ref, o_ref, scratch[:ATTN_TILES], scratch[ATTN_TILES:],
               make_lhs, finish)


def _diffattn_call(lams, g, qkv, n_heads, lambda_init, tq=256):
    b, s, _ = qkv.shape
    steps, q_spec, k_spec, v_spec, o_spec = _attn_specs(
        b, n_heads, s, tq, LANES, LANES, q_col=0, k_col=n_heads, v_col=2 * n_heads)
    vec = lambda a: pl.BlockSpec(a.shape, lambda g: (0, 0))
    return pl.pallas_call(
        functools.partial(_diffattn_kernel, lambda_init=lambda_init),
        out_shape=jax.ShapeDtypeStruct((b, s, n_heads * DIFF_V_DIM), BF16),
        grid=(steps,),
        in_specs=[vec(lams[0]), vec(lams[1]), vec(lams[2]), vec(lams[3]), vec(g),
                  q_spec, k_spec, v_spec],
        out_specs=o_spec,
        scratch_shapes=_attn_scratch(2 * tq, s),
        compiler_params=_cparams(("arbitrary",)),
        name="diffattn",
    )(*lams, g, qkv, qkv, qkv)


def _mlaattn_kernel(q_ref, k_ref, v_ref, o_ref, *scratch):
    _attn_step(q_ref, k_ref, v_ref, o_ref, scratch[:ATTN_TILES], scratch[ATTN_TILES:],
               lambda q: q, lambda o, l: o / l)


def _mlaattn_call(qm, km, vm, n_heads, tq=512):
    b, s, _ = qm.shape
    steps, q_spec, k_spec, v_spec, o_spec = _attn_specs(b, n_heads, s, tq, MLA_QK_PAD, MLA_V_DIM)
    return pl.pallas_call(
        _mlaattn_kernel,
        out_shape=jax.ShapeDtypeStruct(vm.shape, BF16),
        grid=(steps,),
        in_specs=[q_spec, k_spec, v_spec],
        out_specs=o_spec,
        scratch_shapes=_attn_scratch(tq, s),
        compiler_params=_cparams(("arbitrary",)),
        name="mlaattn",
    )(qm, km, vm)


def _outproj_kernel(od_ref, om_ref, wt_ref, wb_ref, x_ref, mod_ref, o_ref):
    acc = _dot(od_ref[...], wt_ref[...].astype(BF16))
    acc += _dot(om_ref[...], wb_ref[...].astype(BF16))
    o_ref[...] = x_ref[...] + mod_ref[0, 2:3, :] * acc


def _outproj_call(od, om, w_out, x2, mod3, seq, tm=2048, tn=512):
    t, d = x2.shape
    kd = od.shape[1]
    tpb = seq // tm
    return pl.pallas_call(
        _outproj_kernel,
        out_shape=jax.ShapeDtypeStruct((t, d), F32),
        grid=(t // tm, d // tn),
        in_specs=[pl.BlockSpec((tm, kd), lambda i, j: (i, 0)),
                  pl.BlockSpec((tm, om.shape[1]), lambda i, j: (i, 0)),
                  pl.BlockSpec((kd, tn), lambda i, j: (0, j)),
                  pl.BlockSpec((om.shape[1], tn), lambda i, j: (kd // om.shape[1], j)),
                  pl.BlockSpec((tm, tn), lambda i, j: (i, j)),
                  pl.BlockSpec((1, N_MOD, tn), lambda i, j: (i // tpb, 0, j))],
        out_specs=pl.BlockSpec((tm, tn), lambda i, j: (i, j)),
        compiler_params=_cparams(("arbitrary", "arbitrary")),
        name="outproj",
    )(od, om, w_out, w_out, x2, mod3)


def _ffn_kernel(x_ref, mod_ref, g_ref, w1a_ref, w1b_ref, w2p_ref, w2a_ref, gf_ref, o_ref,
                h_ref, ra_ref, rb_ref, *, rows):
    j = pl.program_id(1)
    nj = pl.num_programs(1) - 1
    act = lambda u: jnp.square(jnp.maximum(u, 0.0)).astype(BF16)

    @pl.when(j == 0)
    def _():
        def clear(sl):
            o_ref[sl, :] = jnp.zeros((rows, o_ref.shape[1]), o_ref.dtype)
            rb_ref[sl, :] = jnp.zeros((rows, rb_ref.shape[1]), rb_ref.dtype)

        _norm_modulate_rows(x_ref, g_ref, mod_ref, 3, 4, h_ref, rows, also=clear)

    @pl.when(j < nj)
    def _():
        u0 = _dot(h_ref[...], w1a_ref[...].astype(BF16))
        o_ref[...] += _dot(rb_ref[...], w2p_ref[...].astype(BF16))
        ra_ref[...] = act(u0)
        u1 = _dot(h_ref[...], w1b_ref[...].astype(BF16))
        o_ref[...] += _dot(ra_ref[...], w2a_ref[...].astype(BF16))
        rb_ref[...] = act(u1)

    @pl.when(j == nj)
    def _():
        o_ref[...] += _dot(rb_ref[...], w2p_ref[...].astype(BF16))
        d = x_ref.shape[1]
        gate = mod_ref[0, 5:6, :]
        gf = gf_ref[...]
        resid = lambda sl: x_ref[sl, :] + gate * o_ref[sl, :]

        def sumsq(sl):
            y = resid(sl)
            return jnp.sum(y * y, axis=-1, keepdims=True)

        def finish(sl, ss):
            o_ref[sl, :] = resid(sl) * lax.rsqrt(ss * (1.0 / d) + NORM_EPS) * gf

        _two_stage_rows(x_ref.shape[0], rows, sumsq, finish)


def _ffn_call(x1, mod3, g, w1, w2, gf, seq, tm=1024, tf=256):
    t, d = x1.shape
    n_chunks = w1.shape[1] // tf
    nj = n_chunks // 2
    tpb = seq // tm
    last_a, last_b = n_chunks - 2, n_chunks - 1
    return pl.pallas_call(
        functools.partial(_ffn_kernel, rows=32),
        out_shape=jax.ShapeDtypeStruct((t, d), F32),
        grid=(t // tm, nj + 1),
        in_specs=[pl.BlockSpec((tm, d), lambda i, j: (i, 0), pipeline_mode=pl.Buffered(1)),
                  pl.BlockSpec((1, N_MOD, d), lambda i, j: (i // tpb, 0, 0)),
                  pl.BlockSpec((1, d), lambda i, j: (0, 0)),
                  pl.BlockSpec((d, tf), lambda i, j: (0, jnp.minimum(2 * j, last_a))),
                  pl.BlockSpec((d, tf), lambda i, j: (0, jnp.minimum(2 * j + 1, last_b))),
                  pl.BlockSpec((tf, d), lambda i, j: (jnp.maximum(2 * j - 1, 0), 0)),
                  pl.BlockSpec((tf, d), lambda i, j: (jnp.minimum(2 * j, last_a), 0)),
                  pl.BlockSpec((1, d), lambda i, j: (0, 0))],
        out_specs=pl.BlockSpec((tm, d), lambda i, j: (i, 0)),
        scratch_shapes=[pltpu.VMEM((tm, d), BF16),
                        pltpu.VMEM((tm, tf), BF16),
                        pltpu.VMEM((tm, tf), BF16)],
        compiler_params=_cparams(("arbitrary", "arbitrary")),
        name="ffn",
    )(x1, mod3, g, w1, w1, w2, w2, gf)


def _rope_tables(positions, rot_dim, theta, group):
    half = rot_dim // 2
    inv = 1.0 / (theta ** (jnp.arange(0, rot_dim, 2, dtype=F32) / rot_dim))
    ang = positions.astype(F32).reshape(-1, 1) * inv[None, :]
    cos, sin = jnp.cos(ang), jnp.sin(ang)
    t = ang.shape[0]
    rest = group - rot_dim
    widen = lambda lo, hi, fill: jnp.tile(
        jnp.concatenate([lo, hi, jnp.full((t, rest), fill, F32)], axis=1), (1, LANES // group))
    zeros = jnp.zeros_like(sin)
    return (widen(cos, cos, 1.0), widen(-sin, zeros, 0.0), widen(zeros, sin, 0.0))


def kernel(x, c, positions, w_ada, b_ada, g_norm_mix, w_in, lambda_q1, lambda_k1, lambda_q2,
           lambda_k2, g_diff_sub, g_q_a, w_q_b, g_kv_a, w_kv_b, w_out, g_norm_ffn, w_ff1, w_ff2,
           g_final):
    b, s, d = x.shape
    depth = w_ada.shape[0]
    t = b * s
    q_rank = g_q_a.shape[1]
    kv_rank = g_kv_a.shape[1]
    n_mla = w_kv_b.shape[2] // (MLA_NOPE_DIM + MLA_V_DIM)
    n_diff = (w_in.shape[2] - q_rank - kv_rank - MLA_ROPE_DIM) // (3 * DIFF_V_DIM)
    assert n_diff == n_mla

    tabs_d = _rope_tables(positions, DIFF_ROT_DIM, ROPE_THETA, DIFF_HEAD_DIM)
    tabs_m = _rope_tables(positions, MLA_ROPE_DIM, MLA_ROPE_THETA, LANES)
    c_pad = jnp.pad(c, ((0, 8 - b), (0, 0)))
    x2 = x.reshape(t, d)

    for l in range(depth):
        lambda_init = 0.8 - 0.6 * float(np.exp(-0.3 * l))
        mod = _mod_call(c_pad, w_ada[l], b_ada[l][None, :])
        mod3 = mod[:b].reshape(b, N_MOD, d)

        qkv, lat = _inproj_call(x2, mod3, g_norm_mix[l][None, :], jnp.swapaxes(w_in[l], 0, 1),
                                tabs_d, n_diff * DIFF_V_DIM, q_rank, kv_rank, s)

        wq_pad = jnp.pad(w_q_b[l].reshape(q_rank, n_mla, MLA_NOPE_DIM + MLA_ROPE_DIM),
                         ((0, 0), (0, 0), (0, MLA_QK_PAD - MLA_NOPE_DIM - MLA_ROPE_DIM))
                         ).reshape(q_rank, n_mla * MLA_QK_PAD).astype(BF16)
        qm, km, vm = _latent_call(lat, tabs_m, g_q_a[l][None, :], g_kv_a[l][None, :], wq_pad,
                                  w_kv_b[l].astype(BF16), n_mla)

        shp = lambda a: a.reshape(b, s, a.shape[1])
        lams = tuple(v[l][None, :] for v in (lambda_q1, lambda_k1, lambda_q2, lambda_k2))
        o_diff = _diffattn_call(lams, g_diff_sub[l][None, :], shp(qkv), n_diff, lambda_init)
        o_mla = _mlaattn_call(shp(qm), shp(km), shp(vm), n_mla)

        x2 = _outproj_call(o_diff.reshape(t, -1), o_mla.reshape(t, -1), w_out[l], x2, mod3, s)

        last = l == depth - 1
        assert last, "final rmsnorm is fused into the last layer's MLP kernel"
        x2 = _ffn_call(x2, mod3, g_norm_ffn[l][None, :], w_ff1[l], w_ff2[l], g_final[None, :], s)

    return x2.reshape(b, s, d)
```

```python
import functools
import math

import jax
import jax.numpy as jnp
import numpy as np
from jax import lax
from jax.experimental import pallas as pl
from jax.experimental.pallas import tpu as pltpu

F32 = jnp.float32
BF16 = jnp.bfloat16

LANES = 128
NORM_EPS = 1e-6
N_MOD = 6

DIFF_HEAD_DIM = 64
DIFF_V_DIM = 2 * DIFF_HEAD_DIM
DIFF_ROT_DIM = DIFF_HEAD_DIM // 4
ROPE_THETA = 500000.0
MLA_V_DIM = 128
MLA_NOPE_DIM = 128
MLA_ROPE_DIM = 64
MLA_ROPE_THETA = 10000.0
MLA_QK_PAD = 256

DIFF_SCALE = DIFF_HEAD_DIM ** -0.5 * math.log2(math.e)
MLA_SCALE = (MLA_NOPE_DIM + MLA_ROPE_DIM) ** -0.5 * math.log2(math.e)

VMEM_LIMIT = 56 * 1024 * 1024


def _cparams(sem):
    return pltpu.CompilerParams(dimension_semantics=sem, vmem_limit_bytes=VMEM_LIMIT)


def _dot(a, b):
    return jnp.dot(a, b, preferred_element_type=F32)


def _dot_nt(a, b):
    return lax.dot_general(a, b, (((1,), (1,)), ((), ())), preferred_element_type=F32)


def _mod_kernel(c_ref, w_ref, b_ref, o_ref):
    c = c_ref[...]
    sc = c / (1.0 + jnp.exp(-c))
    o_ref[...] = _dot(sc.astype(BF16), w_ref[...].astype(BF16)) + b_ref[...]


def _mod_call(c_pad, w_ada, b_ada, tn=1024):
    m, d = c_pad.shape
    n = w_ada.shape[1]
    return pl.pallas_call(
        _mod_kernel,
        out_shape=jax.ShapeDtypeStruct((m, n), F32),
        grid=(n // tn,),
        in_specs=[pl.BlockSpec((m, d), lambda j: (0, 0)),
                  pl.BlockSpec((d, tn), lambda j: (0, j)),
                  pl.BlockSpec((1, tn), lambda j: (0, j))],
        out_specs=pl.BlockSpec((m, tn), lambda j: (0, j)),
        compiler_params=_cparams(("arbitrary",)),
        name="mod",
    )(c_pad, w_ada, b_ada)


def _two_stage_rows(n_rows, rows, sumsq, finish):
    chunk = lambda r: pl.ds(pl.multiple_of(r * rows, rows), rows)

    def body(r, ss_prev):
        ss = sumsq(chunk(r))
        finish(chunk(r - 1), ss_prev)
        return ss

    n = n_rows // rows
    ss_last = lax.fori_loop(1, n, body, sumsq(chunk(0)))
    finish(chunk(n - 1), ss_last)


def _norm_modulate_rows(x_ref, g_ref, mod_ref, shift_row, scale_row, h_ref, rows):
    tm, d = x_ref.shape
    shift = mod_ref[0, shift_row:shift_row + 1, :]
    gs = g_ref[...] * (1.0 + mod_ref[0, scale_row:scale_row + 1, :])

    def sumsq(sl):
        x = x_ref[sl, :]
        return jnp.sum(x * x, axis=-1, keepdims=True)

    def finish(sl, ss):
        rs = lax.rsqrt(ss * (1.0 / d) + NORM_EPS)
        h_ref[sl, :] = (x_ref[sl, :] * rs * gs + shift).astype(BF16)

    _two_stage_rows(tm, rows, sumsq, finish)


def _rope_slab(x, c, s_lo, s_hi, half):
    return x * c + pltpu.roll(x, half, 1) * s_hi + pltpu.roll(x, LANES - half, 1) * s_lo


def _rms_rows(x, g):
    ms = jnp.mean(x * x, axis=-1, keepdims=True)
    return x * lax.rsqrt(ms + NORM_EPS) * g


def _inproj_kernel(x_ref, mod_ref, g_ref, w_ref, wql_ref, wkvl_ref, wpe_ref,
                   cd_ref, slo_ref, shi_ref, qk_ref, v_ref, lat_ref, h_ref, r_ref, *, rows):
    j = pl.program_id(1)
    tn = r_ref.shape[1]

    def project():
        return _dot_nt(h_ref[...], w_ref[...].astype(BF16))

    @pl.when(j == 0)
    def _():
        _norm_modulate_rows(x_ref, g_ref, mod_ref, 0, 1, h_ref, rows)
        h = h_ref[...]
        q_rank, kv_rank, pe = wql_ref.shape[0], wkvl_ref.shape[0], wpe_ref.shape[0]
        lat_ref[:, :q_rank] = _dot_nt(h, wql_ref[...].astype(BF16))
        lat_ref[:, q_rank:q_rank + kv_rank] = _dot_nt(h, wkvl_ref[...].astype(BF16))
        lat_ref[:, q_rank + kv_rank:q_rank + kv_rank + pe] = _dot_nt(h, wpe_ref[...].astype(BF16))
        pad = lat_ref.shape[1] - (q_rank + kv_rank + pe)
        lat_ref[:, q_rank + kv_rank + pe:] = jnp.zeros((lat_ref.shape[0], pad), F32)
        r_ref[...] = project()

    @pl.when((j >= 1) & (j <= 4))
    def _():
        c, s_lo, s_hi = cd_ref[...], slo_ref[...], shi_ref[...]
        scale = jnp.where(j <= 2, DIFF_SCALE, 1.0)
        for k in range(tn // LANES):
            sl = slice(k * LANES, (k + 1) * LANES)
            qk_ref[:, sl] = (_rope_slab(r_ref[:, sl], c, s_lo, s_hi, DIFF_ROT_DIM // 2)
                             * scale).astype(BF16)
        r_ref[...] = project()

    @pl.when(j == 5)
    def _():
        v_ref[:, :tn] = r_ref[...].astype(BF16)
        v_ref[:, tn:] = project().astype(BF16)


def _inproj_call(x2, mod3, g, w_t, tabs_d, dw, q_rank, kv_rank, seq, tm=1024):
    t, d = x2.shape
    tpb = seq // tm
    pe = MLA_ROPE_DIM
    lat_w = q_rank + kv_rank + LANES
    tn = dw // 2
    assert dw % (2 * LANES) == 0 and seq % tm == 0
    assert (3 * dw) % q_rank == 0 and (3 * dw + q_rank) % kv_rank == 0
    assert (3 * dw + q_rank + kv_rank) % pe == 0 and w_t.shape[0] == 3 * dw + q_rank + kv_rank + pe
    once = lambda rows_, idx: pl.BlockSpec((rows_, d), lambda i, j: (idx, 0),
                                           pipeline_mode=pl.Buffered(1))
    tab = pl.BlockSpec((tm, LANES), lambda i, j: (i, 0))
    return pl.pallas_call(
        functools.partial(_inproj_kernel, rows=32),
        out_shape=(jax.ShapeDtypeStruct((t, 2 * dw), BF16),
                   jax.ShapeDtypeStruct((t, dw), BF16),
                   jax.ShapeDtypeStruct((t, lat_w), F32)),
        grid=(t // tm, 6),
        in_specs=[pl.BlockSpec((tm, d), lambda i, j: (i, 0), pipeline_mode=pl.Buffered(1)),
                  pl.BlockSpec((1, N_MOD, d), lambda i, j: (i // tpb, 0, 0)),
                  pl.BlockSpec((1, d), lambda i, j: (0, 0)),
                  pl.BlockSpec((tn, d), lambda i, j: (j, 0)),
                  once(q_rank, 3 * dw // q_rank),
                  once(kv_rank, (3 * dw + q_rank) // kv_rank),
                  once(pe, (3 * dw + q_rank + kv_rank) // pe),
                  tab, tab, tab],
        out_specs=(pl.BlockSpec((tm, tn), lambda i, j: (i, jnp.clip(j - 1, 0, 3))),
                   pl.BlockSpec((tm, dw), lambda i, j: (i, 0)),
                   pl.BlockSpec((tm, lat_w), lambda i, j: (i, 0))),
        scratch_shapes=[pltpu.VMEM((tm, d), BF16), pltpu.VMEM((tm, tn), F32)],
        compiler_params=_cparams(("arbitrary", "arbitrary")),
        name="inproj",
    )(x2, mod3, g, w_t, w_t, w_t, w_t, *tabs_d)


def _latent_kernel(ql_ref, kvl_ref, kpe_ref, cm_ref, smlo_ref, smhi_ref,
                   gq_ref, gkv_ref, wq_ref, wkv_ref, qmo_ref, kmo_ref, vmo_ref, *, n_heads):
    cm, smlo, smhi = cm_ref[...], smlo_ref[...], smhi_ref[...]
    half_m = MLA_ROPE_DIM // 2

    q = _dot(_rms_rows(ql_ref[...], gq_ref[...]).astype(BF16), wq_ref[...])
    kv = _dot(_rms_rows(kvl_ref[...], gkv_ref[...]).astype(BF16), wkv_ref[...])
    kr = _rope_slab(kpe_ref[...], cm, smlo, smhi, half_m).astype(BF16)

    for h in range(n_heads):
        base = h * MLA_QK_PAD
        nope = slice(base, base + LANES)
        rope = slice(base + LANES, base + 2 * LANES)
        qmo_ref[:, nope] = (q[:, nope] * MLA_SCALE).astype(BF16)
        qmo_ref[:, rope] = (_rope_slab(q[:, rope], cm, smlo, smhi, half_m) * MLA_SCALE).astype(BF16)
        kmo_ref[:, nope] = kv[:, nope].astype(BF16)
        kmo_ref[:, rope] = kr
        vmo_ref[:, h * LANES:(h + 1) * LANES] = kv[:, rope].astype(BF16)


def _latent_call(lat, tabs_m, gq, gkv, wq_pad, wkv, n_heads, tm=512):
    t = lat.shape[0]
    q_rank = gq.shape[1]
    kv_rank = gkv.shape[1]
    qk_w = n_heads * MLA_QK_PAD
    v_w = n_heads * MLA_V_DIM
    assert q_rank % kv_rank == 0 and (q_rank + kv_rank) % LANES == 0
    row = lambda w, c: pl.BlockSpec((tm, w), lambda i, c=c: (i, c))
    full = lambda a: pl.BlockSpec(a.shape, lambda i: (0, 0))
    in_specs = [row(q_rank, 0), row(kv_rank, q_rank // kv_rank),
                row(LANES, (q_rank + kv_rank) // LANES)]
    in_specs += [row(LANES, 0)] * 3
    in_specs += [full(gq), full(gkv), full(wq_pad), full(wkv)]
    out_shape = (jax.ShapeDtypeStruct((t, qk_w), BF16),
                 jax.ShapeDtypeStruct((t, qk_w), BF16),
                 jax.ShapeDtypeStruct((t, v_w), BF16))
    return pl.pallas_call(
        functools.partial(_latent_kernel, n_heads=n_heads),
        out_shape=out_shape,
        grid=(t // tm,),
        in_specs=in_specs,
        out_specs=(row(qk_w, 0), row(qk_w, 0), row(v_w, 0)),
        compiler_params=_cparams(("arbitrary",)),
        name="latent",
    )(lat, lat, lat, *tabs_m, gq, gkv, wq_pad, wkv)


def _numerators(s_ref, e_ref, rows=16):
    for r in range(0, s_ref.shape[0], rows):
        s = s_ref[r:r + rows, :]
        e_ref[r:r + rows, :] = jnp.exp2(s - jnp.max(s, axis=-1, keepdims=True)).astype(BF16)


def _weighted_values(e_ref, v1):
    o = _dot(e_ref[...], v1)
    dv = v1.shape[1] // 2
    return o[:, :dv], o[:, dv:dv + 1]


ATTN_TILES = 4
NUMER_LAG = 2


def _attn_step(q_ref, k_ref, v_ref, o_ref, s_refs, e_refs, make_lhs, finish):
    n = ATTN_TILES
    tq = o_ref.shape[1] // n

    @pl.when(pl.program_id(0) == 0)
    def _():
        for t in range(n - NUMER_LAG, n):
            s_refs[t][...] = jnp.zeros_like(s_refs[t])
        for t in range(n - NUMER_LAG):
            e_refs[t][...] = jnp.zeros_like(e_refs[t])

    v = v_ref[0]
    v1 = jnp.concatenate([v, jnp.ones_like(v)], axis=1)
    for k in range(n):
        rows = slice(k * tq, (k + 1) * tq)
        behind = (k - NUMER_LAG) % n
        s_refs[k][...] = _dot_nt(make_lhs(q_ref[0, rows]), k_ref[0])
        _numerators(s_refs[behind], e_refs[behind])
        o_ref[0, rows] = finish(*_weighted_values(e_refs[k], v1)).astype(o_ref.dtype)


def _attn_specs(b, n_heads, seq, tq, qk_width, v_width, q_col=0, k_col=0, v_col=0):
    rows = ATTN_TILES * tq
    n_groups = seq // rows
    total = b * n_heads * n_groups
    assert seq % rows == 0

    def decode(g):
        return g // (n_heads * n_groups), (g // n_groups) % n_heads, g % n_groups

    cur = lambda g: decode(jnp.minimum(g, total - 1))
    prev = lambda g: decode(jnp.maximum(g - 1, 0))
    q_spec = pl.BlockSpec((1, rows, qk_width),
                          lambda g: (cur(g)[0], cur(g)[2], q_col + cur(g)[1]))
    k_spec = pl.BlockSpec((1, seq, qk_width), lambda g: (cur(g)[0], 0, k_col + cur(g)[1]))
    v_spec = pl.BlockSpec((1, seq, v_width), lambda g: (prev(g)[0], 0, v_col + prev(g)[1]))
    o_spec = pl.BlockSpec((1, rows, v_width), lambda g: (prev(g)[0], prev(g)[2], prev(g)[1]))
    return total + 1, q_spec, k_spec, v_spec, o_spec


def _attn_scratch(rows, seq):
    return ([pltpu.VMEM((rows, seq), F32)] * ATTN_TILES
            + [pltpu.VMEM((rows, seq), BF16)] * ATTN_TILES)


def _diffattn_kernel(lq1_ref, lk1_ref, lq2_ref, lk2_ref, g_ref, q_ref, k_ref, v_ref, o_ref,
                     *scratch, lambda_init):
    lam = (jnp.exp(jnp.sum(lq1_ref[...] * lk1_ref[...], axis=-1, keepdims=True))
           - jnp.exp(jnp.sum(lq2_ref[...] * lk2_ref[...], axis=-1, keepdims=True))
           + lambda_init)
    gain = g_ref[...] * (1.0 - lambda_init)

    def make_lhs(q):
        lane = lax.broadcasted_iota(jnp.int32, q.shape, 1)
        zero = jnp.zeros_like(q)
        return jnp.concatenate([jnp.where(lane < DIFF_HEAD_DIM, q, zero),
                                jnp.where(lane >= DIFF_HEAD_DIM, q, zero)], axis=0)

    def finish(o, l):
        tq = o.shape[0] // 2
        out = o[:tq] / l[:tq] - lam * (o[tq:] / l[tq:])
        ms = jnp.mean(out * out, axis=-1, keepdims=True)
        return out * lax.rsqrt(ms + NORM_EPS) * gain

    _attn_step(q_ref, k_ref, v_ref, o_ref, scratch[:ATTN_TILES], scratch[ATTN_TILES:],
               make_lhs, finish)


def _diffattn_call(lams, g, qk, v, n_heads, lambda_init, tq=256):
    b, s, _ = qk.shape
    steps, q_spec, k_spec, v_spec, o_spec = _attn_specs(
        b, n_heads, s, tq, LANES, LANES, q_col=0, k_col=n_heads, v_col=0)
    vec = lambda a: pl.BlockSpec(a.shape, lambda g: (0, 0))
    return pl.pallas_call(
        functools.partial(_diffattn_kernel, lambda_init=lambda_init),
        out_shape=jax.ShapeDtypeStruct((b, s, n_heads * DIFF_V_DIM), BF16),
        grid=(steps,),
        in_specs=[vec(lams[0]), vec(lams[1]), vec(lams[2]), vec(lams[3]), vec(g),
                  q_spec, k_spec, v_spec],
        out_specs=o_spec,
        scratch_shapes=_attn_scratch(2 * tq, s),
        compiler_params=_cparams(("arbitrary",)),
        name="diffattn",
    )(*lams, g, qk, qk, v)


def _mlaattn_kernel(q_ref, k_ref, v_ref, o_ref, *scratch):
    _attn_step(q_ref, k_ref, v_ref, o_ref, scratch[:ATTN_TILES], scratch[ATTN_TILES:],
               lambda q: q, lambda o, l: o / l)


def _mlaattn_call(qm, km, vm, n_heads, tq=512):
    b, s, _ = qm.shape
    steps, q_spec, k_spec, v_spec, o_spec = _attn_specs(b, n_heads, s, tq, MLA_QK_PAD, MLA_V_DIM)
    return pl.pallas_call(
        _mlaattn_kernel,
        out_shape=jax.ShapeDtypeStruct(vm.shape, BF16),
        grid=(steps,),
        in_specs=[q_spec, k_spec, v_spec],
        out_specs=o_spec,
        scratch_shapes=_attn_scratch(tq, s),
        compiler_params=_cparams(("arbitrary",)),
        name="mlaattn",
    )(qm, km, vm)


def _outproj_kernel(od_ref, om_ref, wt_ref, wb_ref, x_ref, mod_ref, o_ref):
    acc = _dot(od_ref[...], wt_ref[...].astype(BF16))
    acc += _dot(om_ref[...], wb_ref[...].astype(BF16))
    o_ref[...] = x_ref[...] + mod_ref[0, 2:3, :] * acc


def _outproj_call(od, om, w_out, x2, mod3, seq, tm=2048, tn=512):
    t, d = x2.shape
    kd = od.shape[1]
    tpb = seq // tm
    return pl.pallas_call(
        _outproj_kernel,
        out_shape=jax.ShapeDtypeStruct((t, d), F32),
        grid=(t // tm, d // tn),
        in_specs=[pl.BlockSpec((tm, kd), lambda i, j: (i, 0)),
                  pl.BlockSpec((tm, om.shape[1]), lambda i, j: (i, 0)),
                  pl.BlockSpec((kd, tn), lambda i, j: (0, j)),
                  pl.BlockSpec((om.shape[1], tn), lambda i, j: (kd // om.shape[1], j)),
                  pl.BlockSpec((tm, tn), lambda i, j: (i, j)),
                  pl.BlockSpec((1, N_MOD, tn), lambda i, j: (i // tpb, 0, j))],
        out_specs=pl.BlockSpec((tm, tn), lambda i, j: (i, j)),
        compiler_params=_cparams(("arbitrary", "arbitrary")),
        name="outproj",
    )(od, om, w_out, w_out, x2, mod3)


def _ffn_kernel(x_ref, mod_ref, g_ref, w1a_ref, w1b_ref, w2p_ref, w2a_ref, gf_ref, o_ref,
                h_ref, ra_ref, rb_ref, *, rows):
    j = pl.program_id(1)
    nj = pl.num_programs(1) - 1
    act = lambda u: jnp.square(jnp.maximum(u, 0.0)).astype(BF16)

    @pl.when(j == 0)
    def _():
        _norm_modulate_rows(x_ref, g_ref, mod_ref, 3, 4, h_ref, rows)
        ra_ref[...] = act(_dot(h_ref[...], w1a_ref[...].astype(BF16)))
        u1 = _dot(h_ref[...], w1b_ref[...].astype(BF16))
        o_ref[...] = _dot(ra_ref[...], w2a_ref[...].astype(BF16))
        rb_ref[...] = act(u1)

    @pl.when((j > 0) & (j < nj))
    def _():
        u0 = _dot(h_ref[...], w1a_ref[...].astype(BF16))
        o_ref[...] += _dot(rb_ref[...], w2p_ref[...].astype(BF16))
        ra_ref[...] = act(u0)
        u1 = _dot(h_ref[...], w1b_ref[...].astype(BF16))
        o_ref[...] += _dot(ra_ref[...], w2a_ref[...].astype(BF16))
        rb_ref[...] = act(u1)

    @pl.when(j == nj)
    def _():
        o_ref[...] += _dot(rb_ref[...], w2p_ref[...].astype(BF16))
        d = x_ref.shape[1]
        gate = mod_ref[0, 5:6, :]
        gf = gf_ref[...]
        resid = lambda sl: x_ref[sl, :] + gate * o_ref[sl, :]

        def sumsq(sl):
            y = resid(sl)
            return jnp.sum(y * y, axis=-1, keepdims=True)

        def finish(sl, ss):
            o_ref[sl, :] = resid(sl) * lax.rsqrt(ss * (1.0 / d) + NORM_EPS) * gf

        _two_stage_rows(x_ref.shape[0], rows, sumsq, finish)


def _ffn_call(x1, mod3, g, w1, w2, gf, seq, tm=1024, tf=256):
    t, d = x1.shape
    n_chunks = w1.shape[1] // tf
    nj = n_chunks // 2
    tpb = seq // tm
    last_a, last_b = n_chunks - 2, n_chunks - 1
    return pl.pallas_call(
        functools.partial(_ffn_kernel, rows=32),
        out_shape=jax.ShapeDtypeStruct((t, d), F32),
        grid=(t // tm, nj + 1),
        in_specs=[pl.BlockSpec((tm, d), lambda i, j: (i, 0)),
                  pl.BlockSpec((1, N_MOD, d), lambda i, j: (i // tpb, 0, 0)),
                  pl.BlockSpec((1, d), lambda i, j: (0, 0)),
                  pl.BlockSpec((d, tf), lambda i, j: (0, jnp.minimum(2 * j, last_a))),
                  pl.BlockSpec((d, tf), lambda i, j: (0, jnp.minimum(2 * j + 1, last_b))),
                  pl.BlockSpec((tf, d), lambda i, j: (jnp.maximum(2 * j - 1, 0), 0)),
                  pl.BlockSpec((tf, d), lambda i, j: (jnp.minimum(2 * j, last_a), 0)),
                  pl.BlockSpec((1, d), lambda i, j: (0, 0))],
        out_specs=pl.BlockSpec((tm, d), lambda i, j: (i, 0)),
        scratch_shapes=[pltpu.VMEM((tm, d), BF16),
                        pltpu.VMEM((tm, tf), BF16),
                        pltpu.VMEM((tm, tf), BF16)],
        compiler_params=_cparams(("arbitrary", "arbitrary")),
        name="ffn",
    )(x1, mod3, g, w1, w1, w2, w2, gf)


def _rope_tables(positions, rot_dim, theta, group):
    half = rot_dim // 2
    inv = 1.0 / (theta ** (jnp.arange(0, rot_dim, 2, dtype=F32) / rot_dim))
    ang = positions.astype(F32).reshape(-1, 1) * inv[None, :]
    cos, sin = jnp.cos(ang), jnp.sin(ang)
    t = ang.shape[0]
    rest = group - rot_dim
    widen = lambda lo, hi, fill: jnp.concatenate(
        [lo, hi, jnp.full((t, rest), fill, F32)] * (LANES // group), axis=1)
    zeros = jnp.zeros_like(sin)
    return (widen(cos, cos, 1.0), widen(-sin, zeros, 0.0), widen(zeros, sin, 0.0))


def kernel(x, c, positions, w_ada, b_ada, g_norm_mix, w_in, lambda_q1, lambda_k1, lambda_q2,
           lambda_k2, g_diff_sub, g_q_a, w_q_b, g_kv_a, w_kv_b, w_out, g_norm_ffn, w_ff1, w_ff2,
           g_final):
    b, s, d = x.shape
    depth = w_ada.shape[0]
    t = b * s
    q_rank = g_q_a.shape[1]
    kv_rank = g_kv_a.shape[1]
    n_mla = w_kv_b.shape[2] // (MLA_NOPE_DIM + MLA_V_DIM)
    n_diff = (w_in.shape[2] - q_rank - kv_rank - MLA_ROPE_DIM) // (3 * DIFF_V_DIM)
    assert n_diff == n_mla

    tabs_d = _rope_tables(positions, DIFF_ROT_DIM, ROPE_THETA, DIFF_HEAD_DIM)
    tabs_m = _rope_tables(positions, MLA_ROPE_DIM, MLA_ROPE_THETA, LANES)
    c_pad = jnp.pad(c, ((0, 8 - b), (0, 0)))
    x2 = x.reshape(t, d)

    for l in range(depth):
        lambda_init = 0.8 - 0.6 * float(np.exp(-0.3 * l))
        mod = _mod_call(c_pad, w_ada[l], b_ada[l][None, :])
        mod3 = mod[:b].reshape(b, N_MOD, d)

        dqk, dv, lat = _inproj_call(x2, mod3, g_norm_mix[l][None, :],
                                    jnp.swapaxes(w_in[l], 0, 1), tabs_d, n_diff * DIFF_V_DIM,
                                    q_rank, kv_rank, s)

        wq_pad = jnp.pad(w_q_b[l].reshape(q_rank, n_mla, MLA_NOPE_DIM + MLA_ROPE_DIM),
                         ((0, 0), (0, 0), (0, MLA_QK_PAD - MLA_NOPE_DIM - MLA_ROPE_DIM))
                         ).reshape(q_rank, n_mla * MLA_QK_PAD).astype(BF16)
        qm, km, vm = _latent_call(lat, tabs_m, g_q_a[l][None, :], g_kv_a[l][None, :], wq_pad,
                                  w_kv_b[l].astype(BF16), n_mla)

        shp = lambda a: a.reshape(b, s, a.shape[1])
        lams = tuple(v[l][None, :] for v in (lambda_q1, lambda_k1, lambda_q2, lambda_k2))
        o_diff = _diffattn_call(lams, g_diff_sub[l][None, :], shp(dqk), shp(dv), n_diff,
                                lambda_init)
        o_mla = _mlaattn_call(shp(qm), shp(km), shp(vm), n_mla)

        x2 = _outproj_call(o_diff.reshape(t, -1), o_mla.reshape(t, -1), w_out[l], x2, mod3, s)

        last = l == depth - 1
        assert last, "final rmsnorm is fused into the last layer's MLP kernel"
        x2 = _ffn_call(x2, mod3, g_norm_ffn[l][None, :], w_ff1[l], w_ff2[l], g_final[None, :], s)

    return x2.reshape(b, s, d)
```

```python
import functools
import math

import jax
import jax.numpy as jnp
import numpy as np
from jax import lax
from jax.experimental import pallas as pl
from jax.experimental.pallas import tpu as pltpu

F32 = jnp.float32
BF16 = jnp.bfloat16

LANES = 128
NORM_EPS = 1e-6
N_MOD = 6

DIFF_HEAD_DIM = 64
DIFF_V_DIM = 2 * DIFF_HEAD_DIM
DIFF_ROT_DIM = DIFF_HEAD_DIM // 4
ROPE_THETA = 500000.0
MLA_V_DIM = 128
MLA_NOPE_DIM = 128
MLA_ROPE_DIM = 64
MLA_ROPE_THETA = 10000.0
MLA_QK_PAD = 256

DIFF_SCALE = DIFF_HEAD_DIM ** -0.5 * math.log2(math.e)
MLA_SCALE = (MLA_NOPE_DIM + MLA_ROPE_DIM) ** -0.5 * math.log2(math.e)

VMEM_LIMIT = 56 * 1024 * 1024


def _cparams(sem):
    return pltpu.CompilerParams(dimension_semantics=sem, vmem_limit_bytes=VMEM_LIMIT)


def _dot(a, b):
    return jnp.dot(a, b, preferred_element_type=F32)


def _dot_nt(a, b):
    return lax.dot_general(a, b, (((1,), (1,)), ((), ())), preferred_element_type=F32)


def _mod_kernel(c_ref, w_ref, b_ref, o_ref):
    c = c_ref[...]
    sc = c / (1.0 + jnp.exp(-c))
    o_ref[...] = _dot(sc.astype(BF16), w_ref[...].astype(BF16)) + b_ref[...]


def _mod_call(c_pad, w_ada, b_ada, tn=1024):
    m, d = c_pad.shape
    n = w_ada.shape[1]
    return pl.pallas_call(
        _mod_kernel,
        out_shape=jax.ShapeDtypeStruct((m, n), F32),
        grid=(n // tn,),
        in_specs=[pl.BlockSpec((m, d), lambda j: (0, 0)),
                  pl.BlockSpec((d, tn), lambda j: (0, j)),
                  pl.BlockSpec((1, tn), lambda j: (0, j))],
        out_specs=pl.BlockSpec((m, tn), lambda j: (0, j)),
        compiler_params=_cparams(("arbitrary",)),
        name="mod",
    )(c_pad, w_ada, b_ada)


def _two_stage_rows(n_rows, rows, sumsq, finish):
    chunk = lambda r: pl.ds(pl.multiple_of(r * rows, rows), rows)

    def body(r, ss_prev):
        ss = sumsq(chunk(r))
        finish(chunk(r - 1), ss_prev)
        return ss

    n = n_rows // rows
    ss_last = lax.fori_loop(1, n, body, sumsq(chunk(0)))
    finish(chunk(n - 1), ss_last)


def _norm_modulate_rows(x_ref, g_ref, mod_ref, shift_row, scale_row, h_ref, rows):
    tm, d = x_ref.shape
    shift = mod_ref[0, shift_row:shift_row + 1, :]
    gs = g_ref[...] * (1.0 + mod_ref[0, scale_row:scale_row + 1, :])

    def sumsq(sl):
        x = x_ref[sl, :]
        return jnp.sum(x * x, axis=-1, keepdims=True)

    def finish(sl, ss):
        rs = lax.rsqrt(ss * (1.0 / d) + NORM_EPS)
        h_ref[sl, :] = (x_ref[sl, :] * rs * gs + shift).astype(BF16)

    _two_stage_rows(tm, rows, sumsq, finish)


def _rope_slab(x, c, s_lo, s_hi, half):
    return x * c + pltpu.roll(x, half, 1) * s_hi + pltpu.roll(x, LANES - half, 1) * s_lo


def _rms_rows(x, g):
    ms = jnp.mean(x * x, axis=-1, keepdims=True)
    return x * lax.rsqrt(ms + NORM_EPS) * g


def _inproj_kernel(x_ref, mod_ref, g_ref, w_ref, cd_ref, slo_ref, shi_ref,
                   qk_ref, v_ref, lat_ref, h_ref, r_ref, *, rows, lat_cols):
    j = pl.program_id(1)
    tn = r_ref.shape[1]

    def project():
        return _dot_nt(h_ref[...], w_ref[...].astype(BF16))

    @pl.when(j == 0)
    def _():
        _norm_modulate_rows(x_ref, g_ref, mod_ref, 0, 1, h_ref, rows)
        r_ref[...] = project()

    @pl.when((j >= 1) & (j <= 4))
    def _():
        c, s_lo, s_hi = cd_ref[...], slo_ref[...], shi_ref[...]
        scale = jnp.where(j <= 2, DIFF_SCALE, 1.0)
        for k in range(tn // LANES):
            sl = slice(k * LANES, (k + 1) * LANES)
            qk_ref[:, sl] = (_rope_slab(r_ref[:, sl], c, s_lo, s_hi, DIFF_ROT_DIM // 2)
                             * scale).astype(BF16)
        r_ref[...] = project()

    @pl.when(j == 5)
    def _():
        v_ref[:, :tn] = r_ref[...].astype(BF16)
        r_ref[...] = project()

    @pl.when(j == 6)
    def _():
        v_ref[:, tn:] = r_ref[...].astype(BF16)
        r_ref[...] = project()

    @pl.when(j == 7)
    def _():
        lat_ref[:, :tn] = r_ref[...]
        lat_ref[:, tn:tn + lat_cols] = project()[:, :lat_cols]
        pad = lat_ref.shape[1] - (tn + lat_cols)
        lat_ref[:, tn + lat_cols:] = jnp.zeros((lat_ref.shape[0], pad), F32)


def _inproj_call(x2, mod3, g, w_t, tabs_d, dw, q_rank, kv_rank, seq, tm=1024):
    t, d = x2.shape
    tpb = seq // tm
    lat_cols = kv_rank + MLA_ROPE_DIM
    lat_w = q_rank + kv_rank + LANES
    tn = dw // 2
    assert dw % (2 * LANES) == 0 and seq % tm == 0 and q_rank == tn and lat_cols <= tn
    assert w_t.shape[0] == 3 * dw + q_rank + lat_cols
    tab = pl.BlockSpec((tm, LANES), lambda i, j: (i, 0))
    return pl.pallas_call(
        functools.partial(_inproj_kernel, rows=64, lat_cols=lat_cols),
        out_shape=(jax.ShapeDtypeStruct((t, 2 * dw), BF16),
                   jax.ShapeDtypeStruct((t, dw), BF16),
                   jax.ShapeDtypeStruct((t, lat_w), F32)),
        grid=(t // tm, 8),
        in_specs=[pl.BlockSpec((tm, d), lambda i, j: (i, 0)),
                  pl.BlockSpec((1, N_MOD, d), lambda i, j: (i // tpb, 0, 0)),
                  pl.BlockSpec((1, d), lambda i, j: (0, 0)),
                  pl.BlockSpec((tn, d), lambda i, j: (j, 0)),
                  tab, tab, tab],
        out_specs=(pl.BlockSpec((tm, tn), lambda i, j: (i, jnp.clip(j - 1, 0, 3))),
                   pl.BlockSpec((tm, dw), lambda i, j: (i, 0)),
                   pl.BlockSpec((tm, lat_w), lambda i, j: (i, 0))),
        scratch_shapes=[pltpu.VMEM((tm, d), BF16), pltpu.VMEM((tm, tn), F32)],
        compiler_params=_cparams(("arbitrary", "arbitrary")),
        name="inproj",
    )(x2, mod3, g, w_t, *tabs_d)


def _latent_kernel(ql_ref, kvl_ref, kpe_ref, cm_ref, smlo_ref, smhi_ref,
                   gq_ref, gkv_ref, wq_ref, wkv_ref, qmo_ref, kmo_ref, vmo_ref, *, n_heads):
    cm, smlo, smhi = cm_ref[...], smlo_ref[...], smhi_ref[...]
    half_m = MLA_ROPE_DIM // 2

    q = _dot(_rms_rows(ql_ref[...], gq_ref[...]).astype(BF16), wq_ref[...])
    kv = _dot(_rms_rows(kvl_ref[...], gkv_ref[...]).astype(BF16), wkv_ref[...])
    kr = _rope_slab(kpe_ref[...], cm, smlo, smhi, half_m).astype(BF16)

    for h in range(n_heads):
        base = h * MLA_QK_PAD
        nope = slice(base, base + LANES)
        rope = slice(base + LANES, base + 2 * LANES)
        qmo_ref[:, nope] = (q[:, nope] * MLA_SCALE).astype(BF16)
        qmo_ref[:, rope] = (_rope_slab(q[:, rope], cm, smlo, smhi, half_m) * MLA_SCALE).astype(BF16)
        kmo_ref[:, nope] = kv[:, nope].astype(BF16)
        kmo_ref[:, rope] = kr
        vmo_ref[:, h * LANES:(h + 1) * LANES] = kv[:, rope].astype(BF16)


def _latent_call(lat, tabs_m, gq, gkv, wq_pad, wkv, n_heads, tm=512):
    t = lat.shape[0]
    q_rank = gq.shape[1]
    kv_rank = gkv.shape[1]
    qk_w = n_heads * MLA_QK_PAD
    v_w = n_heads * MLA_V_DIM
    assert q_rank % kv_rank == 0 and (q_rank + kv_rank) % LANES == 0
    row = lambda w, c: pl.BlockSpec((tm, w), lambda i, c=c: (i, c))
    full = lambda a: pl.BlockSpec(a.shape, lambda i: (0, 0))
    in_specs = [row(q_rank, 0), row(kv_rank, q_rank // kv_rank),
                row(LANES, (q_rank + kv_rank) // LANES)]
    in_specs += [row(LANES, 0)] * 3
    in_specs += [full(gq), full(gkv), full(wq_pad), full(wkv)]
    out_shape = (jax.ShapeDtypeStruct((t, qk_w), BF16),
                 jax.ShapeDtypeStruct((t, qk_w), BF16),
                 jax.ShapeDtypeStruct((t, v_w), BF16))
    return pl.pallas_call(
        functools.partial(_latent_kernel, n_heads=n_heads),
        out_shape=out_shape,
        grid=(t // tm,),
        in_specs=in_specs,
        out_specs=(row(qk_w, 0), row(qk_w, 0), row(v_w, 0)),
        compiler_params=_cparams(("arbitrary",)),
        name="latent",
    )(lat, lat, lat, *tabs_m, gq, gkv, wq_pad, wkv)


def _numerators(s_ref, e_ref, rows=16):
    for r in range(0, s_ref.shape[0], rows):
        s = s_ref[r:r + rows, :]
        e_ref[r:r + rows, :] = jnp.exp2(s - jnp.max(s, axis=-1, keepdims=True)).astype(BF16)


def _weighted_values(e_ref, v1):
    o = _dot(e_ref[...], v1)
    dv = v1.shape[1] // 2
    return o[:, :dv], o[:, dv:dv + 1]


ATTN_TILES = 4
NUMER_LAG = 2


def _attn_step(q_ref, k_ref, v_ref, o_ref, s_refs, e_refs, make_lhs, finish):
    n = ATTN_TILES
    tq = o_ref.shape[1] // n

    @pl.when(pl.program_id(0) == 0)
    def _():
        for t in range(n - NUMER_LAG, n):
            s_refs[t][...] = jnp.zeros_like(s_refs[t])
        for t in range(n - NUMER_LAG):
            e_refs[t][...] = jnp.zeros_like(e_refs[t])

    v = v_ref[0]
    v1 = jnp.concatenate([v, jnp.ones_like(v)], axis=1)
    for k in range(n):
        rows = slice(k * tq, (k + 1) * tq)
        behind = (k - NUMER_LAG) % n
        s_refs[k][...] = _dot_nt(make_lhs(q_ref[0, rows]), k_ref[0])
        _numerators(s_refs[behind], e_refs[behind])
        o_ref[0, rows] = finish(*_weighted_values(e_refs[k], v1)).astype(o_ref.dtype)


def _attn_specs(b, n_heads, seq, tq, qk_width, v_width, q_col=0, k_col=0, v_col=0):
    rows = ATTN_TILES * tq
    n_groups = seq // rows
    total = b * n_heads * n_groups
    assert seq % rows == 0

    def decode(g):
        return g // (n_heads * n_groups), (g // n_groups) % n_heads, g % n_groups

    cur = lambda g: decode(jnp.minimum(g, total - 1))
    prev = lambda g: decode(jnp.maximum(g - 1, 0))
    q_spec = pl.BlockSpec((1, rows, qk_width),
                          lambda g: (cur(g)[0], cur(g)[2], q_col + cur(g)[1]))
    k_spec = pl.BlockSpec((1, seq, qk_width), lambda g: (cur(g)[0], 0, k_col + cur(g)[1]))
    v_spec = pl.BlockSpec((1, seq, v_width), lambda g: (prev(g)[0], 0, v_col + prev(g)[1]))
    o_spec = pl.BlockSpec((1, rows, v_width), lambda g: (prev(g)[0], prev(g)[2], prev(g)[1]))
    return total + 1, q_spec, k_spec, v_spec, o_spec


def _attn_scratch(rows, seq):
    return ([pltpu.VMEM((rows, seq), F32)] * ATTN_TILES
            + [pltpu.VMEM((rows, seq), BF16)] * ATTN_TILES)


def _diffattn_kernel(lq1_ref, lk1_ref, lq2_ref, lk2_ref, g_ref, q_ref, k_ref, v_ref, o_ref,
                     *scratch, lambda_init):
    lam = (jnp.exp(jnp.sum(lq1_ref[...] * lk1_ref[...], axis=-1, keepdims=True))
           - jnp.exp(jnp.sum(lq2_ref[...] * lk2_ref[...], axis=-1, keepdims=True))
           + lambda_init)
    gain = g_ref[...] * (1.0 - lambda_init)

    def make_lhs(q):
        lane = lax.broadcasted_iota(jnp.int32, q.shape, 1)
        zero = jnp.zeros_like(q)
        return jnp.concatenate([jnp.where(lane < DIFF_HEAD_DIM, q, zero),
                                jnp.where(lane >= DIFF_HEAD_DIM, q, zero)], axis=0)

    def finish(o, l):
        tq = o.shape[0] // 2
        out = o[:tq] / l[:tq] - lam * (o[tq:] / l[tq:])
        ms = jnp.mean(out * out, axis=-1, keepdims=True)
        return out * lax.rsqrt(ms + NORM_EPS) * gain

    _attn_step(q_ref, k_ref, v_ref, o_ref, scratch[:ATTN_TILES], scratch[ATTN_TILES:],
               make_lhs, finish)


def _diffattn_call(lams, g, qk, v, n_heads, lambda_init, tq=256):
    b, s, _ = qk.shape
    steps, q_spec, k_spec, v_spec, o_spec = _attn_specs(
        b, n_heads, s, tq, LANES, LANES, q_col=0, k_col=n_heads, v_col=0)
    vec = lambda a: pl.BlockSpec(a.shape, lambda g: (0, 0))
    return pl.pallas_call(
        functools.partial(_diffattn_kernel, lambda_init=lambda_init),
        out_shape=jax.ShapeDtypeStruct((b, s, n_heads * DIFF_V_DIM), BF16),
        grid=(steps,),
        in_specs=[vec(lams[0]), vec(lams[1]), vec(lams[2]), vec(lams[3]), vec(g),
                  q_spec, k_spec, v_spec],
        out_specs=o_spec,
        scratch_shapes=_attn_scratch(2 * tq, s),
        compiler_params=_cparams(("arbitrary",)),
        name="diffattn",
    )(*lams, g, qk, qk, v)


def _mlaattn_kernel(q_ref, k_ref, v_ref, o_ref, *scratch):
    _attn_step(q_ref, k_ref, v_ref, o_ref, scratch[:ATTN_TILES], scratch[ATTN_TILES:],
               lambda q: q, lambda o, l: o / l)


def _mlaattn_call(qm, km, vm, n_heads, tq=512):
    b, s, _ = qm.shape
    steps, q_spec, k_spec, v_spec, o_spec = _attn_specs(b, n_heads, s, tq, MLA_QK_PAD, MLA_V_DIM)
    return pl.pallas_call(
        _mlaattn_kernel,
        out_shape=jax.ShapeDtypeStruct(vm.shape, BF16),
        grid=(steps,),
        in_specs=[q_spec, k_spec, v_spec],
        out_specs=o_spec,
        scratch_shapes=_attn_scratch(tq, s),
        compiler_params=_cparams(("arbitrary",)),
        name="mlaattn",
    )(qm, km, vm)


def _outproj_kernel(od_ref, om_ref, wt_ref, wb_ref, x_ref, mod_ref, o_ref):
    acc = _dot(od_ref[...], wt_ref[...].astype(BF16))
    acc += _dot(om_ref[...], wb_ref[...].astype(BF16))
    o_ref[...] = x_ref[...] + mod_ref[0, 2:3, :] * acc


def _outproj_call(od, om, w_out, x2, mod3, seq, tm=2048, tn=512):
    t, d = x2.shape
    kd = od.shape[1]
    tpb = seq // tm
    return pl.pallas_call(
        _outproj_kernel,
        out_shape=jax.ShapeDtypeStruct((t, d), F32),
        grid=(t // tm, d // tn),
        in_specs=[pl.BlockSpec((tm, kd), lambda i, j: (i, 0)),
                  pl.BlockSpec((tm, om.shape[1]), lambda i, j: (i, 0)),
                  pl.BlockSpec((kd, tn), lambda i, j: (0, j)),
                  pl.BlockSpec((om.shape[1], tn), lambda i, j: (kd // om.shape[1], j)),
                  pl.BlockSpec((tm, tn), lambda i, j: (i, j)),
                  pl.BlockSpec((1, N_MOD, tn), lambda i, j: (i // tpb, 0, j))],
        out_specs=pl.BlockSpec((tm, tn), lambda i, j: (i, j)),
        compiler_params=_cparams(("arbitrary", "arbitrary")),
        name="outproj",
    )(od, om, w_out, w_out, x2, mod3)


def _ffn_kernel(x_ref, mod_ref, g_ref, w1a_ref, w1b_ref, w2p_ref, w2a_ref, gf_ref, o_ref,
                h_ref, ra_ref, rb_ref, *, rows):
    j = pl.program_id(1)
    nj = pl.num_programs(1) - 1
    act = lambda u: jnp.square(jnp.maximum(u, 0.0)).astype(BF16)

    @pl.when(j == 0)
    def _():
        _norm_modulate_rows(x_ref, g_ref, mod_ref, 3, 4, h_ref, rows)
        ra_ref[...] = act(_dot(h_ref[...], w1a_ref[...].astype(BF16)))
        u1 = _dot(h_ref[...], w1b_ref[...].astype(BF16))
        o_ref[...] = _dot(ra_ref[...], w2a_ref[...].astype(BF16))
        rb_ref[...] = act(u1)

    @pl.when((j > 0) & (j < nj))
    def _():
        u0 = _dot(h_ref[...], w1a_ref[...].astype(BF16))
        o_ref[...] += _dot(rb_ref[...], w2p_ref[...].astype(BF16))
        ra_ref[...] = act(u0)
        u1 = _dot(h_ref[...], w1b_ref[...].astype(BF16))
        o_ref[...] += _dot(ra_ref[...], w2a_ref[...].astype(BF16))
        rb_ref[...] = act(u1)

    @pl.when(j == nj)
    def _():
        o_ref[...] += _dot(rb_ref[...], w2p_ref[...].astype(BF16))
        d = x_ref.shape[1]
        gate = mod_ref[0, 5:6, :]
        gf = gf_ref[...]
        resid = lambda sl: x_ref[sl, :] + gate * o_ref[sl, :]

        def sumsq(sl):
            y = resid(sl)
            return jnp.sum(y * y, axis=-1, keepdims=True)

        def finish(sl, ss):
            o_ref[sl, :] = resid(sl) * lax.rsqrt(ss * (1.0 / d) + NORM_EPS) * gf

        _two_stage_rows(x_ref.shape[0], rows, sumsq, finish)


def _ffn_call(x1, mod3, g, w1, w2, gf, seq, tm=1024, tf=256):
    t, d = x1.shape
    n_chunks = w1.shape[1] // tf
    nj = n_chunks // 2
    tpb = seq // tm
    last_a, last_b = n_chunks - 2, n_chunks - 1
    return pl.pallas_call(
        functools.partial(_ffn_kernel, rows=64),
        out_shape=jax.ShapeDtypeStruct((t, d), F32),
        grid=(t // tm, nj + 1),
        in_specs=[pl.BlockSpec((tm, d), lambda i, j: (i, 0)),
                  pl.BlockSpec((1, N_MOD, d), lambda i, j: (i // tpb, 0, 0)),
                  pl.BlockSpec((1, d), lambda i, j: (0, 0)),
                  pl.BlockSpec((d, tf), lambda i, j: (0, jnp.minimum(2 * j, last_a))),
                  pl.BlockSpec((d, tf), lambda i, j: (0, jnp.minimum(2 * j + 1, last_b))),
                  pl.BlockSpec((tf, d), lambda i, j: (jnp.maximum(2 * j - 1, 0), 0)),
                  pl.BlockSpec((tf, d), lambda i, j: (jnp.minimum(2 * j, last_a), 0)),
                  pl.BlockSpec((1, d), lambda i, j: (0, 0))],
        out_specs=pl.BlockSpec((tm, d), lambda i, j: (i, 0)),
        scratch_shapes=[pltpu.VMEM((tm, d), BF16),
                        pltpu.VMEM((tm, tf), BF16),
                        pltpu.VMEM((tm, tf), BF16)],
        compiler_params=_cparams(("arbitrary", "arbitrary")),
        name="ffn",
    )(x1, mod3, g, w1, w1, w2, w2, gf)


def _rope_tables(positions, rot_dim, theta, group):
    half = rot_dim // 2
    inv = 1.0 / (theta ** (jnp.arange(0, rot_dim, 2, dtype=F32) / rot_dim))
    ang = positions.astype(F32).reshape(-1, 1) * inv[None, :]
    cos, sin = jnp.cos(ang), jnp.sin(ang)
    t = ang.shape[0]
    rest = group - rot_dim
    widen = lambda lo, hi, fill: jnp.tile(
        jnp.concatenate([lo, hi, jnp.full((t, rest), fill, F32)], axis=1), (1, LANES // group))
    zeros = jnp.zeros_like(sin)
    return (widen(cos, cos, 1.0), widen(-sin, zeros, 0.0), widen(zeros, sin, 0.0))


def kernel(x, c, positions, w_ada, b_ada, g_norm_mix, w_in, lambda_q1, lambda_k1, lambda_q2,
           lambda_k2, g_diff_sub, g_q_a, w_q_b, g_kv_a, w_kv_b, w_out, g_norm_ffn, w_ff1, w_ff2,
           g_final):
    b, s, d = x.shape
    depth = w_ada.shape[0]
    t = b * s
    q_rank = g_q_a.shape[1]
    kv_rank = g_kv_a.shape[1]
    n_mla = w_kv_b.shape[2] // (MLA_NOPE_DIM + MLA_V_DIM)
    n_diff = (w_in.shape[2] - q_rank - kv_rank - MLA_ROPE_DIM) // (3 * DIFF_V_DIM)
    assert n_diff == n_mla

    tabs_d = _rope_tables(positions, DIFF_ROT_DIM, ROPE_THETA, DIFF_HEAD_DIM)
    tabs_m = _rope_tables(positions, MLA_ROPE_DIM, MLA_ROPE_THETA, LANES)
    c_pad = jnp.pad(c, ((0, 8 - b), (0, 0)))
    x2 = x.reshape(t, d)

    for l in range(depth):
        lambda_init = 0.8 - 0.6 * float(np.exp(-0.3 * l))
        mod = _mod_call(c_pad, w_ada[l], b_ada[l][None, :])
        mod3 = mod[:b].reshape(b, N_MOD, d)

        dqk, dv, lat = _inproj_call(x2, mod3, g_norm_mix[l][None, :],
                                    jnp.swapaxes(w_in[l], 0, 1), tabs_d, n_diff * DIFF_V_DIM,
                                    q_rank, kv_rank, s)

        wq_pad = jnp.pad(w_q_b[l].reshape(q_rank, n_mla, MLA_NOPE_DIM + MLA_ROPE_DIM),
                         ((0, 0), (0, 0), (0, MLA_QK_PAD - MLA_NOPE_DIM - MLA_ROPE_DIM))
                         ).reshape(q_rank, n_mla * MLA_QK_PAD).astype(BF16)
        qm, km, vm = _latent_call(lat, tabs_m, g_q_a[l][None, :], g_kv_a[l][None, :], wq_pad,
                                  w_kv_b[l].astype(BF16), n_mla)

        shp = lambda a: a.reshape(b, s, a.shape[1])
        lams = tuple(v[l][None, :] for v in (lambda_q1, lambda_k1, lambda_q2, lambda_k2))
        o_diff = _diffattn_call(lams, g_diff_sub[l][None, :], shp(dqk), shp(dv), n_diff,
                                lambda_init)
        o_mla = _mlaattn_call(shp(qm), shp(km), shp(vm), n_mla)

        x2 = _outproj_call(o_diff.reshape(t, -1), o_mla.reshape(t, -1), w_out[l], x2, mod3, s)

        last = l == depth - 1
        assert last, "final rmsnorm is fused into the last layer's MLP kernel"
        x2 = _ffn_call(x2, mod3, g_norm_ffn[l][None, :], w_ff1[l], w_ff2[l], g_final[None, :], s)

    return x2.reshape(b, s, d)
```

```python
import functools
import math

import jax
import jax.numpy as jnp
import numpy as np
from jax import lax
from jax.experimental import pallas as pl
from jax.experimental.pallas import tpu as pltpu

F32 = jnp.float32
BF16 = jnp.bfloat16

LANES = 128
NORM_EPS = 1e-6
N_MOD = 6

DIFF_HEAD_DIM = 64
DIFF_V_DIM = 2 * DIFF_HEAD_DIM
DIFF_ROT_DIM = DIFF_HEAD_DIM // 4
ROPE_THETA = 500000.0
MLA_V_DIM = 128
MLA_NOPE_DIM = 128
MLA_ROPE_DIM = 64
MLA_ROPE_THETA = 10000.0
MLA_QK_PAD = 256

DIFF_SCALE = DIFF_HEAD_DIM ** -0.5 * math.log2(math.e)
MLA_SCALE = (MLA_NOPE_DIM + MLA_ROPE_DIM) ** -0.5 * math.log2(math.e)

MIB = 1024 * 1024
VMEM_LIMIT = 56 * MIB


def _cparams(sem, vmem_limit=VMEM_LIMIT):
    return pltpu.CompilerParams(dimension_semantics=sem, vmem_limit_bytes=vmem_limit)


def _dot(a, b):
    return jnp.dot(a, b, preferred_element_type=F32)


def _dot_nt(a, b):
    return lax.dot_general(a, b, (((1,), (1,)), ((), ())), preferred_element_type=F32)


def _mod_kernel(c_ref, w_ref, b_ref, o_ref):
    c = c_ref[...]
    sc = c / (1.0 + jnp.exp(-c))
    o_ref[...] = _dot(sc.astype(BF16), w_ref[...].astype(BF16)) + b_ref[...]


def _mod_call(c_pad, w_ada, b_ada, tn=1024):
    m, d = c_pad.shape
    n = w_ada.shape[1]
    return pl.pallas_call(
        _mod_kernel,
        out_shape=jax.ShapeDtypeStruct((m, n), F32),
        grid=(n // tn,),
        in_specs=[pl.BlockSpec((m, d), lambda j: (0, 0)),
                  pl.BlockSpec((d, tn), lambda j: (0, j)),
                  pl.BlockSpec((1, tn), lambda j: (0, j))],
        out_specs=pl.BlockSpec((m, tn), lambda j: (0, j)),
        compiler_params=_cparams(("arbitrary",)),
        name="mod",
    )(c_pad, w_ada, b_ada)


def _two_stage_rows(n_rows, rows, sumsq, finish):
    chunk = lambda r: pl.ds(pl.multiple_of(r * rows, rows), rows)

    def body(r, ss_prev):
        ss = sumsq(chunk(r))
        finish(chunk(r - 1), ss_prev)
        return ss

    n = n_rows // rows
    ss_last = lax.fori_loop(1, n, body, sumsq(chunk(0)))
    finish(chunk(n - 1), ss_last)


def _norm_modulate_rows(x_ref, g_ref, mod_ref, shift_row, scale_row, h_ref, rows):
    tm, d = x_ref.shape
    shift = mod_ref[0, shift_row:shift_row + 1, :]
    gs = g_ref[...] * (1.0 + mod_ref[0, scale_row:scale_row + 1, :])

    def sumsq(sl):
        x = x_ref[sl, :]
        return jnp.sum(x * x, axis=-1, keepdims=True)

    def finish(sl, ss):
        rs = lax.rsqrt(ss * (1.0 / d) + NORM_EPS)
        h_ref[sl, :] = (x_ref[sl, :] * rs * gs + shift).astype(BF16)

    _two_stage_rows(tm, rows, sumsq, finish)


def _rope_slab(x, c, s_lo, s_hi, half):
    return x * c + pltpu.roll(x, half, 1) * s_hi + pltpu.roll(x, LANES - half, 1) * s_lo


def _rms_rows(x, g):
    ms = jnp.mean(x * x, axis=-1, keepdims=True)
    return x * lax.rsqrt(ms + NORM_EPS) * g


def _inproj_kernel(x_ref, mod_ref, g_ref, w_ref, cd_ref, slo_ref, shi_ref,
                   qk_ref, v_ref, lat_ref, h_ref, r_ref, *, rows, lat_cols):
    j = pl.program_id(1)
    tn = r_ref.shape[1]

    def project():
        return _dot_nt(h_ref[...], w_ref[...].astype(BF16))

    @pl.when(j == 0)
    def _():
        _norm_modulate_rows(x_ref, g_ref, mod_ref, 0, 1, h_ref, rows)
        r_ref[...] = project()

    @pl.when((j >= 1) & (j <= 4))
    def _():
        c, s_lo, s_hi = cd_ref[...], slo_ref[...], shi_ref[...]
        scale = jnp.where(j <= 2, DIFF_SCALE, 1.0)
        for k in range(tn // LANES):
            sl = slice(k * LANES, (k + 1) * LANES)
            qk_ref[:, sl] = (_rope_slab(r_ref[:, sl], c, s_lo, s_hi, DIFF_ROT_DIM // 2)
                             * scale).astype(BF16)
        r_ref[...] = project()

    @pl.when(j == 5)
    def _():
        v_ref[:, :tn] = r_ref[...].astype(BF16)
        r_ref[...] = project()

    @pl.when(j == 6)
    def _():
        v_ref[:, tn:] = r_ref[...].astype(BF16)
        r_ref[...] = project()

    @pl.when(j == 7)
    def _():
        lat_ref[:, :tn] = r_ref[...]
        lat_ref[:, tn:tn + lat_cols] = project()[:, :lat_cols]
        pad = lat_ref.shape[1] - (tn + lat_cols)
        lat_ref[:, tn + lat_cols:] = jnp.zeros((lat_ref.shape[0], pad), F32)


def _inproj_call(x2, mod3, g, w_t, tabs_d, dw, q_rank, kv_rank, seq, tm=1024):
    t, d = x2.shape
    tpb = seq // tm
    lat_cols = kv_rank + MLA_ROPE_DIM
    lat_w = q_rank + kv_rank + LANES
    tn = dw // 2
    assert dw % (2 * LANES) == 0 and seq % tm == 0 and q_rank == tn and lat_cols <= tn
    assert w_t.shape[0] == 3 * dw + q_rank + lat_cols
    tab = pl.BlockSpec((tm, LANES), lambda i, j: (i, 0))
    return pl.pallas_call(
        functools.partial(_inproj_kernel, rows=64, lat_cols=lat_cols),
        out_shape=(jax.ShapeDtypeStruct((t, 2 * dw), BF16),
                   jax.ShapeDtypeStruct((t, dw), BF16),
                   jax.ShapeDtypeStruct((t, lat_w), F32)),
        grid=(t // tm, 8),
        in_specs=[pl.BlockSpec((tm, d), lambda i, j: (i, 0)),
                  pl.BlockSpec((1, N_MOD, d), lambda i, j: (i // tpb, 0, 0)),
                  pl.BlockSpec((1, d), lambda i, j: (0, 0)),
                  pl.BlockSpec((tn, d), lambda i, j: (j, 0)),
                  tab, tab, tab],
        out_specs=(pl.BlockSpec((tm, tn), lambda i, j: (i, jnp.clip(j - 1, 0, 3))),
                   pl.BlockSpec((tm, dw), lambda i, j: (i, 0)),
                   pl.BlockSpec((tm, lat_w), lambda i, j: (i, 0))),
        scratch_shapes=[pltpu.VMEM((tm, d), BF16), pltpu.VMEM((tm, tn), F32)],
        compiler_params=_cparams(("arbitrary", "arbitrary"), 50 * MIB),
        name="inproj",
    )(x2, mod3, g, w_t, *tabs_d)


def _latent_kernel(lat_ref, cm_ref, smlo_ref, smhi_ref,
                   gq_ref, gkv_ref, wq_ref, wkv_ref, qmo_ref, kmo_ref, vmo_ref, *, n_heads):
    cm, smlo, smhi = cm_ref[...], smlo_ref[...], smhi_ref[...]
    half_m = MLA_ROPE_DIM // 2
    q_rank, kv_rank = gq_ref.shape[1], gkv_ref.shape[1]

    q = _dot(_rms_rows(lat_ref[:, :q_rank], gq_ref[...]).astype(BF16), wq_ref[...])
    kv = _dot(_rms_rows(lat_ref[:, q_rank:q_rank + kv_rank], gkv_ref[...]).astype(BF16),
              wkv_ref[...])
    kr = _rope_slab(lat_ref[:, q_rank + kv_rank:], cm, smlo, smhi, half_m).astype(BF16)

    for h in range(n_heads):
        base = h * MLA_QK_PAD
        nope = slice(base, base + LANES)
        rope = slice(base + LANES, base + 2 * LANES)
        qmo_ref[:, nope] = (q[:, nope] * MLA_SCALE).astype(BF16)
        qmo_ref[:, rope] = (_rope_slab(q[:, rope], cm, smlo, smhi, half_m) * MLA_SCALE).astype(BF16)
        kmo_ref[:, nope] = kv[:, nope].astype(BF16)
        kmo_ref[:, rope] = kr
        vmo_ref[:, h * LANES:(h + 1) * LANES] = kv[:, rope].astype(BF16)


def _latent_call(lat, tabs_m, gq, gkv, wq_pad, wkv, n_heads, tm=512):
    t = lat.shape[0]
    q_rank = gq.shape[1]
    kv_rank = gkv.shape[1]
    qk_w = n_heads * MLA_QK_PAD
    v_w = n_heads * MLA_V_DIM
    assert q_rank % LANES == 0 and kv_rank % LANES == 0
    assert lat.shape[1] == q_rank + kv_rank + LANES
    row = lambda w, c: pl.BlockSpec((tm, w), lambda i, c=c: (i, c))
    full = lambda a: pl.BlockSpec(a.shape, lambda i: (0, 0))
    in_specs = [row(lat.shape[1], 0)]
    in_specs += [row(LANES, 0)] * 3
    in_specs += [full(gq), full(gkv), full(wq_pad), full(wkv)]
    out_shape = (jax.ShapeDtypeStruct((t, qk_w), BF16),
                 jax.ShapeDtypeStruct((t, qk_w), BF16),
                 jax.ShapeDtypeStruct((t, v_w), BF16))
    return pl.pallas_call(
        functools.partial(_latent_kernel, n_heads=n_heads),
        out_shape=out_shape,
        grid=(t // tm,),
        in_specs=in_specs,
        out_specs=(row(qk_w, 0), row(qk_w, 0), row(v_w, 0)),
        compiler_params=_cparams(("arbitrary",)),
        name="latent",
    )(lat, *tabs_m, gq, gkv, wq_pad, wkv)


def _numerators(s_ref, e_ref, rows=16):
    for r in range(0, s_ref.shape[0], rows):
        s = s_ref[r:r + rows, :]
        e_ref[r:r + rows, :] = jnp.exp2(s - jnp.max(s, axis=-1, keepdims=True)).astype(BF16)


def _weighted_values(e_ref, v1):
    o = _dot(e_ref[...], v1)
    dv = v1.shape[1] // 2
    return o[:, :dv], o[:, dv:dv + 1]


ATTN_TILES = 4
NUMER_LAG = 2


def _attn_step(q_ref, k_ref, v_ref, o_ref, s_refs, e_refs, make_lhs, finish):
    n = ATTN_TILES
    tq = o_ref.shape[1] // n

    @pl.when(pl.program_id(0) == 0)
    def _():
        for t in range(n - NUMER_LAG, n):
            s_refs[t][...] = jnp.zeros_like(s_refs[t])
        for t in range(n - NUMER_LAG):
            e_refs[t][...] = jnp.zeros_like(e_refs[t])

    v = v_ref[0]
    v1 = jnp.concatenate([v, jnp.ones_like(v)], axis=1)
    for k in range(n):
        rows = slice(k * tq, (k + 1) * tq)
        behind = (k - NUMER_LAG) % n
        s_refs[k][...] = _dot_nt(make_lhs(q_ref[0, rows]), k_ref[0])
        _numerators(s_refs[behind], e_refs[behind])
        o_ref[0, rows] = finish(*_weighted_values(e_refs[k], v1)).astype(o_ref.dtype)


def _attn_specs(b, n_heads, seq, tq, qk_width, v_width, q_col=0, k_col=0, v_col=0):
    rows = ATTN_TILES * tq
    n_groups = seq // rows
    total = b * n_heads * n_groups
    assert seq % rows == 0

    def decode(g):
        return g // (n_heads * n_groups), (g // n_groups) % n_heads, g % n_groups

    cur = lambda g: decode(jnp.minimum(g, total - 1))
    prev = lambda g: decode(jnp.maximum(g - 1, 0))
    q_spec = pl.BlockSpec((1, rows, qk_width),
                          lambda g: (cur(g)[0], cur(g)[2], q_col + cur(g)[1]))
    k_spec = pl.BlockSpec((1, seq, qk_width), lambda g: (cur(g)[0], 0, k_col + cur(g)[1]))
    v_spec = pl.BlockSpec((1, seq, v_width), lambda g: (prev(g)[0], 0, v_col + prev(g)[1]))
    o_spec = pl.BlockSpec((1, rows, v_width), lambda g: (prev(g)[0], prev(g)[2], prev(g)[1]))
    return total + 1, q_spec, k_spec, v_spec, o_spec


def _attn_scratch(rows, seq):
    return ([pltpu.VMEM((rows, seq), F32)] * ATTN_TILES
            + [pltpu.VMEM((rows, seq), BF16)] * ATTN_TILES)


def _diffattn_kernel(lq1_ref, lk1_ref, lq2_ref, lk2_ref, g_ref, q_ref, k_ref, v_ref, o_ref,
                     *scratch, lambda_init):
    lam = (jnp.exp(jnp.sum(lq1_ref[...] * lk1_ref[...], axis=-1, keepdims=True))
           - jnp.exp(jnp.sum(lq2_ref[...] * lk2_ref[...], axis=-1, keepdims=True))
           + lambda_init)
    gain = g_ref[...] * (1.0 - lambda_init)

    def make_lhs(q):
        lane = lax.broadcasted_iota(jnp.int32, q.shape, 1)
        zero = jnp.zeros_like(q)
        return jnp.concatenate([jnp.where(lane < DIFF_HEAD_DIM, q, zero),
                                jnp.where(lane >= DIFF_HEAD_DIM, q, zero)], axis=0)

    def finish(o, l):
        tq = o.shape[0] // 2
        out = o[:tq] / l[:tq] - lam * (o[tq:] / l[tq:])
        ms = jnp.mean(out * out, axis=-1, keepdims=True)
        return out * lax.rsqrt(ms + NORM_EPS) * gain

    _attn_step(q_ref, k_ref, v_ref, o_ref, scratch[:ATTN_TILES], scratch[ATTN_TILES:],
               make_lhs, finish)


def _diffattn_call(lams, g, qk, v, n_heads, lambda_init, tq=256):
    b, s, _ = qk.shape
    steps, q_spec, k_spec, v_spec, o_spec = _attn_specs(
        b, n_heads, s, tq, LANES, LANES, q_col=0, k_col=n_heads, v_col=0)
    vec = lambda a: pl.BlockSpec(a.shape, lambda g: (0, 0))
    return pl.pallas_call(
        functools.partial(_diffattn_kernel, lambda_init=lambda_init),
        out_shape=jax.ShapeDtypeStruct((b, s, n_heads * DIFF_V_DIM), BF16),
        grid=(steps,),
        in_specs=[vec(lams[0]), vec(lams[1]), vec(lams[2]), vec(lams[3]), vec(g),
                  q_spec, k_spec, v_spec],
        out_specs=o_spec,
        scratch_shapes=_attn_scratch(2 * tq, s),
        compiler_params=_cparams(("arbitrary",)),
        name="diffattn",
    )(*lams, g, qk, qk, v)


def _mlaattn_kernel(q_ref, k_ref, v_ref, o_ref, *scratch):
    _attn_step(q_ref, k_ref, v_ref, o_ref, scratch[:ATTN_TILES], scratch[ATTN_TILES:],
               lambda q: q, lambda o, l: o / l)


def _mlaattn_call(qm, km, vm, n_heads, tq=512):
    b, s, _ = qm.shape
    steps, q_spec, k_spec, v_spec, o_spec = _attn_specs(b, n_heads, s, tq, MLA_QK_PAD, MLA_V_DIM)
    return pl.pallas_call(
        _mlaattn_kernel,
        out_shape=jax.ShapeDtypeStruct(vm.shape, BF16),
        grid=(steps,),
        in_specs=[q_spec, k_spec, v_spec],
        out_specs=o_spec,
        scratch_shapes=_attn_scratch(tq, s),
        compiler_params=_cparams(("arbitrary",)),
        name="mlaattn",
    )(qm, km, vm)


def _outproj_kernel(od_ref, om_ref, w_ref, x_ref, mod_ref, o_ref, wb_ref):
    @pl.when(pl.program_id(0) == 0)
    def _():
        wb_ref[...] = w_ref[...].astype(BF16)

    kd = od_ref.shape[1]
    acc = _dot(od_ref[...], wb_ref[:kd, :])
    acc += _dot(om_ref[...], wb_ref[kd:, :])
    o_ref[...] = x_ref[...] + mod_ref[0, 2:3, :] * acc


def _outproj_call(od, om, w_out, x2, mod3, seq, tm=512):
    t, d = x2.shape
    tpb = seq // tm
    assert seq % tm == 0 and w_out.shape[0] == od.shape[1] + om.shape[1]
    return pl.pallas_call(
        _outproj_kernel,
        out_shape=jax.ShapeDtypeStruct((t, d), F32),
        grid=(t // tm,),
        in_specs=[pl.BlockSpec((tm, od.shape[1]), lambda i: (i, 0)),
                  pl.BlockSpec((tm, om.shape[1]), lambda i: (i, 0)),
                  pl.BlockSpec(w_out.shape, lambda i: (0, 0), pipeline_mode=pl.Buffered(1)),
                  pl.BlockSpec((tm, d), lambda i: (i, 0)),
                  pl.BlockSpec((1, N_MOD, d), lambda i: (i // tpb, 0, 0))],
        out_specs=pl.BlockSpec((tm, d), lambda i: (i, 0)),
        scratch_shapes=[pltpu.VMEM(w_out.shape, BF16)],
        compiler_params=_cparams(("arbitrary",)),
        name="outproj",
    )(od, om, w_out, x2, mod3)


def _ffn_kernel(x_ref, mod_ref, g_ref, w1a_ref, w1b_ref, w2p_ref, w2a_ref, gf_ref, o_ref,
                h_ref, ra_ref, rb_ref, *, rows):
    j = pl.program_id(1)
    nj = pl.num_programs(1) - 1
    act = lambda u: jnp.square(jnp.maximum(u, 0.0)).astype(BF16)

    @pl.when(j == 0)
    def _():
        _norm_modulate_rows(x_ref, g_ref, mod_ref, 3, 4, h_ref, rows)
        ra_ref[...] = act(_dot(h_ref[...], w1a_ref[...].astype(BF16)))
        u1 = _dot(h_ref[...], w1b_ref[...].astype(BF16))
        o_ref[...] = _dot(ra_ref[...], w2a_ref[...].astype(BF16))
        rb_ref[...] = act(u1)

    @pl.when((j > 0) & (j < nj))
    def _():
        u0 = _dot(h_ref[...], w1a_ref[...].astype(BF16))
        o_ref[...] += _dot(rb_ref[...], w2p_ref[...].astype(BF16))
        ra_ref[...] = act(u0)
        u1 = _dot(h_ref[...], w1b_ref[...].astype(BF16))
        o_ref[...] += _dot(ra_ref[...], w2a_ref[...].astype(BF16))
        rb_ref[...] = act(u1)

    @pl.when(j == nj)
    def _():
        o_ref[...] += _dot(rb_ref[...], w2p_ref[...].astype(BF16))
        d = x_ref.shape[1]
        gate = mod_ref[0, 5:6, :]
        gf = gf_ref[...]
        resid = lambda sl: x_ref[sl, :] + gate * o_ref[sl, :]

        def sumsq(sl):
            y = resid(sl)
            return jnp.sum(y * y, axis=-1, keepdims=True)

        def finish(sl, ss):
            o_ref[sl, :] = resid(sl) * lax.rsqrt(ss * (1.0 / d) + NORM_EPS) * gf

        _two_stage_rows(x_ref.shape[0], rows, sumsq, finish)


def _ffn_call(x1, mod3, g, w1, w2, gf, seq, tm=1024, tf=256):
    t, d = x1.shape
    n_chunks = w1.shape[1] // tf
    nj = n_chunks // 2
    tpb = seq // tm
    last_a, last_b = n_chunks - 2, n_chunks - 1
    return pl.pallas_call(
        functools.partial(_ffn_kernel, rows=64),
        out_shape=jax.ShapeDtypeStruct((t, d), F32),
        grid=(t // tm, nj + 1),
        in_specs=[pl.BlockSpec((tm, d), lambda i, j: (i, 0)),
                  pl.BlockSpec((1, N_MOD, d), lambda i, j: (i // tpb, 0, 0)),
                  pl.BlockSpec((1, d), lambda i, j: (0, 0)),
                  pl.BlockSpec((d, tf), lambda i, j: (0, jnp.minimum(2 * j, last_a))),
                  pl.BlockSpec((d, tf), lambda i, j: (0, jnp.minimum(2 * j + 1, last_b))),
                  pl.BlockSpec((tf, d), lambda i, j: (jnp.maximum(2 * j - 1, 0), 0)),
                  pl.BlockSpec((tf, d), lambda i, j: (jnp.minimum(2 * j, last_a), 0)),
                  pl.BlockSpec((1, d), lambda i, j: (0, 0))],
        out_specs=pl.BlockSpec((tm, d), lambda i, j: (i, 0)),
        scratch_shapes=[pltpu.VMEM((tm, d), BF16),
                        pltpu.VMEM((tm, tf), BF16),
                        pltpu.VMEM((tm, tf), BF16)],
        compiler_params=_cparams(("arbitrary", "arbitrary")),
        name="ffn",
    )(x1, mod3, g, w1, w1, w2, w2, gf)


def _rope_tables(positions, rot_dim, theta, group):
    inv = 1.0 / (theta ** (jnp.arange(0, rot_dim, 2, dtype=F32) / rot_dim))
    ang = positions.astype(F32).reshape(-1, 1) * inv[None, :]
    cos, sin = jnp.cos(ang), jnp.sin(ang)
    t = ang.shape[0]
    rest = group - rot_dim
    widen = lambda lo, hi, fill: jnp.tile(
        jnp.concatenate([lo, hi, jnp.full((t, rest), fill, F32)], axis=1), (1, LANES // group))
    zeros = jnp.zeros_like(sin)
    return (widen(cos, cos, 1.0), widen(-sin, zeros, 0.0), widen(zeros, sin, 0.0))


def kernel(x, c, positions, w_ada, b_ada, g_norm_mix, w_in, lambda_q1, lambda_k1, lambda_q2,
           lambda_k2, g_diff_sub, g_q_a, w_q_b, g_kv_a, w_kv_b, w_out, g_norm_ffn, w_ff1, w_ff2,
           g_final):
    b, s, d = x.shape
    depth = w_ada.shape[0]
    t = b * s
    q_rank = g_q_a.shape[1]
    kv_rank = g_kv_a.shape[1]
    n_mla = w_kv_b.shape[2] // (MLA_NOPE_DIM + MLA_V_DIM)
    n_diff = (w_in.shape[2] - q_rank - kv_rank - MLA_ROPE_DIM) // (3 * DIFF_V_DIM)
    assert n_diff == n_mla

    tabs_d = _rope_tables(positions, DIFF_ROT_DIM, ROPE_THETA, DIFF_HEAD_DIM)
    tabs_m = _rope_tables(positions, MLA_ROPE_DIM, MLA_ROPE_THETA, LANES)
    c_pad = jnp.pad(c, ((0, 8 - b), (0, 0)))
    x2 = x.reshape(t, d)

    for l in range(depth):
        lambda_init = 0.8 - 0.6 * float(np.exp(-0.3 * l))
        mod = _mod_call(c_pad, w_ada[l], b_ada[l][None, :])
        mod3 = mod[:b].reshape(b, N_MOD, d)

        dqk, dv, lat = _inproj_call(x2, mod3, g_norm_mix[l][None, :],
                                    jnp.swapaxes(w_in[l], 0, 1), tabs_d, n_diff * DIFF_V_DIM,
                                    q_rank, kv_rank, s)

        wq_pad = jnp.pad(w_q_b[l].reshape(q_rank, n_mla, MLA_NOPE_DIM + MLA_ROPE_DIM),
                         ((0, 0), (0, 0), (0, MLA_QK_PAD - MLA_NOPE_DIM - MLA_ROPE_DIM))
                         ).reshape(q_rank, n_mla * MLA_QK_PAD).astype(BF16)
        qm, km, vm = _latent_call(lat, tabs_m, g_q_a[l][None, :], g_kv_a[l][None, :], wq_pad,
                                  w_kv_b[l].astype(BF16), n_mla)

        shp = lambda a: a.reshape(b, s, a.shape[1])
        lams = tuple(v[l][None, :] for v in (lambda_q1, lambda_k1, lambda_q2, lambda_k2))
        o_diff = _diffattn_call(lams, g_diff_sub[l][None, :], shp(dqk), shp(dv), n_diff,
                                lambda_init)
        o_mla = _mlaattn_call(shp(qm), shp(km), shp(vm), n_mla)

        x2 = _outproj_call(o_diff.reshape(t, -1), o_mla.reshape(t, -1), w_out[l], x2, mod3, s)

        last = l == depth - 1
        assert last, "final rmsnorm is fused into the last layer's MLP kernel"
        x2 = _ffn_call(x2, mod3, g_norm_ffn[l][None, :], w_ff1[l], w_ff2[l], g_final[None, :], s)

    return x2.reshape(b, s, d)
```

```python
import functools
import math

import jax
import jax.numpy as jnp
import numpy as np
from jax import lax
from jax.experimental import pallas as pl
from jax.experimental.pallas import tpu as pltpu

F32 = jnp.float32
BF16 = jnp.bfloat16

LANES = 128
NORM_EPS = 1e-6
N_MOD = 6

DIFF_HEAD_DIM = 64
DIFF_V_DIM = 2 * DIFF_HEAD_DIM
DIFF_ROT_DIM = DIFF_HEAD_DIM // 4
ROPE_THETA = 500000.0
MLA_V_DIM = 128
MLA_NOPE_DIM = 128
MLA_ROPE_DIM = 64
MLA_ROPE_THETA = 10000.0
MLA_QK_PAD = 256

DIFF_SCALE = DIFF_HEAD_DIM ** -0.5 * math.log2(math.e)
MLA_SCALE = (MLA_NOPE_DIM + MLA_ROPE_DIM) ** -0.5 * math.log2(math.e)

MIB = 1024 * 1024
VMEM_LIMIT = 56 * MIB


def _cparams(sem, vmem_limit=VMEM_LIMIT):
    return pltpu.CompilerParams(dimension_semantics=sem, vmem_limit_bytes=vmem_limit)


def _dot(a, b):
    return jnp.dot(a, b, preferred_element_type=F32)


def _dot_nt(a, b):
    return lax.dot_general(a, b, (((1,), (1,)), ((), ())), preferred_element_type=F32)


def _mod_kernel(c_ref, w_ref, b_ref, o_ref):
    c = c_ref[...]
    sc = c / (1.0 + jnp.exp(-c))
    o_ref[...] = _dot(sc.astype(BF16), w_ref[...].astype(BF16)) + b_ref[...]


def _mod_call(c_pad, w_ada, b_ada, tn=1024):
    m, d = c_pad.shape
    n = w_ada.shape[1]
    return pl.pallas_call(
        _mod_kernel,
        out_shape=jax.ShapeDtypeStruct((m, n), F32),
        grid=(n // tn,),
        in_specs=[pl.BlockSpec((m, d), lambda j: (0, 0)),
                  pl.BlockSpec((d, tn), lambda j: (0, j)),
                  pl.BlockSpec((1, tn), lambda j: (0, j))],
        out_specs=pl.BlockSpec((m, tn), lambda j: (0, j)),
        compiler_params=_cparams(("arbitrary",)),
        name="mod",
    )(c_pad, w_ada, b_ada)


def _two_stage_rows(n_rows, rows, sumsq, finish):
    chunk = lambda r: pl.ds(pl.multiple_of(r * rows, rows), rows)

    def body(r, ss_prev):
        ss = sumsq(chunk(r))
        finish(chunk(r - 1), ss_prev)
        return ss

    n = n_rows // rows
    ss_last = lax.fori_loop(1, n, body, sumsq(chunk(0)))
    finish(chunk(n - 1), ss_last)


def _norm_modulate_rows(x_ref, g_ref, mod_ref, shift_row, scale_row, h_ref, rows):
    tm, d = x_ref.shape
    shift = mod_ref[0, shift_row:shift_row + 1, :]
    gs = g_ref[...] * (1.0 + mod_ref[0, scale_row:scale_row + 1, :])

    def sumsq(sl):
        x = x_ref[sl, :]
        return jnp.sum(x * x, axis=-1, keepdims=True)

    def finish(sl, ss):
        rs = lax.rsqrt(ss * (1.0 / d) + NORM_EPS)
        h_ref[sl, :] = (x_ref[sl, :] * rs * gs + shift).astype(BF16)

    _two_stage_rows(tm, rows, sumsq, finish)


def _rope_slab(x, c, s_lo, s_hi, half):
    return x * c + pltpu.roll(x, half, 1) * s_hi + pltpu.roll(x, LANES - half, 1) * s_lo


def _rms_rows(x, g):
    ms = jnp.mean(x * x, axis=-1, keepdims=True)
    return x * lax.rsqrt(ms + NORM_EPS) * g


def _inproj_kernel(x_ref, mod_ref, g_ref, w_ref, cd_ref, slo_ref, shi_ref,
                   qk_ref, v_ref, lat_ref, h_ref, r_ref, *, rows, lat_cols):
    j = pl.program_id(1)
    tn = r_ref.shape[1]

    def project():
        return _dot_nt(h_ref[...], w_ref[...].astype(BF16))

    @pl.when(j == 0)
    def _():
        _norm_modulate_rows(x_ref, g_ref, mod_ref, 0, 1, h_ref, rows)
        r_ref[...] = project()

    @pl.when((j >= 1) & (j <= 4))
    def _():
        repeat = lambda ref: jnp.concatenate([ref[...]] * (LANES // ref.shape[1]), axis=1)
        c, s_lo, s_hi = repeat(cd_ref), repeat(slo_ref), repeat(shi_ref)
        scale = jnp.where(j <= 2, DIFF_SCALE, 1.0)
        for k in range(tn // LANES):
            sl = slice(k * LANES, (k + 1) * LANES)
            qk_ref[:, sl] = (_rope_slab(r_ref[:, sl], c, s_lo, s_hi, DIFF_ROT_DIM // 2)
                             * scale).astype(BF16)
        r_ref[...] = project()

    @pl.when(j == 5)
    def _():
        v_ref[:, :tn] = r_ref[...].astype(BF16)
        r_ref[...] = project()

    @pl.when(j == 6)
    def _():
        v_ref[:, tn:] = r_ref[...].astype(BF16)
        r_ref[...] = project()

    @pl.when(j == 7)
    def _():
        lat_ref[:, :tn] = r_ref[...]
        lat_ref[:, tn:tn + lat_cols] = project()[:, :lat_cols]
        pad = lat_ref.shape[1] - (tn + lat_cols)
        lat_ref[:, tn + lat_cols:] = jnp.zeros((lat_ref.shape[0], pad), F32)


def _inproj_call(x2, mod3, g, w_t, tabs_d, dw, q_rank, kv_rank, seq, tm=1024):
    t, d = x2.shape
    tpb = seq // tm
    lat_cols = kv_rank + MLA_ROPE_DIM
    lat_w = q_rank + kv_rank + LANES
    tn = dw // 2
    assert dw % (2 * LANES) == 0 and seq % tm == 0 and q_rank == tn and lat_cols <= tn
    assert w_t.shape[0] == 3 * dw + q_rank + lat_cols
    tab = pl.BlockSpec((tm, tabs_d[0].shape[1]), lambda i, j: (i, 0))
    return pl.pallas_call(
        functools.partial(_inproj_kernel, rows=64, lat_cols=lat_cols),
        out_shape=(jax.ShapeDtypeStruct((t, 2 * dw), BF16),
                   jax.ShapeDtypeStruct((t, dw), BF16),
                   jax.ShapeDtypeStruct((t, lat_w), F32)),
        grid=(t // tm, 8),
        in_specs=[pl.BlockSpec((tm, d), lambda i, j: (i, 0)),
                  pl.BlockSpec((1, N_MOD, d), lambda i, j: (i // tpb, 0, 0)),
                  pl.BlockSpec((1, d), lambda i, j: (0, 0)),
                  pl.BlockSpec((tn, d), lambda i, j: (j, 0)),
                  tab, tab, tab],
        out_specs=(pl.BlockSpec((tm, tn), lambda i, j: (i, jnp.clip(j - 1, 0, 3))),
                   pl.BlockSpec((tm, dw), lambda i, j: (i, 0)),
                   pl.BlockSpec((tm, lat_w), lambda i, j: (i, 0))),
        scratch_shapes=[pltpu.VMEM((tm, d), BF16), pltpu.VMEM((tm, tn), F32)],
        compiler_params=_cparams(("arbitrary", "arbitrary"), 50 * MIB),
        name="inproj",
    )(x2, mod3, g, w_t, *tabs_d)


def _latent_kernel(lat_ref, cm_ref, smlo_ref, smhi_ref,
                   gq_ref, gkv_ref, wq_ref, wkv_ref, qmo_ref, kmo_ref, vmo_ref, *, n_heads):
    cm, smlo, smhi = cm_ref[...], smlo_ref[...], smhi_ref[...]
    half_m = MLA_ROPE_DIM // 2
    q_rank, kv_rank = gq_ref.shape[1], gkv_ref.shape[1]

    q = _dot(_rms_rows(lat_ref[:, :q_rank], gq_ref[...]).astype(BF16), wq_ref[...])
    kv = _dot(_rms_rows(lat_ref[:, q_rank:q_rank + kv_rank], gkv_ref[...]).astype(BF16),
              wkv_ref[...])
    kr = _rope_slab(lat_ref[:, q_rank + kv_rank:], cm, smlo, smhi, half_m).astype(BF16)

    for h in range(n_heads):
        base = h * MLA_QK_PAD
        nope = slice(base, base + LANES)
        rope = slice(base + LANES, base + 2 * LANES)
        qmo_ref[:, nope] = (q[:, nope] * MLA_SCALE).astype(BF16)
        qmo_ref[:, rope] = (_rope_slab(q[:, rope], cm, smlo, smhi, half_m) * MLA_SCALE).astype(BF16)
        kmo_ref[:, nope] = kv[:, nope].astype(BF16)
        kmo_ref[:, rope] = kr
        vmo_ref[:, h * LANES:(h + 1) * LANES] = kv[:, rope].astype(BF16)


def _latent_call(lat, tabs_m, gq, gkv, wq_pad, wkv, n_heads, tm=512):
    t = lat.shape[0]
    q_rank = gq.shape[1]
    kv_rank = gkv.shape[1]
    qk_w = n_heads * MLA_QK_PAD
    v_w = n_heads * MLA_V_DIM
    assert q_rank % LANES == 0 and kv_rank % LANES == 0
    assert lat.shape[1] == q_rank + kv_rank + LANES
    row = lambda w, c: pl.BlockSpec((tm, w), lambda i, c=c: (i, c))
    full = lambda a: pl.BlockSpec(a.shape, lambda i: (0, 0))
    in_specs = [row(lat.shape[1], 0)]
    in_specs += [row(LANES, 0)] * 3
    in_specs += [full(gq), full(gkv), full(wq_pad), full(wkv)]
    out_shape = (jax.ShapeDtypeStruct((t, qk_w), BF16),
                 jax.ShapeDtypeStruct((t, qk_w), BF16),
                 jax.ShapeDtypeStruct((t, v_w), BF16))
    return pl.pallas_call(
        functools.partial(_latent_kernel, n_heads=n_heads),
        out_shape=out_shape,
        grid=(t // tm,),
        in_specs=in_specs,
        out_specs=(row(qk_w, 0), row(qk_w, 0), row(v_w, 0)),
        compiler_params=_cparams(("arbitrary",)),
        name="latent",
    )(lat, *tabs_m, gq, gkv, wq_pad, wkv)


def _numerators(s_ref, e_ref, rows=16):
    for r in range(0, s_ref.shape[0], rows):
        s = s_ref[r:r + rows, :]
        e_ref[r:r + rows, :] = jnp.exp2(s - jnp.max(s, axis=-1, keepdims=True)).astype(BF16)


def _weighted_values(e_ref, v1):
    o = _dot(e_ref[...], v1)
    dv = v1.shape[1] // 2
    return o[:, :dv], o[:, dv:dv + 1]


ATTN_TILES = 4
NUMER_LAG = 2


def _attn_step(q_ref, k_ref, v_ref, o_ref, s_refs, e_refs, make_lhs, finish):
    n = ATTN_TILES
    tq = o_ref.shape[1] // n

    @pl.when(pl.program_id(0) == 0)
    def _():
        for t in range(n - NUMER_LAG, n):
            s_refs[t][...] = jnp.zeros_like(s_refs[t])
        for t in range(n - NUMER_LAG):
            e_refs[t][...] = jnp.zeros_like(e_refs[t])

    v = v_ref[0]
    v1 = jnp.concatenate([v, jnp.ones_like(v)], axis=1)
    for k in range(n):
        rows = slice(k * tq, (k + 1) * tq)
        behind = (k - NUMER_LAG) % n
        s_refs[k][...] = _dot_nt(make_lhs(q_ref[0, rows]), k_ref[0])
        _numerators(s_refs[behind], e_refs[behind])
        o_ref[0, rows] = finish(*_weighted_values(e_refs[k], v1)).astype(o_ref.dtype)


def _attn_specs(b, n_heads, seq, tq, qk_width, v_width, q_col=0, k_col=0, v_col=0):
    rows = ATTN_TILES * tq
    n_groups = seq // rows
    total = b * n_heads * n_groups
    assert seq % rows == 0

    def decode(g):
        return g // (n_heads * n_groups), (g // n_groups) % n_heads, g % n_groups

    cur = lambda g: decode(jnp.minimum(g, total - 1))
    prev = lambda g: decode(jnp.maximum(g - 1, 0))
    q_spec = pl.BlockSpec((1, rows, qk_width),
                          lambda g: (cur(g)[0], cur(g)[2], q_col + cur(g)[1]))
    k_spec = pl.BlockSpec((1, seq, qk_width), lambda g: (cur(g)[0], 0, k_col + cur(g)[1]))
    v_spec = pl.BlockSpec((1, seq, v_width), lambda g: (prev(g)[0], 0, v_col + prev(g)[1]))
    o_spec = pl.BlockSpec((1, rows, v_width), lambda g: (prev(g)[0], prev(g)[2], prev(g)[1]))
    return total + 1, q_spec, k_spec, v_spec, o_spec


def _attn_scratch(rows, seq):
    return ([pltpu.VMEM((rows, seq), F32)] * ATTN_TILES
            + [pltpu.VMEM((rows, seq), BF16)] * ATTN_TILES)


def _diffattn_kernel(lq1_ref, lk1_ref, lq2_ref, lk2_ref, g_ref, q_ref, k_ref, v_ref, o_ref,
                     *scratch, lambda_init):
    lam = (jnp.exp(jnp.sum(lq1_ref[...] * lk1_ref[...], axis=-1, keepdims=True))
           - jnp.exp(jnp.sum(lq2_ref[...] * lk2_ref[...], axis=-1, keepdims=True))
           + lambda_init)
    gain = g_ref[...] * (1.0 - lambda_init)

    def make_lhs(q):
        lane = lax.broadcasted_iota(jnp.int32, q.shape, 1)
        zero = jnp.zeros_like(q)
        return jnp.concatenate([jnp.where(lane < DIFF_HEAD_DIM, q, zero),
                                jnp.where(lane >= DIFF_HEAD_DIM, q, zero)], axis=0)

    def finish(o, l):
        tq = o.shape[0] // 2
        out = o[:tq] / l[:tq] - lam * (o[tq:] / l[tq:])
        ms = jnp.mean(out * out, axis=-1, keepdims=True)
        return out * lax.rsqrt(ms + NORM_EPS) * gain

    _attn_step(q_ref, k_ref, v_ref, o_ref, scratch[:ATTN_TILES], scratch[ATTN_TILES:],
               make_lhs, finish)


def _diffattn_call(lams, g, qk, v, n_heads, lambda_init, tq=256):
    b, s, _ = qk.shape
    steps, q_spec, k_spec, v_spec, o_spec = _attn_specs(
        b, n_heads, s, tq, LANES, LANES, q_col=0, k_col=n_heads, v_col=0)
    vec = lambda a: pl.BlockSpec(a.shape, lambda g: (0, 0))
    return pl.pallas_call(
        functools.partial(_diffattn_kernel, lambda_init=lambda_init),
        out_shape=jax.ShapeDtypeStruct((b, s, n_heads * DIFF_V_DIM), BF16),
        grid=(steps,),
        in_specs=[vec(lams[0]), vec(lams[1]), vec(lams[2]), vec(lams[3]), vec(g),
                  q_spec, k_spec, v_spec],
        out_specs=o_spec,
        scratch_shapes=_attn_scratch(2 * tq, s),
        compiler_params=_cparams(("arbitrary",)),
        name="diffattn",
    )(*lams, g, qk, qk, v)


def _mlaattn_kernel(q_ref, k_ref, v_ref, o_ref, *scratch):
    _attn_step(q_ref, k_ref, v_ref, o_ref, scratch[:ATTN_TILES], scratch[ATTN_TILES:],
               lambda q: q, lambda o, l: o / l)


def _mlaattn_call(qm, km, vm, n_heads, tq=512):
    b, s, _ = qm.shape
    steps, q_spec, k_spec, v_spec, o_spec = _attn_specs(b, n_heads, s, tq, MLA_QK_PAD, MLA_V_DIM)
    return pl.pallas_call(
        _mlaattn_kernel,
        out_shape=jax.ShapeDtypeStruct(vm.shape, BF16),
        grid=(steps,),
        in_specs=[q_spec, k_spec, v_spec],
        out_specs=o_spec,
        scratch_shapes=_attn_scratch(tq, s),
        compiler_params=_cparams(("arbitrary",)),
        name="mlaattn",
    )(qm, km, vm)


def _outproj_kernel(od_ref, om_ref, w_ref, x_ref, mod_ref, o_ref, wb_ref):
    @pl.when(pl.program_id(0) == 0)
    def _():
        wb_ref[...] = w_ref[...].astype(BF16)

    kd = od_ref.shape[1]
    acc = _dot(od_ref[...], wb_ref[:kd, :])
    acc += _dot(om_ref[...], wb_ref[kd:, :])
    o_ref[...] = x_ref[...] + mod_ref[0, 2:3, :] * acc


def _outproj_call(od, om, w_out, x2, mod3, seq, tm=512):
    t, d = x2.shape
    tpb = seq // tm
    assert seq % tm == 0 and w_out.shape[0] == od.shape[1] + om.shape[1]
    return pl.pallas_call(
        _outproj_kernel,
        out_shape=jax.ShapeDtypeStruct((t, d), F32),
        grid=(t // tm,),
        in_specs=[pl.BlockSpec((tm, od.shape[1]), lambda i: (i, 0)),
                  pl.BlockSpec((tm, om.shape[1]), lambda i: (i, 0)),
                  pl.BlockSpec(w_out.shape, lambda i: (0, 0), pipeline_mode=pl.Buffered(1)),
                  pl.BlockSpec((tm, d), lambda i: (i, 0)),
                  pl.BlockSpec((1, N_MOD, d), lambda i: (i // tpb, 0, 0))],
        out_specs=pl.BlockSpec((tm, d), lambda i: (i, 0)),
        scratch_shapes=[pltpu.VMEM(w_out.shape, BF16)],
        compiler_params=_cparams(("arbitrary",)),
        name="outproj",
    )(od, om, w_out, x2, mod3)


def _ffn_kernel(x_ref, mod_ref, g_ref, w1a_ref, w1b_ref, w2p_ref, w2a_ref, gf_ref, o_ref,
                h_ref, ra_ref, rb_ref, *, rows):
    j = pl.program_id(1)
    nj = pl.num_programs(1) - 1
    act = lambda u: jnp.square(jnp.maximum(u, 0.0)).astype(BF16)

    @pl.when(j == 0)
    def _():
        _norm_modulate_rows(x_ref, g_ref, mod_ref, 3, 4, h_ref, rows)
        ra_ref[...] = act(_dot(h_ref[...], w1a_ref[...].astype(BF16)))
        u1 = _dot(h_ref[...], w1b_ref[...].astype(BF16))
        o_ref[...] = _dot(ra_ref[...], w2a_ref[...].astype(BF16))
        rb_ref[...] = act(u1)

    @pl.when((j > 0) & (j < nj))
    def _():
        u0 = _dot(h_ref[...], w1a_ref[...].astype(BF16))
        o_ref[...] += _dot(rb_ref[...], w2p_ref[...].astype(BF16))
        ra_ref[...] = act(u0)
        u1 = _dot(h_ref[...], w1b_ref[...].astype(BF16))
        o_ref[...] += _dot(ra_ref[...], w2a_ref[...].astype(BF16))
        rb_ref[...] = act(u1)

    @pl.when(j == nj)
    def _():
        o_ref[...] += _dot(rb_ref[...], w2p_ref[...].astype(BF16))
        d = x_ref.shape[1]
        gate = mod_ref[0, 5:6, :]
        gf = gf_ref[...]
        resid = lambda sl: x_ref[sl, :] + gate * o_ref[sl, :]

        def sumsq(sl):
            y = resid(sl)
            return jnp.sum(y * y, axis=-1, keepdims=True)

        def finish(sl, ss):
            o_ref[sl, :] = resid(sl) * lax.rsqrt(ss * (1.0 / d) + NORM_EPS) * gf

        _two_stage_rows(x_ref.shape[0], rows, sumsq, finish)


def _ffn_call(x1, mod3, g, w1, w2, gf, seq, tm=1024, tf=256):
    t, d = x1.shape
    n_chunks = w1.shape[1] // tf
    nj = n_chunks // 2
    tpb = seq // tm
    last_a, last_b = n_chunks - 2, n_chunks - 1
    return pl.pallas_call(
        functools.partial(_ffn_kernel, rows=64),
        out_shape=jax.ShapeDtypeStruct((t, d), F32),
        grid=(t // tm, nj + 1),
        in_specs=[pl.BlockSpec((tm, d), lambda i, j: (i, 0)),
                  pl.BlockSpec((1, N_MOD, d), lambda i, j: (i // tpb, 0, 0)),
                  pl.BlockSpec((1, d), lambda i, j: (0, 0)),
                  pl.BlockSpec((d, tf), lambda i, j: (0, jnp.minimum(2 * j, last_a))),
                  pl.BlockSpec((d, tf), lambda i, j: (0, jnp.minimum(2 * j + 1, last_b))),
                  pl.BlockSpec((tf, d), lambda i, j: (jnp.maximum(2 * j - 1, 0), 0)),
                  pl.BlockSpec((tf, d), lambda i, j: (jnp.minimum(2 * j, last_a), 0)),
                  pl.BlockSpec((1, d), lambda i, j: (0, 0))],
        out_specs=pl.BlockSpec((tm, d), lambda i, j: (i, 0)),
        scratch_shapes=[pltpu.VMEM((tm, d), BF16),
                        pltpu.VMEM((tm, tf), BF16),
                        pltpu.VMEM((tm, tf), BF16)],
        compiler_params=_cparams(("arbitrary", "arbitrary")),
        name="ffn",
    )(x1, mod3, g, w1, w1, w2, w2, gf)


def _rope_tables(positions, rot_dim, theta, group):
    inv = 1.0 / (theta ** (jnp.arange(0, rot_dim, 2, dtype=F32) / rot_dim))
    ang = positions.astype(F32).reshape(-1, 1) * inv[None, :]
    cos, sin = jnp.cos(ang), jnp.sin(ang)
    t = ang.shape[0]
    rest = group - rot_dim
    widen = lambda lo, hi, fill: jnp.concatenate(
        [lo, hi, jnp.full((t, rest), fill, F32)], axis=1)
    zeros = jnp.zeros_like(sin)
    return (widen(cos, cos, 1.0), widen(-sin, zeros, 0.0), widen(zeros, sin, 0.0))


def kernel(x, c, positions, w_ada, b_ada, g_norm_mix, w_in, lambda_q1, lambda_k1, lambda_q2,
           lambda_k2, g_diff_sub, g_q_a, w_q_b, g_kv_a, w_kv_b, w_out, g_norm_ffn, w_ff1, w_ff2,
           g_final):
    b, s, d = x.shape
    depth = w_ada.shape[0]
    t = b * s
    q_rank = g_q_a.shape[1]
    kv_rank = g_kv_a.shape[1]
    n_mla = w_kv_b.shape[2] // (MLA_NOPE_DIM + MLA_V_DIM)
    n_diff = (w_in.shape[2] - q_rank - kv_rank - MLA_ROPE_DIM) // (3 * DIFF_V_DIM)
    assert n_diff == n_mla

    tabs_d = _rope_tables(positions, DIFF_ROT_DIM, ROPE_THETA, DIFF_HEAD_DIM)
    tabs_m = _rope_tables(positions, MLA_ROPE_DIM, MLA_ROPE_THETA, LANES)
    c_pad = jnp.pad(c, ((0, 8 - b), (0, 0)))
    x2 = x.reshape(t, d)

    for l in range(depth):
        lambda_init = 0.8 - 0.6 * float(np.exp(-0.3 * l))
        mod = _mod_call(c_pad, w_ada[l], b_ada[l][None, :])
        mod3 = mod[:b].reshape(b, N_MOD, d)

        dqk, dv, lat = _inproj_call(x2, mod3, g_norm_mix[l][None, :],
                                    jnp.swapaxes(w_in[l], 0, 1), tabs_d, n_diff * DIFF_V_DIM,
                                    q_rank, kv_rank, s)

        wq_pad = jnp.pad(w_q_b[l].reshape(q_rank, n_mla, MLA_NOPE_DIM + MLA_ROPE_DIM),
                         ((0, 0), (0, 0), (0, MLA_QK_PAD - MLA_NOPE_DIM - MLA_ROPE_DIM))
                         ).reshape(q_rank, n_mla * MLA_QK_PAD).astype(BF16)
        qm, km, vm = _latent_call(lat, tabs_m, g_q_a[l][None, :], g_kv_a[l][None, :], wq_pad,
                                  w_kv_b[l].astype(BF16), n_mla)

        shp = lambda a: a.reshape(b, s, a.shape[1])
        lams = tuple(v[l][None, :] for v in (lambda_q1, lambda_k1, lambda_q2, lambda_k2))
        o_diff = _diffattn_call(lams, g_diff_sub[l][None, :], shp(dqk), shp(dv), n_diff,
                                lambda_init)
        o_mla = _mlaattn_call(shp(qm), shp(km), shp(vm), n_mla)

        x2 = _outproj_call(o_diff.reshape(t, -1), o_mla.reshape(t, -1), w_out[l], x2, mod3, s)

        last = l == depth - 1
        assert last, "final rmsnorm is fused into the last layer's MLP kernel"
        x2 = _ffn_call(x2, mod3, g_norm_ffn[l][None, :], w_ff1[l], w_ff2[l], g_final[None, :], s)

    return x2.reshape(b, s, d)
```

```python
import functools
import math

import jax
import jax.numpy as jnp
import numpy as np
from jax import lax
from jax.experimental import pallas as pl
from jax.experimental.pallas import tpu as pltpu

F32 = jnp.float32
BF16 = jnp.bfloat16

LANES = 128
NORM_EPS = 1e-6
N_MOD = 6

DIFF_HEAD_DIM = 64
DIFF_V_DIM = 2 * DIFF_HEAD_DIM
DIFF_ROT_DIM = DIFF_HEAD_DIM // 4
ROPE_THETA = 500000.0
MLA_V_DIM = 128
MLA_NOPE_DIM = 128
MLA_ROPE_DIM = 64
MLA_ROPE_THETA = 10000.0
MLA_QK_PAD = 256

DIFF_SCALE = DIFF_HEAD_DIM ** -0.5 * math.log2(math.e)
MLA_SCALE = (MLA_NOPE_DIM + MLA_ROPE_DIM) ** -0.5 * math.log2(math.e)

MIB = 1024 * 1024
VMEM_LIMIT = 56 * MIB


def _cparams(sem, vmem_limit=VMEM_LIMIT):
    return pltpu.CompilerParams(dimension_semantics=sem, vmem_limit_bytes=vmem_limit)


def _dot(a, b):
    return jnp.dot(a, b, preferred_element_type=F32)


def _dot_nt(a, b):
    return lax.dot_general(a, b, (((1,), (1,)), ((), ())), preferred_element_type=F32)


def _mod_kernel(c_ref, w_ref, b_ref, o_ref):
    c = c_ref[...]
    sc = c / (1.0 + jnp.exp(-c))
    o_ref[...] = _dot(sc.astype(BF16), w_ref[...].astype(BF16)) + b_ref[...]


def _mod_call(c_pad, w_ada, b_ada, tn=1024):
    m, d = c_pad.shape
    n = w_ada.shape[1]
    return pl.pallas_call(
        _mod_kernel,
        out_shape=jax.ShapeDtypeStruct((m, n), F32),
        grid=(n // tn,),
        in_specs=[pl.BlockSpec((m, d), lambda j: (0, 0)),
                  pl.BlockSpec((d, tn), lambda j: (0, j)),
                  pl.BlockSpec((1, tn), lambda j: (0, j))],
        out_specs=pl.BlockSpec((m, tn), lambda j: (0, j)),
        compiler_params=_cparams(("arbitrary",)),
        name="mod",
    )(c_pad, w_ada, b_ada)


def _two_stage_rows(n_rows, rows, sumsq, finish):
    chunk = lambda r: pl.ds(pl.multiple_of(r * rows, rows), rows)

    def body(r, ss_prev):
        ss = sumsq(chunk(r))
        finish(chunk(r - 1), ss_prev)
        return ss

    n = n_rows // rows
    ss_last = lax.fori_loop(1, n, body, sumsq(chunk(0)))
    finish(chunk(n - 1), ss_last)


def _norm_modulate_rows(x_ref, g_ref, mod_ref, shift_row, scale_row, h_ref, rows):
    tm, d = x_ref.shape
    shift = mod_ref[0, shift_row:shift_row + 1, :]
    gs = g_ref[...] * (1.0 + mod_ref[0, scale_row:scale_row + 1, :])

    def sumsq(sl):
        x = x_ref[sl, :]
        return jnp.sum(x * x, axis=-1, keepdims=True)

    def finish(sl, ss):
        rs = lax.rsqrt(ss * (1.0 / d) + NORM_EPS)
        h_ref[sl, :] = (x_ref[sl, :] * rs * gs + shift).astype(BF16)

    _two_stage_rows(tm, rows, sumsq, finish)


def _rope_slab(x, c, s_lo, s_hi, half):
    return x * c + pltpu.roll(x, half, 1) * s_hi + pltpu.roll(x, LANES - half, 1) * s_lo


def _rms_rows(x, g):
    ms = jnp.mean(x * x, axis=-1, keepdims=True)
    return x * lax.rsqrt(ms + NORM_EPS) * g


def _inproj_kernel(x_ref, mod_ref, g_ref, w_ref, cd_ref, slo_ref, shi_ref,
                   qk_ref, v_ref, lat_ref, h_ref, r_ref, *, rows, lat_cols):
    j = pl.program_id(1)
    tn = r_ref.shape[1]

    def project():
        return _dot_nt(h_ref[...], w_ref[...].astype(BF16))

    @pl.when(j == 0)
    def _():
        _norm_modulate_rows(x_ref, g_ref, mod_ref, 0, 1, h_ref, rows)
        r_ref[...] = project()

    @pl.when((j >= 1) & (j <= 4))
    def _():
        repeat = lambda ref: jnp.concatenate([ref[...]] * (LANES // ref.shape[1]), axis=1)
        c, s_lo, s_hi = repeat(cd_ref), repeat(slo_ref), repeat(shi_ref)
        scale = jnp.where(j <= 2, DIFF_SCALE, 1.0)
        for k in range(tn // LANES):
            sl = slice(k * LANES, (k + 1) * LANES)
            qk_ref[:, sl] = (_rope_slab(r_ref[:, sl], c, s_lo, s_hi, DIFF_ROT_DIM // 2)
                             * scale).astype(BF16)
        r_ref[...] = project()

    @pl.when(j == 5)
    def _():
        v_ref[:, :tn] = r_ref[...].astype(BF16)
        r_ref[...] = project()

    @pl.when(j == 6)
    def _():
        v_ref[:, tn:] = r_ref[...].astype(BF16)
        r_ref[...] = project()

    @pl.when(j == 7)
    def _():
        lat_ref[:, :tn] = r_ref[...]
        lat_ref[:, tn:tn + lat_cols] = project()[:, :lat_cols]
        pad = lat_ref.shape[1] - (tn + lat_cols)
        lat_ref[:, tn + lat_cols:] = jnp.zeros((lat_ref.shape[0], pad), F32)


def _inproj_call(x2, mod3, g, w_t, tabs_d, dw, q_rank, kv_rank, seq, tm=1024):
    t, d = x2.shape
    tpb = seq // tm
    lat_cols = kv_rank + MLA_ROPE_DIM
    lat_w = q_rank + kv_rank + LANES
    tn = dw // 2
    assert dw % (2 * LANES) == 0 and seq % tm == 0 and q_rank == tn and lat_cols <= tn
    assert w_t.shape[0] == 3 * dw + q_rank + lat_cols
    tab = pl.BlockSpec((tm, tabs_d[0].shape[1]), lambda i, j: (i, 0))
    return pl.pallas_call(
        functools.partial(_inproj_kernel, rows=64, lat_cols=lat_cols),
        out_shape=(jax.ShapeDtypeStruct((t, 2 * dw), BF16),
                   jax.ShapeDtypeStruct((t, dw), BF16),
                   jax.ShapeDtypeStruct((t, lat_w), F32)),
        grid=(t // tm, 8),
        in_specs=[pl.BlockSpec((tm, d), lambda i, j: (i, 0)),
                  pl.BlockSpec((1, N_MOD, d), lambda i, j: (i // tpb, 0, 0)),
                  pl.BlockSpec((1, d), lambda i, j: (0, 0)),
                  pl.BlockSpec((tn, d), lambda i, j: (j, 0)),
                  tab, tab, tab],
        out_specs=(pl.BlockSpec((tm, tn), lambda i, j: (i, jnp.clip(j - 1, 0, 3))),
                   pl.BlockSpec((tm, dw), lambda i, j: (i, 0)),
                   pl.BlockSpec((tm, lat_w), lambda i, j: (i, 0))),
        scratch_shapes=[pltpu.VMEM((tm, d), BF16), pltpu.VMEM((tm, tn), F32)],
        compiler_params=_cparams(("arbitrary", "arbitrary"), 50 * MIB),
        name="inproj",
    )(x2, mod3, g, w_t, *tabs_d)


def _latent_kernel(lat_ref, cm_ref, smlo_ref, smhi_ref,
                   gq_ref, gkv_ref, wq_ref, wkv_ref, qmo_ref, kmo_ref, vmo_ref, *, n_heads):
    cm, smlo, smhi = cm_ref[...], smlo_ref[...], smhi_ref[...]
    half_m = MLA_ROPE_DIM // 2
    q_rank, kv_rank = gq_ref.shape[1], gkv_ref.shape[1]

    q = _dot(_rms_rows(lat_ref[:, :q_rank], gq_ref[...]).astype(BF16), wq_ref[...])
    kv = _dot(_rms_rows(lat_ref[:, q_rank:q_rank + kv_rank], gkv_ref[...]).astype(BF16),
              wkv_ref[...])
    kr = _rope_slab(lat_ref[:, q_rank + kv_rank:], cm, smlo, smhi, half_m).astype(BF16)

    for h in range(n_heads):
        base = h * MLA_QK_PAD
        nope = slice(base, base + LANES)
        rope = slice(base + LANES, base + 2 * LANES)
        qmo_ref[:, nope] = (q[:, nope] * MLA_SCALE).astype(BF16)
        qmo_ref[:, rope] = (_rope_slab(q[:, rope], cm, smlo, smhi, half_m) * MLA_SCALE).astype(BF16)
        kmo_ref[:, nope] = kv[:, nope].astype(BF16)
        kmo_ref[:, rope] = kr
        vmo_ref[:, h * LANES:(h + 1) * LANES] = kv[:, rope].astype(BF16)


def _latent_call(lat, tabs_m, gq, gkv, wq_pad, wkv, n_heads, tm=512):
    t = lat.shape[0]
    q_rank = gq.shape[1]
    kv_rank = gkv.shape[1]
    qk_w = n_heads * MLA_QK_PAD
    v_w = n_heads * MLA_V_DIM
    assert q_rank % LANES == 0 and kv_rank % LANES == 0
    assert lat.shape[1] == q_rank + kv_rank + LANES
    row = lambda w, c: pl.BlockSpec((tm, w), lambda i, c=c: (i, c))
    full = lambda a: pl.BlockSpec(a.shape, lambda i: (0, 0))
    in_specs = [row(lat.shape[1], 0)]
    in_specs += [row(LANES, 0)] * 3
    in_specs += [full(gq), full(gkv), full(wq_pad), full(wkv)]
    out_shape = (jax.ShapeDtypeStruct((t, qk_w), BF16),
                 jax.ShapeDtypeStruct((t, qk_w), BF16),
                 jax.ShapeDtypeStruct((t, v_w), BF16))
    return pl.pallas_call(
        functools.partial(_latent_kernel, n_heads=n_heads),
        out_shape=out_shape,
        grid=(t // tm,),
        in_specs=in_specs,
        out_specs=(row(qk_w, 0), row(qk_w, 0), row(v_w, 0)),
        compiler_params=_cparams(("arbitrary",)),
        name="latent",
    )(lat, *tabs_m, gq, gkv, wq_pad, wkv)


def _numerators(s_ref, e_ref, rows=16):
    for r in range(0, s_ref.shape[0], rows):
        s = s_ref[r:r + rows, :]
        e_ref[r:r + rows, :] = jnp.exp2(s - jnp.max(s, axis=-1, keepdims=True)).astype(BF16)


def _weighted_values(e_ref, v1):
    o = _dot(e_ref[...], v1)
    dv = v1.shape[1] // 2
    return o[:, :dv], o[:, dv:dv + 1]


ATTN_TILES = 4
NUMER_LAG = 2


def _attn_step(q_ref, k_ref, v_ref, o_ref, s_refs, e_refs, make_lhs, finish):
    n = ATTN_TILES
    tq = o_ref.shape[1] // n

    @pl.when(pl.program_id(0) == 0)
    def _():
        for t in range(n - NUMER_LAG, n):
            s_refs[t][...] = jnp.zeros_like(s_refs[t])
        for t in range(n - NUMER_LAG):
            e_refs[t][...] = jnp.zeros_like(e_refs[t])

    v = v_ref[0]
    v1 = jnp.concatenate([v, jnp.ones_like(v)], axis=1)
    for k in range(n):
        rows = slice(k * tq, (k + 1) * tq)
        behind = (k - NUMER_LAG) % n
        s_refs[k][...] = _dot_nt(make_lhs(q_ref[0, rows]), k_ref[0])
        _numerators(s_refs[behind], e_refs[behind])
        o_ref[0, rows] = finish(*_weighted_values(e_refs[k], v1)).astype(o_ref.dtype)


def _attn_specs(b, n_heads, seq, tq, qk_width, v_width, q_col=0, k_col=0, v_col=0):
    rows = ATTN_TILES * tq
    n_groups = seq // rows
    total = b * n_heads * n_groups
    assert seq % rows == 0

    def decode(g):
        return g // (n_heads * n_groups), (g // n_groups) % n_heads, g % n_groups

    cur = lambda g: decode(jnp.minimum(g, total - 1))
    prev = lambda g: decode(jnp.maximum(g - 1, 0))
    q_spec = pl.BlockSpec((1, rows, qk_width),
                          lambda g: (cur(g)[0], cur(g)[2], q_col + cur(g)[1]))
    k_spec = pl.BlockSpec((1, seq, qk_width), lambda g: (cur(g)[0], 0, k_col + cur(g)[1]))
    v_spec = pl.BlockSpec((1, seq, v_width), lambda g: (prev(g)[0], 0, v_col + prev(g)[1]))
    o_spec = pl.BlockSpec((1, rows, v_width), lambda g: (prev(g)[0], prev(g)[2], prev(g)[1]))
    return total + 1, q_spec, k_spec, v_spec, o_spec


def _attn_scratch(rows, seq):
    return ([pltpu.VMEM((rows, seq), F32)] * ATTN_TILES
            + [pltpu.VMEM((rows, seq), BF16)] * ATTN_TILES)


def _diffattn_kernel(lq1_ref, lk1_ref, lq2_ref, lk2_ref, g_ref, q_ref, k_ref, v_ref, o_ref,
                     *scratch, lambda_init):
    lam = (jnp.exp(jnp.sum(lq1_ref[...] * lk1_ref[...], axis=-1, keepdims=True))
           - jnp.exp(jnp.sum(lq2_ref[...] * lk2_ref[...], axis=-1, keepdims=True))
           + lambda_init)
    gain = g_ref[...] * (1.0 - lambda_init)

    def make_lhs(q):
        lane = lax.broadcasted_iota(jnp.int32, q.shape, 1)
        zero = jnp.zeros_like(q)
        return jnp.concatenate([jnp.where(lane < DIFF_HEAD_DIM, q, zero),
                                jnp.where(lane >= DIFF_HEAD_DIM, q, zero)], axis=0)

    def finish(o, l):
        tq = o.shape[0] // 2
        out = o[:tq] / l[:tq] - lam * (o[tq:] / l[tq:])
        ms = jnp.mean(out * out, axis=-1, keepdims=True)
        return out * lax.rsqrt(ms + NORM_EPS) * gain

    _attn_step(q_ref, k_ref, v_ref, o_ref, scratch[:ATTN_TILES], scratch[ATTN_TILES:],
               make_lhs, finish)


def _diffattn_call(lams, g, qk, v, n_heads, lambda_init, tq=256):
    b, s, _ = qk.shape
    steps, q_spec, k_spec, v_spec, o_spec = _attn_specs(
        b, n_heads, s, tq, LANES, LANES, q_col=0, k_col=n_heads, v_col=0)
    vec = lambda a: pl.BlockSpec(a.shape, lambda g: (0, 0))
    return pl.pallas_call(
        functools.partial(_diffattn_kernel, lambda_init=lambda_init),
        out_shape=jax.ShapeDtypeStruct((b, s, n_heads * DIFF_V_DIM), BF16),
        grid=(steps,),
        in_specs=[vec(lams[0]), vec(lams[1]), vec(lams[2]), vec(lams[3]), vec(g),
                  q_spec, k_spec, v_spec],
        out_specs=o_spec,
        scratch_shapes=_attn_scratch(2 * tq, s),
        compiler_params=_cparams(("arbitrary",)),
        name="diffattn",
    )(*lams, g, qk, qk, v)


def _mlaattn_kernel(q_ref, k_ref, v_ref, o_ref, *scratch):
    _attn_step(q_ref, k_ref, v_ref, o_ref, scratch[:ATTN_TILES], scratch[ATTN_TILES:],
               lambda q: q, lambda o, l: o / l)


def _mlaattn_call(qm, km, vm, n_heads, tq=512):
    b, s, _ = qm.shape
    steps, q_spec, k_spec, v_spec, o_spec = _attn_specs(b, n_heads, s, tq, MLA_QK_PAD, MLA_V_DIM)
    return pl.pallas_call(
        _mlaattn_kernel,
        out_shape=jax.ShapeDtypeStruct(vm.shape, BF16),
        grid=(steps,),
        in_specs=[q_spec, k_spec, v_spec],
        out_specs=o_spec,
        scratch_shapes=_attn_scratch(tq, s),
        compiler_params=_cparams(("arbitrary",)),
        name="mlaattn",
    )(qm, km, vm)


def _outproj_kernel(od_ref, om_ref, w_ref, x_ref, mod_ref, o_ref, wb_ref):
    @pl.when(pl.program_id(0) == 0)
    def _():
        wb_ref[...] = w_ref[...].astype(BF16)

    kd = od_ref.shape[1]
    acc = _dot(od_ref[...], wb_ref[:kd, :])
    acc += _dot(om_ref[...], wb_ref[kd:, :])
    o_ref[...] = x_ref[...] + mod_ref[0, 2:3, :] * acc


def _outproj_call(od, om, w_out, x2, mod3, seq, tm=512):
    t, d = x2.shape
    tpb = seq // tm
    assert seq % tm == 0 and w_out.shape[0] == od.shape[1] + om.shape[1]
    return pl.pallas_call(
        _outproj_kernel,
        out_shape=jax.ShapeDtypeStruct((t, d), F32),
        grid=(t // tm,),
        in_specs=[pl.BlockSpec((tm, od.shape[1]), lambda i: (i, 0)),
                  pl.BlockSpec((tm, om.shape[1]), lambda i: (i, 0)),
                  pl.BlockSpec(w_out.shape, lambda i: (0, 0), pipeline_mode=pl.Buffered(1)),
                  pl.BlockSpec((tm, d), lambda i: (i, 0)),
                  pl.BlockSpec((1, N_MOD, d), lambda i: (i // tpb, 0, 0))],
        out_specs=pl.BlockSpec((tm, d), lambda i: (i, 0)),
        scratch_shapes=[pltpu.VMEM(w_out.shape, BF16)],
        compiler_params=_cparams(("arbitrary",)),
        name="outproj",
    )(od, om, w_out, x2, mod3)


def _ffn_kernel(x_ref, mod_ref, g_ref, w1a_ref, w1b_ref, w2p_ref, w2a_ref, gf_ref, o_ref,
                h_ref, ra_ref, rb_ref, *, rows):
    j = pl.program_id(1)
    nj = pl.num_programs(1) - 1
    act = lambda u: jnp.square(jnp.maximum(u, 0.0)).astype(BF16)

    @pl.when(j == 0)
    def _():
        _norm_modulate_rows(x_ref, g_ref, mod_ref, 3, 4, h_ref, rows)
        ra_ref[...] = act(_dot(h_ref[...], w1a_ref[...].astype(BF16)))
        u1 = _dot(h_ref[...], w1b_ref[...].astype(BF16))
        o_ref[...] = _dot(ra_ref[...], w2a_ref[...].astype(BF16))
        rb_ref[...] = act(u1)

    @pl.when((j > 0) & (j < nj))
    def _():
        u0 = _dot(h_ref[...], w1a_ref[...].astype(BF16))
        o_ref[...] += _dot(rb_ref[...], w2p_ref[...].astype(BF16))
        ra_ref[...] = act(u0)
        u1 = _dot(h_ref[...], w1b_ref[...].astype(BF16))
        o_ref[...] += _dot(ra_ref[...], w2a_ref[...].astype(BF16))
        rb_ref[...] = act(u1)

    @pl.when(j == nj)
    def _():
        o_ref[...] += _dot(rb_ref[...], w2p_ref[...].astype(BF16))
        d = x_ref.shape[1]
        gate = mod_ref[0, 5:6, :]
        gf = gf_ref[...]
        resid = lambda sl: x_ref[sl, :] + gate * o_ref[sl, :]

        def sumsq(sl):
            y = resid(sl)
            return jnp.sum(y * y, axis=-1, keepdims=True)

        def finish(sl, ss):
            o_ref[sl, :] = resid(sl) * lax.rsqrt(ss * (1.0 / d) + NORM_EPS) * gf

        _two_stage_rows(x_ref.shape[0], rows, sumsq, finish)


def _ffn_call(x1, mod3, g, w1, w2, gf, seq, tm=1024, tf=256):
    t, d = x1.shape
    n_chunks = w1.shape[1] // tf
    nj = n_chunks // 2
    tpb = seq // tm
    last_a, last_b = n_chunks - 2, n_chunks - 1
    return pl.pallas_call(
        functools.partial(_ffn_kernel, rows=64),
        out_shape=jax.ShapeDtypeStruct((t, d), F32),
        grid=(t // tm, nj + 1),
        in_specs=[pl.BlockSpec((tm, d), lambda i, j: (i, 0)),
                  pl.BlockSpec((1, N_MOD, d), lambda i, j: (i // tpb, 0, 0)),
                  pl.BlockSpec((1, d), lambda i, j: (0, 0)),
                  pl.BlockSpec((d, tf), lambda i, j: (0, jnp.minimum(2 * j, last_a))),
                  pl.BlockSpec((d, tf), lambda i, j: (0, jnp.minimum(2 * j + 1, last_b))),
                  pl.BlockSpec((tf, d), lambda i, j: (jnp.maximum(2 * j - 1, 0), 0)),
                  pl.BlockSpec((tf, d), lambda i, j: (jnp.minimum(2 * j, last_a), 0)),
                  pl.BlockSpec((1, d), lambda i, j: (0, 0))],
        out_specs=pl.BlockSpec((tm, d), lambda i, j: (i, 0)),
        scratch_shapes=[pltpu.VMEM((tm, d), BF16),
                        pltpu.VMEM((tm, tf), BF16),
                        pltpu.VMEM((tm, tf), BF16)],
        compiler_params=_cparams(("arbitrary", "arbitrary")),
        name="ffn",
    )(x1, mod3, g, w1, w1, w2, w2, gf)


def _rope_tables(positions, specs):
    inv = jnp.concatenate([1.0 / (theta ** (jnp.arange(0, rot, 2, dtype=F32) / rot))
                           for rot, theta, _ in specs])
    ang = positions.astype(F32).reshape(-1, 1) * inv[None, :]
    cos_all, sin_all = jnp.cos(ang), jnp.sin(ang)
    t = ang.shape[0]
    out, start = [], 0
    for rot, _, group in specs:
        cos, sin = cos_all[:, start:start + rot // 2], sin_all[:, start:start + rot // 2]
        start += rot // 2
        widen = lambda lo, hi, fill: jnp.concatenate(
            [lo, hi, jnp.full((t, group - rot), fill, F32)], axis=1)
        zeros = jnp.zeros_like(sin)
        out.append((widen(cos, cos, 1.0), widen(-sin, zeros, 0.0), widen(zeros, sin, 0.0)))
    return out


def kernel(x, c, positions, w_ada, b_ada, g_norm_mix, w_in, lambda_q1, lambda_k1, lambda_q2,
           lambda_k2, g_diff_sub, g_q_a, w_q_b, g_kv_a, w_kv_b, w_out, g_norm_ffn, w_ff1, w_ff2,
           g_final):
    b, s, d = x.shape
    depth = w_ada.shape[0]
    t = b * s
    q_rank = g_q_a.shape[1]
    kv_rank = g_kv_a.shape[1]
    n_mla = w_kv_b.shape[2] // (MLA_NOPE_DIM + MLA_V_DIM)
    n_diff = (w_in.shape[2] - q_rank - kv_rank - MLA_ROPE_DIM) // (3 * DIFF_V_DIM)
    assert n_diff == n_mla

    tabs_d, tabs_m = _rope_tables(positions, ((DIFF_ROT_DIM, ROPE_THETA, DIFF_HEAD_DIM),
                                             (MLA_ROPE_DIM, MLA_ROPE_THETA, LANES)))
    c_pad = jnp.pad(c, ((0, 8 - b), (0, 0)))
    x2 = x.reshape(t, d)

    for l in range(depth):
        lambda_init = 0.8 - 0.6 * float(np.exp(-0.3 * l))
        mod = _mod_call(c_pad, w_ada[l], b_ada[l][None, :])
        mod3 = mod[:b].reshape(b, N_MOD, d)

        dqk, dv, lat = _inproj_call(x2, mod3, g_norm_mix[l][None, :],
                                    jnp.swapaxes(w_in[l], 0, 1), tabs_d, n_diff * DIFF_V_DIM,
                                    q_rank, kv_rank, s)

        wq_pad = jnp.pad(w_q_b[l].reshape(q_rank, n_mla, MLA_NOPE_DIM + MLA_ROPE_DIM),
                         ((0, 0), (0, 0), (0, MLA_QK_PAD - MLA_NOPE_DIM - MLA_ROPE_DIM))
                         ).reshape(q_rank, n_mla * MLA_QK_PAD).astype(BF16)
        qm, km, vm = _latent_call(lat, tabs_m, g_q_a[l][None, :], g_kv_a[l][None, :], wq_pad,
                                  w_kv_b[l].astype(BF16), n_mla)

        shp = lambda a: a.reshape(b, s, a.shape[1])
        lams = tuple(v[l][None, :] for v in (lambda_q1, lambda_k1, lambda_q2, lambda_k2))
        o_diff = _diffattn_call(lams, g_diff_sub[l][None, :], shp(dqk), shp(dv), n_diff,
                                lambda_init)
        o_mla = _mlaattn_call(shp(qm), shp(km), shp(vm), n_mla)

        x2 = _outproj_call(o_diff.reshape(t, -1), o_mla.reshape(t, -1), w_out[l], x2, mod3, s)

        last = l == depth - 1
        assert last, "final rmsnorm is fused into the last layer's MLP kernel"
        x2 = _ffn_call(x2, mod3, g_norm_ffn[l][None, :], w_ff1[l], w_ff2[l], g_final[None, :], s)

    return x2.reshape(b, s, d)
```

```python
import functools
import math

import jax
import jax.numpy as jnp
import numpy as np
from jax import lax
from jax.experimental import pallas as pl
from jax.experimental.pallas import tpu as pltpu

F32 = jnp.float32
BF16 = jnp.bfloat16

LANES = 128
NORM_EPS = 1e-6
N_MOD = 6

DIFF_HEAD_DIM = 64
DIFF_V_DIM = 2 * DIFF_HEAD_DIM
DIFF_ROT_DIM = DIFF_HEAD_DIM // 4
ROPE_THETA = 500000.0
MLA_V_DIM = 128
MLA_NOPE_DIM = 128
MLA_ROPE_DIM = 64
MLA_ROPE_THETA = 10000.0
MLA_QK_PAD = 256

DIFF_SCALE = DIFF_HEAD_DIM ** -0.5 * math.log2(math.e)
MLA_SCALE = (MLA_NOPE_DIM + MLA_ROPE_DIM) ** -0.5 * math.log2(math.e)

MIB = 1024 * 1024
VMEM_LIMIT = 56 * MIB


def _cparams(sem, vmem_limit=VMEM_LIMIT):
    return pltpu.CompilerParams(dimension_semantics=sem, vmem_limit_bytes=vmem_limit)


def _dot(a, b):
    return jnp.dot(a, b, preferred_element_type=F32)


def _dot_nt(a, b):
    return lax.dot_general(a, b, (((1,), (1,)), ((), ())), preferred_element_type=F32)


def _mod_kernel(c_ref, w_ref, b_ref, o_ref):
    c = c_ref[...]
    sc = c / (1.0 + jnp.exp(-c))
    o_ref[...] = _dot(sc.astype(BF16), w_ref[...].astype(BF16)) + b_ref[...]


def _mod_call(c_pad, w_ada, b_ada, tn=1024):
    m, d = c_pad.shape
    n = w_ada.shape[1]
    return pl.pallas_call(
        _mod_kernel,
        out_shape=jax.ShapeDtypeStruct((m, n), F32),
        grid=(n // tn,),
        in_specs=[pl.BlockSpec((m, d), lambda j: (0, 0)),
                  pl.BlockSpec((d, tn), lambda j: (0, j)),
                  pl.BlockSpec((1, tn), lambda j: (0, j))],
        out_specs=pl.BlockSpec((m, tn), lambda j: (0, j)),
        compiler_params=_cparams(("arbitrary",)),
        name="mod",
    )(c_pad, w_ada, b_ada)


def _two_stage_rows(n_rows, rows, sumsq, finish):
    chunk = lambda r: pl.ds(pl.multiple_of(r * rows, rows), rows)

    def body(r, ss_prev):
        ss = sumsq(chunk(r))
        finish(chunk(r - 1), ss_prev)
        return ss

    n = n_rows // rows
    ss_last = lax.fori_loop(1, n, body, sumsq(chunk(0)))
    finish(chunk(n - 1), ss_last)


def _norm_modulate_rows(x_ref, g_ref, mod_ref, shift_row, scale_row, h_ref, rows):
    tm, d = x_ref.shape
    shift = mod_ref[0, shift_row:shift_row + 1, :]
    gs = g_ref[...] * (1.0 + mod_ref[0, scale_row:scale_row + 1, :])

    def sumsq(sl):
        x = x_ref[sl, :]
        return jnp.sum(x * x, axis=-1, keepdims=True)

    def finish(sl, ss):
        rs = lax.rsqrt(ss * (1.0 / d) + NORM_EPS)
        h_ref[sl, :] = (x_ref[sl, :] * rs * gs + shift).astype(BF16)

    _two_stage_rows(tm, rows, sumsq, finish)


def _rope_slab(x, c, s_lo, s_hi, half):
    return x * c + pltpu.roll(x, half, 1) * s_hi + pltpu.roll(x, LANES - half, 1) * s_lo


def _rms_rows(x, g):
    ms = jnp.mean(x * x, axis=-1, keepdims=True)
    return x * lax.rsqrt(ms + NORM_EPS) * g


def _inproj_kernel(x_ref, mod_ref, g_ref, w_ref, cd_ref, slo_ref, shi_ref,
                   qk_ref, v_ref, lat_ref, h_ref, r_ref, *, rows, lat_cols):
    j = pl.program_id(1)
    tn = r_ref.shape[1]

    def project():
        return _dot_nt(h_ref[...], w_ref[...])

    @pl.when(j == 0)
    def _():
        _norm_modulate_rows(x_ref, g_ref, mod_ref, 0, 1, h_ref, rows)
        r_ref[...] = project()

    @pl.when((j >= 1) & (j <= 4))
    def _():
        repeat = lambda ref: jnp.concatenate([ref[...]] * (LANES // ref.shape[1]), axis=1)
        c, s_lo, s_hi = repeat(cd_ref), repeat(slo_ref), repeat(shi_ref)
        scale = jnp.where(j <= 2, DIFF_SCALE, 1.0)
        for k in range(tn // LANES):
            sl = slice(k * LANES, (k + 1) * LANES)
            qk_ref[:, sl] = (_rope_slab(r_ref[:, sl], c, s_lo, s_hi, DIFF_ROT_DIM // 2)
                             * scale).astype(BF16)
        r_ref[...] = project()

    @pl.when(j == 5)
    def _():
        v_ref[:, :tn] = r_ref[...].astype(BF16)
        r_ref[...] = project()

    @pl.when(j == 6)
    def _():
        v_ref[:, tn:] = r_ref[...].astype(BF16)
        r_ref[...] = project()

    @pl.when(j == 7)
    def _():
        lat_ref[:, :tn] = r_ref[...]
        lat_ref[:, tn:tn + lat_cols] = project()[:, :lat_cols]
        pad = lat_ref.shape[1] - (tn + lat_cols)
        lat_ref[:, tn + lat_cols:] = jnp.zeros((lat_ref.shape[0], pad), F32)


def _inproj_call(x2, mod3, g, w_t, tabs_d, dw, q_rank, kv_rank, seq, tm=1024):
    t, d = x2.shape
    tpb = seq // tm
    lat_cols = kv_rank + MLA_ROPE_DIM
    lat_w = q_rank + kv_rank + LANES
    tn = dw // 2
    assert dw % (2 * LANES) == 0 and seq % tm == 0 and q_rank == tn and lat_cols <= tn
    assert w_t.shape[0] == 3 * dw + q_rank + lat_cols
    tab = pl.BlockSpec((tm, tabs_d[0].shape[1]), lambda i, j: (i, 0))
    return pl.pallas_call(
        functools.partial(_inproj_kernel, rows=64, lat_cols=lat_cols),
        out_shape=(jax.ShapeDtypeStruct((t, 2 * dw), BF16),
                   jax.ShapeDtypeStruct((t, dw), BF16),
                   jax.ShapeDtypeStruct((t, lat_w), F32)),
        grid=(t // tm, 8),
        in_specs=[pl.BlockSpec((tm, d), lambda i, j: (i, 0)),
                  pl.BlockSpec((1, N_MOD, d), lambda i, j: (i // tpb, 0, 0)),
                  pl.BlockSpec((1, d), lambda i, j: (0, 0)),
                  pl.BlockSpec((tn, d), lambda i, j: (j, 0)),
                  tab, tab, tab],
        out_specs=(pl.BlockSpec((tm, tn), lambda i, j: (i, jnp.clip(j - 1, 0, 3))),
                   pl.BlockSpec((tm, dw), lambda i, j: (i, 0)),
                   pl.BlockSpec((tm, lat_w), lambda i, j: (i, 0))),
        scratch_shapes=[pltpu.VMEM((tm, d), BF16), pltpu.VMEM((tm, tn), F32)],
        compiler_params=_cparams(("arbitrary", "arbitrary"), 50 * MIB),
        name="inproj",
    )(x2, mod3, g, w_t, *tabs_d)


def _latent_kernel(lat_ref, cm_ref, smlo_ref, smhi_ref,
                   gq_ref, gkv_ref, wq_ref, wkv_ref, qmo_ref, kmo_ref, vmo_ref, *, n_heads):
    cm, smlo, smhi = cm_ref[...], smlo_ref[...], smhi_ref[...]
    half_m = MLA_ROPE_DIM // 2
    q_rank, kv_rank = gq_ref.shape[1], gkv_ref.shape[1]

    q = _dot(_rms_rows(lat_ref[:, :q_rank], gq_ref[...]).astype(BF16), wq_ref[...])
    kv = _dot(_rms_rows(lat_ref[:, q_rank:q_rank + kv_rank], gkv_ref[...]).astype(BF16),
              wkv_ref[...])
    kr = _rope_slab(lat_ref[:, q_rank + kv_rank:], cm, smlo, smhi, half_m).astype(BF16)

    for h in range(n_heads):
        base = h * MLA_QK_PAD
        nope = slice(base, base + LANES)
        rope = slice(base + LANES, base + 2 * LANES)
        qmo_ref[:, nope] = (q[:, nope] * MLA_SCALE).astype(BF16)
        qmo_ref[:, rope] = (_rope_slab(q[:, rope], cm, smlo, smhi, half_m) * MLA_SCALE).astype(BF16)
        kmo_ref[:, nope] = kv[:, nope].astype(BF16)
        kmo_ref[:, rope] = kr
        vmo_ref[:, h * LANES:(h + 1) * LANES] = kv[:, rope].astype(BF16)


def _latent_call(lat, tabs_m, gq, gkv, wq_pad, wkv, n_heads, tm=1024):
    t = lat.shape[0]
    q_rank = gq.shape[1]
    kv_rank = gkv.shape[1]
    qk_w = n_heads * MLA_QK_PAD
    v_w = n_heads * MLA_V_DIM
    assert q_rank % LANES == 0 and kv_rank % LANES == 0
    assert lat.shape[1] == q_rank + kv_rank + LANES
    row = lambda w, c: pl.BlockSpec((tm, w), lambda i, c=c: (i, c))
    full = lambda a: pl.BlockSpec(a.shape, lambda i: (0, 0))
    in_specs = [row(lat.shape[1], 0)]
    in_specs += [row(LANES, 0)] * 3
    in_specs += [full(gq), full(gkv), full(wq_pad), full(wkv)]
    out_shape = (jax.ShapeDtypeStruct((t, qk_w), BF16),
                 jax.ShapeDtypeStruct((t, qk_w), BF16),
                 jax.ShapeDtypeStruct((t, v_w), BF16))
    return pl.pallas_call(
        functools.partial(_latent_kernel, n_heads=n_heads),
        out_shape=out_shape,
        grid=(t // tm,),
        in_specs=in_specs,
        out_specs=(row(qk_w, 0), row(qk_w, 0), row(v_w, 0)),
        compiler_params=_cparams(("arbitrary",)),
        name="latent",
    )(lat, *tabs_m, gq, gkv, wq_pad, wkv)


def _numerators(s_ref, e_ref, rows=16):
    for r in range(0, s_ref.shape[0], rows):
        s = s_ref[r:r + rows, :]
        e_ref[r:r + rows, :] = jnp.exp2(s - jnp.max(s, axis=-1, keepdims=True)).astype(BF16)


def _weighted_values(e_ref, v1):
    o = _dot(e_ref[...], v1)
    dv = v1.shape[1] // 2
    return o[:, :dv], o[:, dv:dv + 1]


ATTN_TILES = 4
NUMER_LAG = 2


def _attn_step(q_ref, k_ref, v_ref, o_ref, s_refs, e_refs, make_lhs, finish):
    n = ATTN_TILES
    tq = o_ref.shape[1] // n

    @pl.when(pl.program_id(0) == 0)
    def _():
        for t in range(n - NUMER_LAG, n):
            s_refs[t][...] = jnp.zeros_like(s_refs[t])
        for t in range(n - NUMER_LAG):
            e_refs[t][...] = jnp.zeros_like(e_refs[t])

    v = v_ref[0]
    v1 = jnp.concatenate([v, jnp.ones_like(v)], axis=1)
    for k in range(n):
        rows = slice(k * tq, (k + 1) * tq)
        behind = (k - NUMER_LAG) % n
        s_refs[k][...] = _dot_nt(make_lhs(q_ref[0, rows]), k_ref[0])
        _numerators(s_refs[behind], e_refs[behind])
        o_ref[0, rows] = finish(*_weighted_values(e_refs[k], v1)).astype(o_ref.dtype)


def _attn_specs(b, n_heads, seq, tq, qk_width, v_width, q_col=0, k_col=0, v_col=0):
    rows = ATTN_TILES * tq
    n_groups = seq // rows
    total = b * n_heads * n_groups
    assert seq % rows == 0

    def decode(g):
        return g // (n_heads * n_groups), (g // n_groups) % n_heads, g % n_groups

    cur = lambda g: decode(jnp.minimum(g, total - 1))
    prev = lambda g: decode(jnp.maximum(g - 1, 0))
    q_spec = pl.BlockSpec((1, rows, qk_width),
                          lambda g: (cur(g)[0], cur(g)[2], q_col + cur(g)[1]))
    k_spec = pl.BlockSpec((1, seq, qk_width), lambda g: (cur(g)[0], 0, k_col + cur(g)[1]))
    v_spec = pl.BlockSpec((1, seq, v_width), lambda g: (prev(g)[0], 0, v_col + prev(g)[1]))
    o_spec = pl.BlockSpec((1, rows, v_width), lambda g: (prev(g)[0], prev(g)[2], prev(g)[1]))
    return total + 1, q_spec, k_spec, v_spec, o_spec


def _attn_scratch(rows, seq):
    return ([pltpu.VMEM((rows, seq), F32)] * ATTN_TILES
            + [pltpu.VMEM((rows, seq), BF16)] * ATTN_TILES)


def _diffattn_kernel(lq1_ref, lk1_ref, lq2_ref, lk2_ref, g_ref, q_ref, k_ref, v_ref, o_ref,
                     *scratch, lambda_init):
    lam = (jnp.exp(jnp.sum(lq1_ref[...] * lk1_ref[...], axis=-1, keepdims=True))
           - jnp.exp(jnp.sum(lq2_ref[...] * lk2_ref[...], axis=-1, keepdims=True))
           + lambda_init)
    gain = g_ref[...] * (1.0 - lambda_init)

    def make_lhs(q):
        lane = lax.broadcasted_iota(jnp.int32, q.shape, 1)
        zero = jnp.zeros_like(q)
        return jnp.concatenate([jnp.where(lane < DIFF_HEAD_DIM, q, zero),
                                jnp.where(lane >= DIFF_HEAD_DIM, q, zero)], axis=0)

    def finish(o, l):
        tq = o.shape[0] // 2
        out = o[:tq] / l[:tq] - lam * (o[tq:] / l[tq:])
        ms = jnp.mean(out * out, axis=-1, keepdims=True)
        return out * lax.rsqrt(ms + NORM_EPS) * gain

    _attn_step(q_ref, k_ref, v_ref, o_ref, scratch[:ATTN_TILES], scratch[ATTN_TILES:],
               make_lhs, finish)


def _diffattn_call(lams, g, qk, v, n_heads, lambda_init, tq=256):
    b, s, _ = qk.shape
    steps, q_spec, k_spec, v_spec, o_spec = _attn_specs(
        b, n_heads, s, tq, LANES, LANES, q_col=0, k_col=n_heads, v_col=0)
    vec = lambda a: pl.BlockSpec(a.shape, lambda g: (0, 0))
    return pl.pallas_call(
        functools.partial(_diffattn_kernel, lambda_init=lambda_init),
        out_shape=jax.ShapeDtypeStruct((b, s, n_heads * DIFF_V_DIM), BF16),
        grid=(steps,),
        in_specs=[vec(lams[0]), vec(lams[1]), vec(lams[2]), vec(lams[3]), vec(g),
                  q_spec, k_spec, v_spec],
        out_specs=o_spec,
        scratch_shapes=_attn_scratch(2 * tq, s),
        compiler_params=_cparams(("arbitrary",)),
        name="diffattn",
    )(*lams, g, qk, qk, v)


def _mlaattn_kernel(q_ref, k_ref, v_ref, o_ref, *scratch):
    _attn_step(q_ref, k_ref, v_ref, o_ref, scratch[:ATTN_TILES], scratch[ATTN_TILES:],
               lambda q: q, lambda o, l: o / l)


def _mlaattn_call(qm, km, vm, n_heads, tq=512):
    b, s, _ = qm.shape
    steps, q_spec, k_spec, v_spec, o_spec = _attn_specs(b, n_heads, s, tq, MLA_QK_PAD, MLA_V_DIM)
    return pl.pallas_call(
        _mlaattn_kernel,
        out_shape=jax.ShapeDtypeStruct(vm.shape, BF16),
        grid=(steps,),
        in_specs=[q_spec, k_spec, v_spec],
        out_specs=o_spec,
        scratch_shapes=_attn_scratch(tq, s),
        compiler_params=_cparams(("arbitrary",)),
        name="mlaattn",
    )(qm, km, vm)


def _outproj_kernel(od_ref, om_ref, w_ref, x_ref, mod_ref, o_ref, wb_ref):
    @pl.when(pl.program_id(0) == 0)
    def _():
        wb_ref[...] = w_ref[...].astype(BF16)

    kd = od_ref.shape[1]
    acc = _dot(od_ref[...], wb_ref[:kd, :])
    acc += _dot(om_ref[...], wb_ref[kd:, :])
    o_ref[...] = x_ref[...] + mod_ref[0, 2:3, :] * acc


def _outproj_call(od, om, w_out, x2, mod3, seq, tm=512):
    t, d = x2.shape
    tpb = seq // tm
    assert seq % tm == 0 and w_out.shape[0] == od.shape[1] + om.shape[1]
    return pl.pallas_call(
        _outproj_kernel,
        out_shape=jax.ShapeDtypeStruct((t, d), F32),
        grid=(t // tm,),
        in_specs=[pl.BlockSpec((tm, od.shape[1]), lambda i: (i, 0)),
                  pl.BlockSpec((tm, om.shape[1]), lambda i: (i, 0)),
                  pl.BlockSpec(w_out.shape, lambda i: (0, 0), pipeline_mode=pl.Buffered(1)),
                  pl.BlockSpec((tm, d), lambda i: (i, 0)),
                  pl.BlockSpec((1, N_MOD, d), lambda i: (i // tpb, 0, 0))],
        out_specs=pl.BlockSpec((tm, d), lambda i: (i, 0)),
        scratch_shapes=[pltpu.VMEM(w_out.shape, BF16)],
        compiler_params=_cparams(("arbitrary",)),
        name="outproj",
    )(od, om, w_out, x2, mod3)


def _ffn_kernel(x_ref, mod_ref, g_ref, w1a_ref, w1b_ref, w2p_ref, w2a_ref, gf_ref, o_ref,
                h_ref, ra_ref, rb_ref, *, rows):
    j = pl.program_id(1)
    nj = pl.num_programs(1) - 1
    act = lambda u: jnp.square(jnp.maximum(u, 0.0)).astype(BF16)

    @pl.when(j == 0)
    def _():
        _norm_modulate_rows(x_ref, g_ref, mod_ref, 3, 4, h_ref, rows)
        ra_ref[...] = act(_dot(h_ref[...], w1a_ref[...].astype(BF16)))
        u1 = _dot(h_ref[...], w1b_ref[...].astype(BF16))
        o_ref[...] = _dot(ra_ref[...], w2a_ref[...].astype(BF16))
        rb_ref[...] = act(u1)

    @pl.when((j > 0) & (j < nj))
    def _():
        u0 = _dot(h_ref[...], w1a_ref[...].astype(BF16))
        o_ref[...] += _dot(rb_ref[...], w2p_ref[...].astype(BF16))
        ra_ref[...] = act(u0)
        u1 = _dot(h_ref[...], w1b_ref[...].astype(BF16))
        o_ref[...] += _dot(ra_ref[...], w2a_ref[...].astype(BF16))
        rb_ref[...] = act(u1)

    @pl.when(j == nj)
    def _():
        o_ref[...] += _dot(rb_ref[...], w2p_ref[...].astype(BF16))
        d = x_ref.shape[1]
        gate = mod_ref[0, 5:6, :]
        gf = gf_ref[...]
        resid = lambda sl: x_ref[sl, :] + gate * o_ref[sl, :]

        def sumsq(sl):
            y = resid(sl)
            return jnp.sum(y * y, axis=-1, keepdims=True)

        def finish(sl, ss):
            o_ref[sl, :] = resid(sl) * lax.rsqrt(ss * (1.0 / d) + NORM_EPS) * gf

        _two_stage_rows(x_ref.shape[0], rows, sumsq, finish)


def _ffn_call(x1, mod3, g, w1, w2, gf, seq, tm=1024, tf=256):
    t, d = x1.shape
    n_chunks = w1.shape[1] // tf
    nj = n_chunks // 2
    tpb = seq // tm
    last_a, last_b = n_chunks - 2, n_chunks - 1
    return pl.pallas_call(
        functools.partial(_ffn_kernel, rows=64),
        out_shape=jax.ShapeDtypeStruct((t, d), F32),
        grid=(t // tm, nj + 1),
        in_specs=[pl.BlockSpec((tm, d), lambda i, j: (i, 0)),
                  pl.BlockSpec((1, N_MOD, d), lambda i, j: (i // tpb, 0, 0)),
                  pl.BlockSpec((1, d), lambda i, j: (0, 0)),
                  pl.BlockSpec((d, tf), lambda i, j: (0, jnp.minimum(2 * j, last_a))),
                  pl.BlockSpec((d, tf), lambda i, j: (0, jnp.minimum(2 * j + 1, last_b))),
                  pl.BlockSpec((tf, d), lambda i, j: (jnp.maximum(2 * j - 1, 0), 0)),
                  pl.BlockSpec((tf, d), lambda i, j: (jnp.minimum(2 * j, last_a), 0)),
                  pl.BlockSpec((1, d), lambda i, j: (0, 0))],
        out_specs=pl.BlockSpec((tm, d), lambda i, j: (i, 0)),
        scratch_shapes=[pltpu.VMEM((tm, d), BF16),
                        pltpu.VMEM((tm, tf), BF16),
                        pltpu.VMEM((tm, tf), BF16)],
        compiler_params=_cparams(("arbitrary", "arbitrary")),
        name="ffn",
    )(x1, mod3, g, w1, w1, w2, w2, gf)


def _rope_tables(positions, specs):
    inv = jnp.concatenate([1.0 / (theta ** (jnp.arange(0, rot, 2, dtype=F32) / rot))
                           for rot, theta, _ in specs])
    ang = positions.astype(F32).reshape(-1, 1) * inv[None, :]
    cos_all, sin_all = jnp.cos(ang), jnp.sin(ang)
    t = ang.shape[0]
    out, start = [], 0
    for rot, _, group in specs:
        cos, sin = cos_all[:, start:start + rot // 2], sin_all[:, start:start + rot // 2]
        start += rot // 2
        widen = lambda lo, hi, fill: jnp.concatenate(
            [lo, hi, jnp.full((t, group - rot), fill, F32)], axis=1)
        zeros = jnp.zeros_like(sin)
        out.append((widen(cos, cos, 1.0), widen(-sin, zeros, 0.0), widen(zeros, sin, 0.0)))
    return out


def kernel(x, c, positions, w_ada, b_ada, g_norm_mix, w_in, lambda_q1, lambda_k1, lambda_q2,
           lambda_k2, g_diff_sub, g_q_a, w_q_b, g_kv_a, w_kv_b, w_out, g_norm_ffn, w_ff1, w_ff2,
           g_final):
    b, s, d = x.shape
    depth = w_ada.shape[0]
    t = b * s
    q_rank = g_q_a.shape[1]
    kv_rank = g_kv_a.shape[1]
    n_mla = w_kv_b.shape[2] // (MLA_NOPE_DIM + MLA_V_DIM)
    n_diff = (w_in.shape[2] - q_rank - kv_rank - MLA_ROPE_DIM) // (3 * DIFF_V_DIM)
    assert n_diff == n_mla

    tabs_d, tabs_m = _rope_tables(positions, ((DIFF_ROT_DIM, ROPE_THETA, DIFF_HEAD_DIM),
                                             (MLA_ROPE_DIM, MLA_ROPE_THETA, LANES)))
    c_pad = jnp.pad(c, ((0, 8 - b), (0, 0)))
    x2 = x.reshape(t, d)

    for l in range(depth):
        lambda_init = 0.8 - 0.6 * float(np.exp(-0.3 * l))
        mod = _mod_call(c_pad, w_ada[l], b_ada[l][None, :])
        mod3 = mod[:b].reshape(b, N_MOD, d)

        w_t = jnp.swapaxes(w_in[l], 0, 1).astype(BF16)
        dqk, dv, lat = _inproj_call(x2, mod3, g_norm_mix[l][None, :], w_t, tabs_d,
                                    n_diff * DIFF_V_DIM, q_rank, kv_rank, s)

        wq_pad = jnp.pad(w_q_b[l].reshape(q_rank, n_mla, MLA_NOPE_DIM + MLA_ROPE_DIM),
                         ((0, 0), (0, 0), (0, MLA_QK_PAD - MLA_NOPE_DIM - MLA_ROPE_DIM))
                         ).reshape(q_rank, n_mla * MLA_QK_PAD).astype(BF16)
        qm, km, vm = _latent_call(lat, tabs_m, g_q_a[l][None, :], g_kv_a[l][None, :], wq_pad,
                                  w_kv_b[l].astype(BF16), n_mla)

        shp = lambda a: a.reshape(b, s, a.shape[1])
        lams = tuple(v[l][None, :] for v in (lambda_q1, lambda_k1, lambda_q2, lambda_k2))
        o_diff = _diffattn_call(lams, g_diff_sub[l][None, :], shp(dqk), shp(dv), n_diff,
                                lambda_init)
        o_mla = _mlaattn_call(shp(qm), shp(km), shp(vm), n_mla)

        x2 = _outproj_call(o_diff.reshape(t, -1), o_mla.reshape(t, -1), w_out[l], x2, mod3, s)

        last = l == depth - 1
        assert last, "final rmsnorm is fused into the last layer's MLP kernel"
        x2 = _ffn_call(x2, mod3, g_norm_ffn[l][None, :], w_ff1[l], w_ff2[l], g_final[None, :], s)

    return x2.reshape(b, s, d)
```

```python
import functools
import math

import jax
import jax.numpy as jnp
import numpy as np
from jax import lax
from jax.experimental import pallas as pl
from jax.experimental.pallas import tpu as pltpu

F32 = jnp.float32
BF16 = jnp.bfloat16

LANES = 128
NORM_EPS = 1e-6

DIFF_HEAD_DIM = 64
DIFF_V_DIM = 2 * DIFF_HEAD_DIM
DIFF_ROT_DIM = DIFF_HEAD_DIM // 4
ROPE_THETA = 500000.0
MLA_V_DIM = 128
MLA_NOPE_DIM = 128
MLA_ROPE_DIM = 64
MLA_ROPE_THETA = 10000.0
MLA_QK_PAD = 256

DIFF_SCALE = DIFF_HEAD_DIM ** -0.5 * math.log2(math.e)
MLA_SCALE = (MLA_NOPE_DIM + MLA_ROPE_DIM) ** -0.5 * math.log2(math.e)

MIB = 1024 * 1024
VMEM_LIMIT = 56 * MIB


def _cparams(sem, vmem_limit=VMEM_LIMIT):
    return pltpu.CompilerParams(dimension_semantics=sem, vmem_limit_bytes=vmem_limit)


def _dot(a, b):
    return jnp.dot(a, b, preferred_element_type=F32)


def _dot_nt(a, b):
    return lax.dot_general(a, b, (((1,), (1,)), ((), ())), preferred_element_type=F32)


MOD_MIX_ROWS = 2
MOD_REST_ROWS = 4


def _mod_block(c_ref, w_ref, b_ref):
    c = c_ref[...]
    sc = c / (1.0 + jnp.exp(-c))
    return _dot(sc.astype(BF16), w_ref[...].astype(BF16)) + b_ref[...]


def _mod_kernel(c_ref, w_ref, b_ref, o_ref):
    o_ref[...] = _mod_block(c_ref, w_ref, b_ref)


def _mod_call(c_pad, w_ada, b_ada, n, tn=1024):
    m, d = c_pad.shape
    return pl.pallas_call(
        _mod_kernel,
        out_shape=jax.ShapeDtypeStruct((m, n), F32),
        grid=(n // tn,),
        in_specs=[pl.BlockSpec((m, d), lambda j: (0, 0)),
                  pl.BlockSpec((d, tn), lambda j: (0, j)),
                  pl.BlockSpec((1, tn), lambda j: (0, j))],
        out_specs=pl.BlockSpec((m, tn), lambda j: (0, j)),
        compiler_params=_cparams(("arbitrary",)),
        name="mod",
    )(c_pad, w_ada, b_ada)


def _two_stage_rows(n_rows, rows, sumsq, finish):
    chunk = lambda r: pl.ds(pl.multiple_of(r * rows, rows), rows)

    def body(r, ss_prev):
        ss = sumsq(chunk(r))
        finish(chunk(r - 1), ss_prev)
        return ss

    n = n_rows // rows
    ss_last = lax.fori_loop(1, n, body, sumsq(chunk(0)))
    finish(chunk(n - 1), ss_last)


def _norm_modulate_rows(x_ref, g_ref, mod_ref, shift_row, scale_row, h_ref, rows):
    tm, d = x_ref.shape
    shift = mod_ref[0, shift_row:shift_row + 1, :]
    gs = g_ref[...] * (1.0 + mod_ref[0, scale_row:scale_row + 1, :])

    def sumsq(sl):
        x = x_ref[sl, :]
        return jnp.sum(x * x, axis=-1, keepdims=True)

    def finish(sl, ss):
        rs = lax.rsqrt(ss * (1.0 / d) + NORM_EPS)
        h_ref[sl, :] = (x_ref[sl, :] * rs * gs + shift).astype(BF16)

    _two_stage_rows(tm, rows, sumsq, finish)


def _rope_slab(x, c, s_lo, s_hi, half):
    return x * c + pltpu.roll(x, half, 1) * s_hi + pltpu.roll(x, LANES - half, 1) * s_lo


def _rms_rows(x, g):
    ms = jnp.mean(x * x, axis=-1, keepdims=True)
    return x * lax.rsqrt(ms + NORM_EPS) * g


def _inproj_kernel(x_ref, mod_ref, g_ref, w_ref, cd_ref, slo_ref, shi_ref,
                   qk_ref, v_ref, lat_ref, h_ref, r_ref, *, rows, lat_cols):
    j = pl.program_id(1)
    tn = r_ref.shape[1]

    def project():
        return _dot_nt(h_ref[...], w_ref[...].astype(BF16))

    @pl.when(j == 0)
    def _():
        _norm_modulate_rows(x_ref, g_ref, mod_ref, 0, 1, h_ref, rows)
        r_ref[...] = project()

    @pl.when((j >= 1) & (j <= 4))
    def _():
        repeat = lambda ref: jnp.concatenate([ref[...]] * (LANES // ref.shape[1]), axis=1)
        c, s_lo, s_hi = repeat(cd_ref), repeat(slo_ref), repeat(shi_ref)
        scale = jnp.where(j <= 2, DIFF_SCALE, 1.0)
        for k in range(tn // LANES):
            sl = slice(k * LANES, (k + 1) * LANES)
            qk_ref[:, sl] = (_rope_slab(r_ref[:, sl], c, s_lo, s_hi, DIFF_ROT_DIM // 2)
                             * scale).astype(BF16)
        r_ref[...] = project()

    @pl.when(j == 5)
    def _():
        v_ref[:, :tn] = r_ref[...].astype(BF16)
        r_ref[...] = project()

    @pl.when(j == 6)
    def _():
        v_ref[:, tn:] = r_ref[...].astype(BF16)
        r_ref[...] = project()

    @pl.when(j == 7)
    def _():
        lat_ref[:, :tn] = r_ref[...]
        lat_ref[:, tn:tn + lat_cols] = project()[:, :lat_cols]
        pad = lat_ref.shape[1] - (tn + lat_cols)
        lat_ref[:, tn + lat_cols:] = jnp.zeros((lat_ref.shape[0], pad), F32)


def _inproj_call(x2, mod3, g, w_t, tabs_d, dw, q_rank, kv_rank, seq, tm=1024):
    t, d = x2.shape
    tpb = seq // tm
    lat_cols = kv_rank + MLA_ROPE_DIM
    lat_w = q_rank + kv_rank + LANES
    tn = dw // 2
    assert dw % (2 * LANES) == 0 and seq % tm == 0 and q_rank == tn and lat_cols <= tn
    assert w_t.shape[0] == 3 * dw + q_rank + lat_cols
    tab = pl.BlockSpec((tm, tabs_d[0].shape[1]), lambda i, j: (i, 0))
    return pl.pallas_call(
        functools.partial(_inproj_kernel, rows=64, lat_cols=lat_cols),
        out_shape=(jax.ShapeDtypeStruct((t, 2 * dw), BF16),
                   jax.ShapeDtypeStruct((t, dw), BF16),
                   jax.ShapeDtypeStruct((t, lat_w), F32)),
        grid=(t // tm, 8),
        in_specs=[pl.BlockSpec((tm, d), lambda i, j: (i, 0)),
                  pl.BlockSpec((1, mod3.shape[1], d), lambda i, j: (i // tpb, 0, 0)),
                  pl.BlockSpec((1, d), lambda i, j: (0, 0)),
                  pl.BlockSpec((tn, d), lambda i, j: (j, 0)),
                  tab, tab, tab],
        out_specs=(pl.BlockSpec((tm, tn), lambda i, j: (i, jnp.clip(j - 1, 0, 3))),
                   pl.BlockSpec((tm, dw), lambda i, j: (i, 0)),
                   pl.BlockSpec((tm, lat_w), lambda i, j: (i, 0))),
        scratch_shapes=[pltpu.VMEM((tm, d), BF16), pltpu.VMEM((tm, tn), F32)],
        compiler_params=_cparams(("arbitrary", "arbitrary"), 50 * MIB),
        name="inproj",
    )(x2, mod3, g, w_t, *tabs_d)


def _latent_kernel(lat_ref, cm_ref, smlo_ref, smhi_ref,
                   gq_ref, gkv_ref, wq_ref, wkv_ref, qmo_ref, kmo_ref, vmo_ref, *, n_heads):
    cm, smlo, smhi = cm_ref[...], smlo_ref[...], smhi_ref[...]
    half_m = MLA_ROPE_DIM // 2
    q_rank, kv_rank = gq_ref.shape[1], gkv_ref.shape[1]

    q = _dot(_rms_rows(lat_ref[:, :q_rank], gq_ref[...]).astype(BF16), wq_ref[...])
    kv = _dot(_rms_rows(lat_ref[:, q_rank:q_rank + kv_rank], gkv_ref[...]).astype(BF16),
              wkv_ref[...])
    kr = _rope_slab(lat_ref[:, q_rank + kv_rank:], cm, smlo, smhi, half_m).astype(BF16)

    for h in range(n_heads):
        base = h * MLA_QK_PAD
        nope = slice(base, base + LANES)
        rope = slice(base + LANES, base + 2 * LANES)
        qmo_ref[:, nope] = (q[:, nope] * MLA_SCALE).astype(BF16)
        qmo_ref[:, rope] = (_rope_slab(q[:, rope], cm, smlo, smhi, half_m) * MLA_SCALE).astype(BF16)
        kmo_ref[:, nope] = kv[:, nope].astype(BF16)
        kmo_ref[:, rope] = kr
        vmo_ref[:, h * LANES:(h + 1) * LANES] = kv[:, rope].astype(BF16)


def _latent_call(lat, tabs_m, gq, gkv, wq_pad, wkv, n_heads, tm=1024):
    t = lat.shape[0]
    q_rank = gq.shape[1]
    kv_rank = gkv.shape[1]
    qk_w = n_heads * MLA_QK_PAD
    v_w = n_heads * MLA_V_DIM
    assert q_rank % LANES == 0 and kv_rank % LANES == 0
    assert lat.shape[1] == q_rank + kv_rank + LANES
    row = lambda w, c: pl.BlockSpec((tm, w), lambda i, c=c: (i, c))
    full = lambda a: pl.BlockSpec(a.shape, lambda i: (0, 0))
    in_specs = [row(lat.shape[1], 0)]
    in_specs += [row(LANES, 0)] * 3
    in_specs += [full(gq), full(gkv), full(wq_pad), full(wkv)]
    out_shape = (jax.ShapeDtypeStruct((t, qk_w), BF16),
                 jax.ShapeDtypeStruct((t, qk_w), BF16),
                 jax.ShapeDtypeStruct((t, v_w), BF16))
    return pl.pallas_call(
        functools.partial(_latent_kernel, n_heads=n_heads),
        out_shape=out_shape,
        grid=(t // tm,),
        in_specs=in_specs,
        out_specs=(row(qk_w, 0), row(qk_w, 0), row(v_w, 0)),
        compiler_params=_cparams(("arbitrary",)),
        name="latent",
    )(lat, *tabs_m, gq, gkv, wq_pad, wkv)


def _numerators(s_ref, e_ref, rows=16):
    for r in range(0, s_ref.shape[0], rows):
        s = s_ref[r:r + rows, :]
        e_ref[r:r + rows, :] = jnp.exp2(s - jnp.max(s, axis=-1, keepdims=True)).astype(BF16)


def _weighted_values(e_ref, v1):
    o = _dot(e_ref[...], v1)
    dv = v1.shape[1] // 2
    return o[:, :dv], o[:, dv:dv + 1]


ATTN_TILES = 4
NUMER_LAG = 2


def _attn_step(q_ref, k_ref, v_ref, o_ref, s_refs, e_refs, make_lhs, finish):
    n = ATTN_TILES
    tq = o_ref.shape[1] // n

    @pl.when(pl.program_id(0) == 0)
    def _():
        for t in range(n - NUMER_LAG, n):
            s_refs[t][...] = jnp.zeros_like(s_refs[t])
        for t in range(n - NUMER_LAG):
            e_refs[t][...] = jnp.zeros_like(e_refs[t])

    v = v_ref[0]
    v1 = jnp.concatenate([v, jnp.ones_like(v)], axis=1)
    for k in range(n):
        rows = slice(k * tq, (k + 1) * tq)
        behind = (k - NUMER_LAG) % n
        s_refs[k][...] = _dot_nt(make_lhs(q_ref[0, rows]), k_ref[0])
        _numerators(s_refs[behind], e_refs[behind])
        o_ref[0, rows] = finish(*_weighted_values(e_refs[k], v1)).astype(o_ref.dtype)


def _attn_specs(b, n_heads, seq, tq, qk_width, v_width, q_col=0, k_col=0, v_col=0):
    rows = ATTN_TILES * tq
    n_groups = seq // rows
    total = b * n_heads * n_groups
    assert seq % rows == 0

    def decode(g):
        return g // (n_heads * n_groups), (g // n_groups) % n_heads, g % n_groups

    cur = lambda g: decode(jnp.minimum(g, total - 1))
    prev = lambda g: decode(jnp.maximum(g - 1, 0))
    q_spec = pl.BlockSpec((1, rows, qk_width),
                          lambda g: (cur(g)[0], cur(g)[2], q_col + cur(g)[1]))
    k_spec = pl.BlockSpec((1, seq, qk_width), lambda g: (cur(g)[0], 0, k_col + cur(g)[1]))
    v_spec = pl.BlockSpec((1, seq, v_width), lambda g: (prev(g)[0], 0, v_col + prev(g)[1]))
    o_spec = pl.BlockSpec((1, rows, v_width), lambda g: (prev(g)[0], prev(g)[2], prev(g)[1]))
    return total + 1, q_spec, k_spec, v_spec, o_spec


def _attn_scratch(rows, seq):
    return ([pltpu.VMEM((rows, seq), F32)] * ATTN_TILES
            + [pltpu.VMEM((rows, seq), BF16)] * ATTN_TILES)


def _diffattn_kernel(lq1_ref, lk1_ref, lq2_ref, lk2_ref, g_ref, c_ref, wa_ref, ba_ref,
                     q_ref, k_ref, v_ref, o_ref, mod_ref, *scratch, lambda_init):
    mod_ref[...] = _mod_block(c_ref, wa_ref, ba_ref)

    lam = (jnp.exp(jnp.sum(lq1_ref[...] * lk1_ref[...], axis=-1, keepdims=True))
           - jnp.exp(jnp.sum(lq2_ref[...] * lk2_ref[...], axis=-1, keepdims=True))
           + lambda_init)
    gain = g_ref[...] * (1.0 - lambda_init)

    def make_lhs(q):
        lane = lax.broadcasted_iota(jnp.int32, q.shape, 1)
        zero = jnp.zeros_like(q)
        return jnp.concatenate([jnp.where(lane < DIFF_HEAD_DIM, q, zero),
                                jnp.where(lane >= DIFF_HEAD_DIM, q, zero)], axis=0)

    def finish(o, l):
        tq = o.shape[0] // 2
        out = o[:tq] / l[:tq] - lam * (o[tq:] / l[tq:])
        ms = jnp.mean(out * out, axis=-1, keepdims=True)
        return out * lax.rsqrt(ms + NORM_EPS) * gain

    _attn_step(q_ref, k_ref, v_ref, o_ref, scratch[:ATTN_TILES], scratch[ATTN_TILES:],
               make_lhs, finish)


def _diffattn_call(lams, g, qk, v, c_pad, w_ada, b_ada, mod_col0, n_heads, lambda_init, tq=256):
    b, s, _ = qk.shape
    steps, q_spec, k_spec, v_spec, o_spec = _attn_specs(
        b, n_heads, s, tq, LANES, LANES, q_col=0, k_col=n_heads, v_col=0)
    d = c_pad.shape[1]
    n_mod = w_ada.shape[1] - mod_col0
    n_blk = n_mod // LANES
    assert n_mod % LANES == 0 and mod_col0 % LANES == 0 and n_blk <= steps
    blk = lambda g: jnp.minimum(g, n_blk - 1)
    vec = lambda a: pl.BlockSpec(a.shape, lambda g: (0, 0))
    return pl.pallas_call(
        functools.partial(_diffattn_kernel, lambda_init=lambda_init),
        out_shape=(jax.ShapeDtypeStruct((b, s, n_heads * DIFF_V_DIM), BF16),
                   jax.ShapeDtypeStruct((c_pad.shape[0], n_mod), F32)),
        grid=(steps,),
        in_specs=[vec(lams[0]), vec(lams[1]), vec(lams[2]), vec(lams[3]), vec(g), vec(c_pad),
                  pl.BlockSpec((d, LANES), lambda g: (0, mod_col0 // LANES + blk(g))),
                  pl.BlockSpec((1, LANES), lambda g: (0, mod_col0 // LANES + blk(g))),
                  q_spec, k_spec, v_spec],
        out_specs=(o_spec, pl.BlockSpec((c_pad.shape[0], LANES), lambda g: (0, blk(g)))),
        scratch_shapes=_attn_scratch(2 * tq, s),
        compiler_params=_cparams(("arbitrary",)),
        name="diffattn",
    )(*lams, g, c_pad, w_ada, b_ada, qk, qk, v)


def _mlaattn_kernel(q_ref, k_ref, v_ref, o_ref, *scratch):
    _attn_step(q_ref, k_ref, v_ref, o_ref, scratch[:ATTN_TILES], scratch[ATTN_TILES:],
               lambda q: q, lambda o, l: o / l)


def _mlaattn_call(qm, km, vm, n_heads, tq=512):
    b, s, _ = qm.shape
    steps, q_spec, k_spec, v_spec, o_spec = _attn_specs(b, n_heads, s, tq, MLA_QK_PAD, MLA_V_DIM)
    return pl.pallas_call(
        _mlaattn_kernel,
        out_shape=jax.ShapeDtypeStruct(vm.shape, BF16),
        grid=(steps,),
        in_specs=[q_spec, k_spec, v_spec],
        out_specs=o_spec,
        scratch_shapes=_attn_scratch(tq, s),
        compiler_params=_cparams(("arbitrary",)),
        name="mlaattn",
    )(qm, km, vm)


def _outproj_kernel(od_ref, om_ref, w_ref, x_ref, mod_ref, o_ref, wb_ref):
    @pl.when(pl.program_id(0) == 0)
    def _():
        wb_ref[...] = w_ref[...].astype(BF16)

    kd = od_ref.shape[1]
    acc = _dot(od_ref[...], wb_ref[:kd, :])
    acc += _dot(om_ref[...], wb_ref[kd:, :])
    o_ref[...] = x_ref[...] + mod_ref[0, 0:1, :] * acc


def _outproj_call(od, om, w_out, x2, mod3, seq, tm=512):
    t, d = x2.shape
    tpb = seq // tm
    assert seq % tm == 0 and w_out.shape[0] == od.shape[1] + om.shape[1]
    return pl.pallas_call(
        _outproj_kernel,
        out_shape=jax.ShapeDtypeStruct((t, d), F32),
        grid=(t // tm,),
        in_specs=[pl.BlockSpec((tm, od.shape[1]), lambda i: (i, 0)),
                  pl.BlockSpec((tm, om.shape[1]), lambda i: (i, 0)),
                  pl.BlockSpec(w_out.shape, lambda i: (0, 0), pipeline_mode=pl.Buffered(1)),
                  pl.BlockSpec((tm, d), lambda i: (i, 0)),
                  pl.BlockSpec((1, mod3.shape[1], d), lambda i: (i // tpb, 0, 0))],
        out_specs=pl.BlockSpec((tm, d), lambda i: (i, 0)),
        scratch_shapes=[pltpu.VMEM(w_out.shape, BF16)],
        compiler_params=_cparams(("arbitrary",)),
        name="outproj",
    )(od, om, w_out, x2, mod3)


def _ffn_kernel(x_ref, mod_ref, g_ref, w1a_ref, w1b_ref, w2p_ref, w2a_ref, gf_ref, o_ref,
                h_ref, ra_ref, rb_ref, *, rows):
    j = pl.program_id(1)
    nj = pl.num_programs(1) - 1
    act = lambda u: jnp.square(jnp.maximum(u, 0.0)).astype(BF16)

    @pl.when(j == 0)
    def _():
        _norm_modulate_rows(x_ref, g_ref, mod_ref, 1, 2, h_ref, rows)
        ra_ref[...] = act(_dot(h_ref[...], w1a_ref[...].astype(BF16)))
        u1 = _dot(h_ref[...], w1b_ref[...].astype(BF16))
        o_ref[...] = _dot(ra_ref[...], w2a_ref[...].astype(BF16))
        rb_ref[...] = act(u1)

    @pl.when((j > 0) & (j < nj))
    def _():
        u0 = _dot(h_ref[...], w1a_ref[...].astype(BF16))
        o_ref[...] += _dot(rb_ref[...], w2p_ref[...].astype(BF16))
        ra_ref[...] = act(u0)
        u1 = _dot(h_ref[...], w1b_ref[...].astype(BF16))
        o_ref[...] += _dot(ra_ref[...], w2a_ref[...].astype(BF16))
        rb_ref[...] = act(u1)

    @pl.when(j == nj)
    def _():
        o_ref[...] += _dot(rb_ref[...], w2p_ref[...].astype(BF16))
        d = x_ref.shape[1]
        gate = mod_ref[0, 3:4, :]
        gf = gf_ref[...]
        resid = lambda sl: x_ref[sl, :] + gate * o_ref[sl, :]

        def sumsq(sl):
            y = resid(sl)
            return jnp.sum(y * y, axis=-1, keepdims=True)

        def finish(sl, ss):
            o_ref[sl, :] = resid(sl) * lax.rsqrt(ss * (1.0 / d) + NORM_EPS) * gf

        _two_stage_rows(x_ref.shape[0], rows, sumsq, finish)


def _ffn_call(x1, mod3, g, w1, w2, gf, seq, tm=1024, tf=256):
    t, d = x1.shape
    n_chunks = w1.shape[1] // tf
    nj = n_chunks // 2
    tpb = seq // tm
    last_a, last_b = n_chunks - 2, n_chunks - 1
    return pl.pallas_call(
        functools.partial(_ffn_kernel, rows=64),
        out_shape=jax.ShapeDtypeStruct((t, d), F32),
        grid=(t // tm, nj + 1),
        in_specs=[pl.BlockSpec((tm, d), lambda i, j: (i, 0)),
                  pl.BlockSpec((1, mod3.shape[1], d), lambda i, j: (i // tpb, 0, 0)),
                  pl.BlockSpec((1, d), lambda i, j: (0, 0)),
                  pl.BlockSpec((d, tf), lambda i, j: (0, jnp.minimum(2 * j, last_a))),
                  pl.BlockSpec((d, tf), lambda i, j: (0, jnp.minimum(2 * j + 1, last_b))),
                  pl.BlockSpec((tf, d), lambda i, j: (jnp.maximum(2 * j - 1, 0), 0)),
                  pl.BlockSpec((tf, d), lambda i, j: (jnp.minimum(2 * j, last_a), 0)),
                  pl.BlockSpec((1, d), lambda i, j: (0, 0))],
        out_specs=pl.BlockSpec((tm, d), lambda i, j: (i, 0)),
        scratch_shapes=[pltpu.VMEM((tm, d), BF16),
                        pltpu.VMEM((tm, tf), BF16),
                        pltpu.VMEM((tm, tf), BF16)],
        compiler_params=_cparams(("arbitrary", "arbitrary")),
        name="ffn",
    )(x1, mod3, g, w1, w1, w2, w2, gf)


def _rope_tables(positions, specs):
    inv = jnp.concatenate([1.0 / (theta ** (jnp.arange(0, rot, 2, dtype=F32) / rot))
                           for rot, theta, _ in specs])
    ang = positions.astype(F32).reshape(-1, 1) * inv[None, :]
    cos_all, sin_all = jnp.cos(ang), jnp.sin(ang)
    t = ang.shape[0]
    out, start = [], 0
    for rot, _, group in specs:
        cos, sin = cos_all[:, start:start + rot // 2], sin_all[:, start:start + rot // 2]
        start += rot // 2
        widen = lambda lo, hi, fill: jnp.concatenate(
            [lo, hi, jnp.full((t, group - rot), fill, F32)], axis=1)
        zeros = jnp.zeros_like(sin)
        out.append((widen(cos, cos, 1.0), widen(-sin, zeros, 0.0), widen(zeros, sin, 0.0)))
    return out


def kernel(x, c, positions, w_ada, b_ada, g_norm_mix, w_in, lambda_q1, lambda_k1, lambda_q2,
           lambda_k2, g_diff_sub, g_q_a, w_q_b, g_kv_a, w_kv_b, w_out, g_norm_ffn, w_ff1, w_ff2,
           g_final):
    b, s, d = x.shape
    depth = w_ada.shape[0]
    t = b * s
    q_rank = g_q_a.shape[1]
    kv_rank = g_kv_a.shape[1]
    n_mla = w_kv_b.shape[2] // (MLA_NOPE_DIM + MLA_V_DIM)
    n_diff = (w_in.shape[2] - q_rank - kv_rank - MLA_ROPE_DIM) // (3 * DIFF_V_DIM)
    assert n_diff == n_mla

    tabs_d, tabs_m = _rope_tables(positions, ((DIFF_ROT_DIM, ROPE_THETA, DIFF_HEAD_DIM),
                                             (MLA_ROPE_DIM, MLA_ROPE_THETA, LANES)))
    c_pad = jnp.pad(c, ((0, 8 - b), (0, 0)))
    x2 = x.reshape(t, d)

    for l in range(depth):
        lambda_init = 0.8 - 0.6 * float(np.exp(-0.3 * l))
        assert w_ada.shape[2] == (MOD_MIX_ROWS + MOD_REST_ROWS) * d
        mod_mix = _mod_call(c_pad, w_ada[l], b_ada[l][None, :], MOD_MIX_ROWS * d)
        mod_mix = mod_mix[:b].reshape(b, MOD_MIX_ROWS, d)

        w_t = jnp.swapaxes(w_in[l], 0, 1)
        dqk, dv, lat = _inproj_call(x2, mod_mix, g_norm_mix[l][None, :], w_t, tabs_d,
                                    n_diff * DIFF_V_DIM, q_rank, kv_rank, s)

        wq_pad = jnp.pad(w_q_b[l].reshape(q_rank, n_mla, MLA_NOPE_DIM + MLA_ROPE_DIM),
                         ((0, 0), (0, 0), (0, MLA_QK_PAD - MLA_NOPE_DIM - MLA_ROPE_DIM))
                         ).reshape(q_rank, n_mla * MLA_QK_PAD).astype(BF16)
        qm, km, vm = _latent_call(lat, tabs_m, g_q_a[l][None, :], g_kv_a[l][None, :], wq_pad,
                                  w_kv_b[l].astype(BF16), n_mla)

        shp = lambda a: a.reshape(b, s, a.shape[1])
        lams = tuple(v[l][None, :] for v in (lambda_q1, lambda_k1, lambda_q2, lambda_k2))
        o_diff, mod_rest = _diffattn_call(lams, g_diff_sub[l][None, :], shp(dqk), shp(dv), c_pad,
                                          w_ada[l], b_ada[l][None, :], MOD_MIX_ROWS * d, n_diff,
                                          lambda_init)
        mod_rest = mod_rest[:b].reshape(b, MOD_REST_ROWS, d)
        o_mla = _mlaattn_call(shp(qm), shp(km), shp(vm), n_mla)

        x2 = _outproj_call(o_diff.reshape(t, -1), o_mla.reshape(t, -1), w_out[l], x2, mod_rest, s)

        last = l == depth - 1
        assert last, "final rmsnorm is fused into the last layer's MLP kernel"
        x2 = _ffn_call(x2, mod_rest, g_norm_ffn[l][None, :], w_ff1[l], w_ff2[l],
                       g_final[None, :], s)

    return x2.reshape(b, s, d)
```

```python
import functools
import math

import jax
import jax.numpy as jnp
import numpy as np
from jax import lax
from jax.experimental import pallas as pl
from jax.experimental.pallas import tpu as pltpu

F32 = jnp.float32
BF16 = jnp.bfloat16

LANES = 128
NORM_EPS = 1e-6

DIFF_HEAD_DIM = 64
DIFF_V_DIM = 2 * DIFF_HEAD_DIM
DIFF_ROT_DIM = DIFF_HEAD_DIM // 4
ROPE_THETA = 500000.0
MLA_V_DIM = 128
MLA_NOPE_DIM = 128
MLA_ROPE_DIM = 64
MLA_ROPE_THETA = 10000.0
MLA_QK_PAD = 256

DIFF_SCALE = DIFF_HEAD_DIM ** -0.5 * math.log2(math.e)
MLA_SCALE = (MLA_NOPE_DIM + MLA_ROPE_DIM) ** -0.5 * math.log2(math.e)

MIB = 1024 * 1024
VMEM_LIMIT = 56 * MIB


def _cparams(sem, vmem_limit=VMEM_LIMIT):
    return pltpu.CompilerParams(dimension_semantics=sem, vmem_limit_bytes=vmem_limit)


def _dot(a, b):
    return jnp.dot(a, b, preferred_element_type=F32)


def _dot_nt(a, b):
    return lax.dot_general(a, b, (((1,), (1,)), ((), ())), preferred_element_type=F32)


MOD_MIX_ROWS = 2
MOD_REST_ROWS = 4


def _mod_block(c_ref, w_ref, b_ref):
    c = c_ref[...]
    sc = c / (1.0 + jnp.exp(-c))
    return _dot(sc.astype(BF16), w_ref[...].astype(BF16)) + b_ref[...]


def _mod_kernel(c_ref, w_ref, b_ref, o_ref):
    o_ref[...] = _mod_block(c_ref, w_ref, b_ref)


def _mod_call(c_pad, w_ada, b_ada, n, tn=1024):
    m, d = c_pad.shape
    return pl.pallas_call(
        _mod_kernel,
        out_shape=jax.ShapeDtypeStruct((m, n), F32),
        grid=(n // tn,),
        in_specs=[pl.BlockSpec((m, d), lambda j: (0, 0)),
                  pl.BlockSpec((d, tn), lambda j: (0, j)),
                  pl.BlockSpec((1, tn), lambda j: (0, j))],
        out_specs=pl.BlockSpec((m, tn), lambda j: (0, j)),
        compiler_params=_cparams(("arbitrary",)),
        name="mod",
    )(c_pad, w_ada, b_ada)


def _two_stage_rows(n_rows, rows, sumsq, finish):
    chunk = lambda r: pl.ds(pl.multiple_of(r * rows, rows), rows)

    def body(r, ss_prev):
        ss = sumsq(chunk(r))
        finish(chunk(r - 1), ss_prev)
        return ss

    n = n_rows // rows
    ss_last = lax.fori_loop(1, n, body, sumsq(chunk(0)))
    finish(chunk(n - 1), ss_last)


def _norm_modulate_rows(x_ref, g_ref, mod_ref, shift_row, scale_row, h_ref, rows):
    tm, d = x_ref.shape
    shift = mod_ref[0, shift_row:shift_row + 1, :]
    gs = g_ref[...] * (1.0 + mod_ref[0, scale_row:scale_row + 1, :])

    def sumsq(sl):
        x = x_ref[sl, :]
        return jnp.sum(x * x, axis=-1, keepdims=True)

    def finish(sl, ss):
        rs = lax.rsqrt(ss * (1.0 / d) + NORM_EPS)
        h_ref[sl, :] = (x_ref[sl, :] * rs * gs + shift).astype(BF16)

    _two_stage_rows(tm, rows, sumsq, finish)


def _rope_slab(x, c, s, half, group):
    g = lax.broadcasted_iota(jnp.int32, (1, LANES), 1) % group
    partner = jnp.where(g >= half, pltpu.roll(x, half, 1), pltpu.roll(x, LANES - half, 1))
    return x * c + partner * s


def _rms_rows(x, g):
    ms = jnp.mean(x * x, axis=-1, keepdims=True)
    return x * lax.rsqrt(ms + NORM_EPS) * g


def _inproj_kernel(x_ref, mod_ref, g_ref, w_ref, cd_ref, sd_ref,
                   qk_ref, v_ref, lat_ref, h_ref, r_ref, *, rows, lat_cols):
    j = pl.program_id(1)
    tn = r_ref.shape[1]

    def project():
        return _dot_nt(h_ref[...], w_ref[...].astype(BF16))

    @pl.when(j == 0)
    def _():
        _norm_modulate_rows(x_ref, g_ref, mod_ref, 0, 1, h_ref, rows)
        r_ref[...] = project()

    @pl.when((j >= 1) & (j <= 4))
    def _():
        repeat = lambda ref: jnp.concatenate([ref[...]] * (LANES // ref.shape[1]), axis=1)
        c, s = repeat(cd_ref), repeat(sd_ref)
        scale = jnp.where(j <= 2, DIFF_SCALE, 1.0)
        for k in range(tn // LANES):
            sl = slice(k * LANES, (k + 1) * LANES)
            qk_ref[:, sl] = (_rope_slab(r_ref[:, sl], c, s, DIFF_ROT_DIM // 2, cd_ref.shape[1])
                             * scale).astype(BF16)
        r_ref[...] = project()

    @pl.when(j == 5)
    def _():
        v_ref[:, :tn] = r_ref[...].astype(BF16)
        r_ref[...] = project()

    @pl.when(j == 6)
    def _():
        v_ref[:, tn:] = r_ref[...].astype(BF16)
        r_ref[...] = project()

    @pl.when(j == 7)
    def _():
        lat_ref[:, :tn] = r_ref[...]
        lat_ref[:, tn:tn + lat_cols] = project()[:, :lat_cols]
        pad = lat_ref.shape[1] - (tn + lat_cols)
        lat_ref[:, tn + lat_cols:] = jnp.zeros((lat_ref.shape[0], pad), F32)


def _inproj_call(x2, mod3, g, w_t, tabs_d, dw, q_rank, kv_rank, seq, tm=1024):
    t, d = x2.shape
    tpb = seq // tm
    lat_cols = kv_rank + MLA_ROPE_DIM
    lat_w = q_rank + kv_rank + LANES
    tn = dw // 2
    assert dw % (2 * LANES) == 0 and seq % tm == 0 and q_rank == tn and lat_cols <= tn
    assert w_t.shape[0] == 3 * dw + q_rank + lat_cols
    tab = pl.BlockSpec((tm, tabs_d[0].shape[1]), lambda i, j: (i, 0))
    return pl.pallas_call(
        functools.partial(_inproj_kernel, rows=64, lat_cols=lat_cols),
        out_shape=(jax.ShapeDtypeStruct((t, 2 * dw), BF16),
                   jax.ShapeDtypeStruct((t, dw), BF16),
                   jax.ShapeDtypeStruct((t, lat_w), F32)),
        grid=(t // tm, 8),
        in_specs=[pl.BlockSpec((tm, d), lambda i, j: (i, 0)),
                  pl.BlockSpec((1, mod3.shape[1], d), lambda i, j: (i // tpb, 0, 0)),
                  pl.BlockSpec((1, d), lambda i, j: (0, 0)),
                  pl.BlockSpec((tn, d), lambda i, j: (j, 0)),
                  tab, tab],
        out_specs=(pl.BlockSpec((tm, tn), lambda i, j: (i, jnp.clip(j - 1, 0, 3))),
                   pl.BlockSpec((tm, dw), lambda i, j: (i, 0)),
                   pl.BlockSpec((tm, lat_w), lambda i, j: (i, 0))),
        scratch_shapes=[pltpu.VMEM((tm, d), BF16), pltpu.VMEM((tm, tn), F32)],
        compiler_params=_cparams(("arbitrary", "arbitrary"), 50 * MIB),
        name="inproj",
    )(x2, mod3, g, w_t, *tabs_d)


def _latent_kernel(lat_ref, cm_ref, sm_ref,
                   gq_ref, gkv_ref, wq_ref, wkv_ref, qmo_ref, kmo_ref, vmo_ref, *, n_heads):
    half_m = MLA_ROPE_DIM // 2
    rope_m = lambda x: _rope_slab(x, cm_ref[...], sm_ref[...], half_m, LANES)
    q_rank, kv_rank = gq_ref.shape[1], gkv_ref.shape[1]

    q = _dot(_rms_rows(lat_ref[:, :q_rank], gq_ref[...]).astype(BF16), wq_ref[...])
    kv = _dot(_rms_rows(lat_ref[:, q_rank:q_rank + kv_rank], gkv_ref[...]).astype(BF16),
              wkv_ref[...])
    kr = rope_m(lat_ref[:, q_rank + kv_rank:]).astype(BF16)

    for h in range(n_heads):
        base = h * MLA_QK_PAD
        nope = slice(base, base + LANES)
        rope = slice(base + LANES, base + 2 * LANES)
        qmo_ref[:, nope] = (q[:, nope] * MLA_SCALE).astype(BF16)
        qmo_ref[:, rope] = (rope_m(q[:, rope]) * MLA_SCALE).astype(BF16)
        kmo_ref[:, nope] = kv[:, nope].astype(BF16)
        kmo_ref[:, rope] = kr
        vmo_ref[:, h * LANES:(h + 1) * LANES] = kv[:, rope].astype(BF16)


def _latent_call(lat, tabs_m, gq, gkv, wq_pad, wkv, n_heads, tm=1024):
    t = lat.shape[0]
    q_rank = gq.shape[1]
    kv_rank = gkv.shape[1]
    qk_w = n_heads * MLA_QK_PAD
    v_w = n_heads * MLA_V_DIM
    assert q_rank % LANES == 0 and kv_rank % LANES == 0
    assert lat.shape[1] == q_rank + kv_rank + LANES
    row = lambda w, c: pl.BlockSpec((tm, w), lambda i, c=c: (i, c))
    full = lambda a: pl.BlockSpec(a.shape, lambda i: (0, 0))
    in_specs = [row(lat.shape[1], 0)]
    in_specs += [row(LANES, 0)] * len(tabs_m)
    in_specs += [full(gq), full(gkv), full(wq_pad), full(wkv)]
    out_shape = (jax.ShapeDtypeStruct((t, qk_w), BF16),
                 jax.ShapeDtypeStruct((t, qk_w), BF16),
                 jax.ShapeDtypeStruct((t, v_w), BF16))
    return pl.pallas_call(
        functools.partial(_latent_kernel, n_heads=n_heads),
        out_shape=out_shape,
        grid=(t // tm,),
        in_specs=in_specs,
        out_specs=(row(qk_w, 0), row(qk_w, 0), row(v_w, 0)),
        compiler_params=_cparams(("arbitrary",)),
        name="latent",
    )(lat, *tabs_m, gq, gkv, wq_pad, wkv)


def _numerators(s_ref, e_ref, rows=16):
    for r in range(0, s_ref.shape[0], rows):
        s = s_ref[r:r + rows, :]
        e_ref[r:r + rows, :] = jnp.exp2(s - jnp.max(s, axis=-1, keepdims=True)).astype(BF16)


def _weighted_values(e_ref, v1):
    o = _dot(e_ref[...], v1)
    dv = v1.shape[1] // 2
    return o[:, :dv], o[:, dv:dv + 1]


SCORE_BUFS = 4
NUMER_LAG = 2


def _attn_step(q_ref, k_ref, v_ref, o_ref, s_refs, e_refs, make_lhs, finish):
    n, n_s = len(e_refs), len(s_refs)
    tq = o_ref.shape[1] // n
    assert n_s >= NUMER_LAG + 2 and n % n_s == 0

    @pl.when(pl.program_id(0) == 0)
    def _():
        for t in range(NUMER_LAG):
            s_refs[(t - NUMER_LAG) % n_s][...] = jnp.zeros_like(s_refs[0])
        for t in range(n - NUMER_LAG):
            e_refs[t][...] = jnp.zeros_like(e_refs[t])

    v = v_ref[0]
    v1 = jnp.concatenate([v, jnp.ones_like(v)], axis=1)
    for k in range(n):
        rows = slice(k * tq, (k + 1) * tq)
        behind = k - NUMER_LAG
        s_refs[k % n_s][...] = _dot_nt(make_lhs(q_ref[0, rows]), k_ref[0])
        _numerators(s_refs[behind % n_s], e_refs[behind % n])
        o_ref[0, rows] = finish(*_weighted_values(e_refs[k], v1)).astype(o_ref.dtype)


def _attn_specs(b, n_heads, seq, tq, n_tiles, qk_width, v_width, q_col=0, k_col=0, v_col=0):
    rows = n_tiles * tq
    n_groups = seq // rows
    total = b * n_heads * n_groups
    assert seq % rows == 0

    def decode(g):
        return g // (n_heads * n_groups), (g // n_groups) % n_heads, g % n_groups

    cur = lambda g: decode(jnp.minimum(g, total - 1))
    prev = lambda g: decode(jnp.maximum(g - 1, 0))
    q_spec = pl.BlockSpec((1, rows, qk_width),
                          lambda g: (cur(g)[0], cur(g)[2], q_col + cur(g)[1]))
    k_spec = pl.BlockSpec((1, seq, qk_width), lambda g: (cur(g)[0], 0, k_col + cur(g)[1]))
    v_spec = pl.BlockSpec((1, seq, v_width), lambda g: (prev(g)[0], 0, v_col + prev(g)[1]))
    o_spec = pl.BlockSpec((1, rows, v_width), lambda g: (prev(g)[0], prev(g)[2], prev(g)[1]))
    return total + 1, q_spec, k_spec, v_spec, o_spec


def _attn_scratch(rows, seq, n_tiles):
    return ([pltpu.VMEM((rows, seq), F32)] * SCORE_BUFS
            + [pltpu.VMEM((rows, seq), BF16)] * n_tiles)


def _diffattn_kernel(lq1_ref, lk1_ref, lq2_ref, lk2_ref, g_ref, c_ref, wa_ref, ba_ref,
                     q_ref, k_ref, v_ref, o_ref, mod_ref, *scratch, lambda_init):
    mod_ref[...] = _mod_block(c_ref, wa_ref, ba_ref)

    lam = (jnp.exp(jnp.sum(lq1_ref[...] * lk1_ref[...], axis=-1, keepdims=True))
           - jnp.exp(jnp.sum(lq2_ref[...] * lk2_ref[...], axis=-1, keepdims=True))
           + lambda_init)
    gain = g_ref[...] * (1.0 - lambda_init)

    def make_lhs(q):
        lane = lax.broadcasted_iota(jnp.int32, q.shape, 1)
        zero = jnp.zeros_like(q)
        return jnp.concatenate([jnp.where(lane < DIFF_HEAD_DIM, q, zero),
                                jnp.where(lane >= DIFF_HEAD_DIM, q, zero)], axis=0)

    def finish(o, l):
        tq = o.shape[0] // 2
        out = o[:tq] / l[:tq] - lam * (o[tq:] / l[tq:])
        ms = jnp.mean(out * out, axis=-1, keepdims=True)
        return out * lax.rsqrt(ms + NORM_EPS) * gain

    _attn_step(q_ref, k_ref, v_ref, o_ref, scratch[:SCORE_BUFS], scratch[SCORE_BUFS:],
               make_lhs, finish)


def _diffattn_call(lams, g, qk, v, c_pad, w_ada, b_ada, mod_col0, n_heads, lambda_init,
                   tq=256, n_tiles=4):
    b, s, _ = qk.shape
    steps, q_spec, k_spec, v_spec, o_spec = _attn_specs(
        b, n_heads, s, tq, n_tiles, LANES, LANES, q_col=0, k_col=n_heads, v_col=0)
    d = c_pad.shape[1]
    n_mod = w_ada.shape[1] - mod_col0
    mw = LANES * pl.cdiv(n_mod // LANES, steps - 1)
    n_blk = n_mod // mw
    assert n_mod % mw == 0 and mod_col0 % mw == 0 and n_blk <= steps
    blk = lambda g: jnp.minimum(g, n_blk - 1)
    vec = lambda a: pl.BlockSpec(a.shape, lambda g: (0, 0))
    return pl.pallas_call(
        functools.partial(_diffattn_kernel, lambda_init=lambda_init),
        out_shape=(jax.ShapeDtypeStruct((b, s, n_heads * DIFF_V_DIM), BF16),
                   jax.ShapeDtypeStruct((c_pad.shape[0], n_mod), F32)),
        grid=(steps,),
        in_specs=[vec(lams[0]), vec(lams[1]), vec(lams[2]), vec(lams[3]), vec(g), vec(c_pad),
                  pl.BlockSpec((d, mw), lambda g: (0, mod_col0 // mw + blk(g))),
                  pl.BlockSpec((1, mw), lambda g: (0, mod_col0 // mw + blk(g))),
                  q_spec, k_spec, v_spec],
        out_specs=(o_spec, pl.BlockSpec((c_pad.shape[0], mw), lambda g: (0, blk(g)))),
        scratch_shapes=_attn_scratch(2 * tq, s, n_tiles),
        compiler_params=_cparams(("arbitrary",)),
        name="diffattn",
    )(*lams, g, c_pad, w_ada, b_ada, qk, qk, v)


def _mlaattn_kernel(q_ref, k_ref, v_ref, o_ref, *scratch):
    _attn_step(q_ref, k_ref, v_ref, o_ref, scratch[:SCORE_BUFS], scratch[SCORE_BUFS:],
               lambda q: q, lambda o, l: o / l)


def _mlaattn_call(qm, km, vm, n_heads, tq=512, n_tiles=4):
    b, s, _ = qm.shape
    steps, q_spec, k_spec, v_spec, o_spec = _attn_specs(b, n_heads, s, tq, n_tiles,
                                                        MLA_QK_PAD, MLA_V_DIM)
    return pl.pallas_call(
        _mlaattn_kernel,
        out_shape=jax.ShapeDtypeStruct(vm.shape, BF16),
        grid=(steps,),
        in_specs=[q_spec, k_spec, v_spec],
        out_specs=o_spec,
        scratch_shapes=_attn_scratch(tq, s, n_tiles),
        compiler_params=_cparams(("arbitrary",)),
        name="mlaattn",
    )(qm, km, vm)


def _outproj_kernel(od_ref, om_ref, w_ref, x_ref, mod_ref, o_ref, wb_ref):
    @pl.when(pl.program_id(0) == 0)
    def _():
        wb_ref[...] = w_ref[...].astype(BF16)

    kd = od_ref.shape[1]
    acc = _dot(od_ref[...], wb_ref[:kd, :])
    acc += _dot(om_ref[...], wb_ref[kd:, :])
    o_ref[...] = x_ref[...] + mod_ref[0, 0:1, :] * acc


def _outproj_call(od, om, w_out, x2, mod3, seq, tm=512):
    t, d = x2.shape
    tpb = seq // tm
    assert seq % tm == 0 and w_out.shape[0] == od.shape[1] + om.shape[1]
    return pl.pallas_call(
        _outproj_kernel,
        out_shape=jax.ShapeDtypeStruct((t, d), F32),
        grid=(t // tm,),
        in_specs=[pl.BlockSpec((tm, od.shape[1]), lambda i: (i, 0)),
                  pl.BlockSpec((tm, om.shape[1]), lambda i: (i, 0)),
                  pl.BlockSpec(w_out.shape, lambda i: (0, 0), pipeline_mode=pl.Buffered(1)),
                  pl.BlockSpec((tm, d), lambda i: (i, 0)),
                  pl.BlockSpec((1, mod3.shape[1], d), lambda i: (i // tpb, 0, 0))],
        out_specs=pl.BlockSpec((tm, d), lambda i: (i, 0)),
        scratch_shapes=[pltpu.VMEM(w_out.shape, BF16)],
        compiler_params=_cparams(("arbitrary",)),
        name="outproj",
    )(od, om, w_out, x2, mod3)


def _ffn_kernel(x_ref, mod_ref, g_ref, w1a_ref, w1b_ref, w2p_ref, w2a_ref, gf_ref, o_ref,
                h_ref, ra_ref, rb_ref, *, rows):
    j = pl.program_id(1)
    nj = pl.num_programs(1) - 1
    act = lambda u: jnp.square(jnp.maximum(u, 0.0)).astype(BF16)

    @pl.when(j == 0)
    def _():
        _norm_modulate_rows(x_ref, g_ref, mod_ref, 1, 2, h_ref, rows)
        ra_ref[...] = act(_dot(h_ref[...], w1a_ref[...].astype(BF16)))
        u1 = _dot(h_ref[...], w1b_ref[...].astype(BF16))
        o_ref[...] = _dot(ra_ref[...], w2a_ref[...].astype(BF16))
        rb_ref[...] = act(u1)

    @pl.when((j > 0) & (j < nj))
    def _():
        u0 = _dot(h_ref[...], w1a_ref[...].astype(BF16))
        o_ref[...] += _dot(rb_ref[...], w2p_ref[...].astype(BF16))
        ra_ref[...] = act(u0)
        u1 = _dot(h_ref[...], w1b_ref[...].astype(BF16))
        o_ref[...] += _dot(ra_ref[...], w2a_ref[...].astype(BF16))
        rb_ref[...] = act(u1)

    @pl.when(j == nj)
    def _():
        o_ref[...] += _dot(rb_ref[...], w2p_ref[...].astype(BF16))
        d = x_ref.shape[1]
        gate = mod_ref[0, 3:4, :]
        gf = gf_ref[...]

        def sumsq(sl):
            y = x_ref[sl, :] + gate * o_ref[sl, :]
            o_ref[sl, :] = y
            return jnp.sum(y * y, axis=-1, keepdims=True)

        def finish(sl, ss):
            o_ref[sl, :] = o_ref[sl, :] * lax.rsqrt(ss * (1.0 / d) + NORM_EPS) * gf

        _two_stage_rows(x_ref.shape[0], rows, sumsq, finish)


def _ffn_call(x1, mod3, g, w1, w2, gf, seq, tm=1024, tf=256):
    t, d = x1.shape
    n_chunks = w1.shape[1] // tf
    nj = n_chunks // 2
    tpb = seq // tm
    last_a, last_b = n_chunks - 2, n_chunks - 1
    return pl.pallas_call(
        functools.partial(_ffn_kernel, rows=64),
        out_shape=jax.ShapeDtypeStruct((t, d), F32),
        grid=(t // tm, nj + 1),
        in_specs=[pl.BlockSpec((tm, d), lambda i, j: (i, 0)),
                  pl.BlockSpec((1, mod3.shape[1], d), lambda i, j: (i // tpb, 0, 0)),
                  pl.BlockSpec((1, d), lambda i, j: (0, 0)),
                  pl.BlockSpec((d, tf), lambda i, j: (0, jnp.minimum(2 * j, last_a))),
                  pl.BlockSpec((d, tf), lambda i, j: (0, jnp.minimum(2 * j + 1, last_b))),
                  pl.BlockSpec((tf, d), lambda i, j: (jnp.maximum(2 * j - 1, 0), 0)),
                  pl.BlockSpec((tf, d), lambda i, j: (jnp.minimum(2 * j, last_a), 0)),
                  pl.BlockSpec((1, d), lambda i, j: (0, 0))],
        out_specs=pl.BlockSpec((tm, d), lambda i, j: (i, 0)),
        scratch_shapes=[pltpu.VMEM((tm, d), BF16),
                        pltpu.VMEM((tm, tf), BF16),
                        pltpu.VMEM((tm, tf), BF16)],
        compiler_params=_cparams(("arbitrary", "arbitrary")),
        name="ffn",
    )(x1, mod3, g, w1, w1, w2, w2, gf)


def _rope_tables(positions, specs):
    inv = jnp.concatenate([1.0 / (theta ** (jnp.arange(0, rot, 2, dtype=F32) / rot))
                           for rot, theta, _ in specs])
    ang = positions.astype(F32).reshape(-1, 1) * inv[None, :]
    cos_all, sin_all = jnp.cos(ang), jnp.sin(ang)
    t = ang.shape[0]
    out, start = [], 0
    for rot, _, group in specs:
        cos, sin = cos_all[:, start:start + rot // 2], sin_all[:, start:start + rot // 2]
        start += rot // 2
        widen = lambda lo, hi, fill: jnp.concatenate(
            [lo, hi, jnp.full((t, group - rot), fill, F32)], axis=1)
        out.append((widen(cos, cos, 1.0), widen(-sin, sin, 0.0)))
    return out


def kernel(x, c, positions, w_ada, b_ada, g_norm_mix, w_in, lambda_q1, lambda_k1, lambda_q2,
           lambda_k2, g_diff_sub, g_q_a, w_q_b, g_kv_a, w_kv_b, w_out, g_norm_ffn, w_ff1, w_ff2,
           g_final):
    b, s, d = x.shape
    depth = w_ada.shape[0]
    t = b * s
    q_rank = g_q_a.shape[1]
    kv_rank = g_kv_a.shape[1]
    n_mla = w_kv_b.shape[2] // (MLA_NOPE_DIM + MLA_V_DIM)
    n_diff = (w_in.shape[2] - q_rank - kv_rank - MLA_ROPE_DIM) // (3 * DIFF_V_DIM)
    assert n_diff == n_mla

    tabs_d, tabs_m = _rope_tables(positions, ((DIFF_ROT_DIM, ROPE_THETA, DIFF_HEAD_DIM),
                                             (MLA_ROPE_DIM, MLA_ROPE_THETA, LANES)))
    c_pad = jnp.pad(c, ((0, 8 - b), (0, 0)))
    x2 = x.reshape(t, d)

    for l in range(depth):
        lambda_init = 0.8 - 0.6 * float(np.exp(-0.3 * l))
        assert w_ada.shape[2] == (MOD_MIX_ROWS + MOD_REST_ROWS) * d
        mod_mix = _mod_call(c_pad, w_ada[l], b_ada[l][None, :], MOD_MIX_ROWS * d)
        mod_mix = mod_mix[:b].reshape(b, MOD_MIX_ROWS, d)

        w_t = jnp.swapaxes(w_in[l], 0, 1)
        dqk, dv, lat = _inproj_call(x2, mod_mix, g_norm_mix[l][None, :], w_t, tabs_d,
                                    n_diff * DIFF_V_DIM, q_rank, kv_rank, s)

        wq_pad = jnp.pad(w_q_b[l].reshape(q_rank, n_mla, MLA_NOPE_DIM + MLA_ROPE_DIM),
                         ((0, 0), (0, 0), (0, MLA_QK_PAD - MLA_NOPE_DIM - MLA_ROPE_DIM))
                         ).reshape(q_rank, n_mla * MLA_QK_PAD).astype(BF16)
        qm, km, vm = _latent_call(lat, tabs_m, g_q_a[l][None, :], g_kv_a[l][None, :], wq_pad,
                                  w_kv_b[l].astype(BF16), n_mla)

        shp = lambda a: a.reshape(b, s, a.shape[1])
        lams = tuple(v[l][None, :] for v in (lambda_q1, lambda_k1, lambda_q2, lambda_k2))
        o_diff, mod_rest = _diffattn_call(lams, g_diff_sub[l][None, :], shp(dqk), shp(dv), c_pad,
                                          w_ada[l], b_ada[l][None, :], MOD_MIX_ROWS * d, n_diff,
                                          lambda_init)
        mod_rest = mod_rest[:b].reshape(b, MOD_REST_ROWS, d)
        o_mla = _mlaattn_call(shp(qm), shp(km), shp(vm), n_mla)

        x2 = _outproj_call(o_diff.reshape(t, -1), o_mla.reshape(t, -1), w_out[l], x2, mod_rest, s)

        last = l == depth - 1
        assert last, "final rmsnorm is fused into the last layer's MLP kernel"
        x2 = _ffn_call(x2, mod_rest, g_norm_ffn[l][None, :], w_ff1[l], w_ff2[l],
                       g_final[None, :], s)

    return x2.reshape(b, s, d)
```

```python
import functools
import math

import jax
import jax.numpy as jnp
import numpy as np
from jax import lax
from jax.experimental import pallas as pl
from jax.experimental.pallas import tpu as pltpu

F32 = jnp.float32
BF16 = jnp.bfloat16

LANES = 128
NORM_EPS = 1e-6

DIFF_HEAD_DIM = 64
DIFF_V_DIM = 2 * DIFF_HEAD_DIM
DIFF_ROT_DIM = DIFF_HEAD_DIM // 4
ROPE_THETA = 500000.0
MLA_V_DIM = 128
MLA_NOPE_DIM = 128
MLA_ROPE_DIM = 64
MLA_ROPE_THETA = 10000.0
MLA_QK_PAD = 256

DIFF_SCALE = DIFF_HEAD_DIM ** -0.5 * math.log2(math.e)
MLA_SCALE = (MLA_NOPE_DIM + MLA_ROPE_DIM) ** -0.5 * math.log2(math.e)

MIB = 1024 * 1024
VMEM_LIMIT = 56 * MIB


def _cparams(sem, vmem_limit=VMEM_LIMIT):
    return pltpu.CompilerParams(dimension_semantics=sem, vmem_limit_bytes=vmem_limit)


def _dot(a, b):
    return jnp.dot(a, b, preferred_element_type=F32)


def _dot_nt(a, b):
    return lax.dot_general(a, b, (((1,), (1,)), ((), ())), preferred_element_type=F32)


MOD_MIX_ROWS = 2
MOD_REST_ROWS = 4


def _mod_block(c_ref, w_ref, b_ref):
    c = c_ref[...]
    sc = c / (1.0 + jnp.exp(-c))
    return _dot(sc.astype(BF16), w_ref[...].astype(BF16)) + b_ref[...]


def _mod_kernel(c_ref, w_ref, b_ref, o_ref):
    o_ref[...] = _mod_block(c_ref, w_ref, b_ref)


def _mod_call(c_pad, w_ada, b_ada, n, tn=1024):
    m, d = c_pad.shape
    return pl.pallas_call(
        _mod_kernel,
        out_shape=jax.ShapeDtypeStruct((m, n), F32),
        grid=(n // tn,),
        in_specs=[pl.BlockSpec((m, d), lambda j: (0, 0)),
                  pl.BlockSpec((d, tn), lambda j: (0, j)),
                  pl.BlockSpec((1, tn), lambda j: (0, j))],
        out_specs=pl.BlockSpec((m, tn), lambda j: (0, j)),
        compiler_params=_cparams(("arbitrary",)),
        name="mod",
    )(c_pad, w_ada, b_ada)


def _two_stage_rows(n_rows, rows, sumsq, finish):
    chunk = lambda r: pl.ds(pl.multiple_of(r * rows, rows), rows)

    def body(r, ss_prev):
        ss = sumsq(chunk(r))
        finish(chunk(r - 1), ss_prev)
        return ss

    n = n_rows // rows
    ss_last = lax.fori_loop(1, n, body, sumsq(chunk(0)))
    finish(chunk(n - 1), ss_last)


def _norm_modulate_rows(x_ref, g_ref, mod_ref, shift_row, scale_row, h_ref, rows):
    tm, d = x_ref.shape
    shift = mod_ref[0, shift_row:shift_row + 1, :]
    gs = g_ref[...] * (1.0 + mod_ref[0, scale_row:scale_row + 1, :])

    def sumsq(sl):
        x = x_ref[sl, :]
        return jnp.sum(x * x, axis=-1, keepdims=True)

    def finish(sl, ss):
        rs = lax.rsqrt(ss * (1.0 / d) + NORM_EPS)
        h_ref[sl, :] = (x_ref[sl, :] * rs * gs + shift).astype(BF16)

    _two_stage_rows(tm, rows, sumsq, finish)


def _rope_slab(x, c, s, half, group):
    g = lax.broadcasted_iota(jnp.int32, (1, LANES), 1) % group
    partner = jnp.where(g >= half, pltpu.roll(x, half, 1), pltpu.roll(x, LANES - half, 1))
    return x * c + partner * s


def _rms_rows(x, g):
    ms = jnp.mean(x * x, axis=-1, keepdims=True)
    return x * lax.rsqrt(ms + NORM_EPS) * g


def _inproj_kernel(x_ref, mod_ref, g_ref, w_ref, cd_ref, sd_ref,
                   qk_ref, v_ref, lat_ref, h_ref, r_ref, *, rows, lat_cols):
    j = pl.program_id(1)
    tn = r_ref.shape[1]

    def project():
        return _dot_nt(h_ref[...], w_ref[...].astype(BF16))

    @pl.when(j == 0)
    def _():
        _norm_modulate_rows(x_ref, g_ref, mod_ref, 0, 1, h_ref, rows)
        r_ref[...] = project()

    @pl.when((j >= 1) & (j <= 4))
    def _():
        repeat = lambda ref: jnp.concatenate([ref[...]] * (LANES // ref.shape[1]), axis=1)
        c, s = repeat(cd_ref), repeat(sd_ref)
        scale = jnp.where(j <= 2, DIFF_SCALE, 1.0)
        for k in range(tn // LANES):
            sl = slice(k * LANES, (k + 1) * LANES)
            qk_ref[:, sl] = (_rope_slab(r_ref[:, sl], c, s, DIFF_ROT_DIM // 2, cd_ref.shape[1])
                             * scale).astype(BF16)
        r_ref[...] = project()

    @pl.when(j == 5)
    def _():
        v_ref[:, :tn] = r_ref[...].astype(BF16)
        r_ref[...] = project()

    @pl.when(j == 6)
    def _():
        v_ref[:, tn:] = r_ref[...].astype(BF16)
        r_ref[...] = project()

    @pl.when(j == 7)
    def _():
        lat_ref[:, :tn] = r_ref[...]
        lat_ref[:, tn:tn + lat_cols] = project()[:, :lat_cols]
        pad = lat_ref.shape[1] - (tn + lat_cols)
        lat_ref[:, tn + lat_cols:] = jnp.zeros((lat_ref.shape[0], pad), F32)


def _inproj_call(x2, mod3, g, w_t, tabs_d, dw, q_rank, kv_rank, seq, tm=1024):
    t, d = x2.shape
    tpb = seq // tm
    lat_cols = kv_rank + MLA_ROPE_DIM
    lat_w = q_rank + kv_rank + LANES
    tn = dw // 2
    assert dw % (2 * LANES) == 0 and seq % tm == 0 and q_rank == tn and lat_cols <= tn
    assert w_t.shape[0] == 3 * dw + q_rank + lat_cols
    tab = pl.BlockSpec((tm, tabs_d[0].shape[1]), lambda i, j: (i, 0))
    return pl.pallas_call(
        functools.partial(_inproj_kernel, rows=64, lat_cols=lat_cols),
        out_shape=(jax.ShapeDtypeStruct((t, 2 * dw), BF16),
                   jax.ShapeDtypeStruct((t, dw), BF16),
                   jax.ShapeDtypeStruct((t, lat_w), F32)),
        grid=(t // tm, 8),
        in_specs=[pl.BlockSpec((tm, d), lambda i, j: (i, 0)),
                  pl.BlockSpec((1, mod3.shape[1], d), lambda i, j: (i // tpb, 0, 0)),
                  pl.BlockSpec((1, d), lambda i, j: (0, 0)),
                  pl.BlockSpec((tn, d), lambda i, j: (j, 0)),
                  tab, tab],
        out_specs=(pl.BlockSpec((tm, tn), lambda i, j: (i, jnp.clip(j - 1, 0, 3))),
                   pl.BlockSpec((tm, dw), lambda i, j: (i, 0)),
                   pl.BlockSpec((tm, lat_w), lambda i, j: (i, 0))),
        scratch_shapes=[pltpu.VMEM((tm, d), BF16), pltpu.VMEM((tm, tn), F32)],
        compiler_params=_cparams(("arbitrary", "arbitrary"), 50 * MIB),
        name="inproj",
    )(x2, mod3, g, w_t, *tabs_d)


def _latent_kernel(lat_ref, cm_ref, sm_ref,
                   gq_ref, gkv_ref, wq_ref, wkv_ref, qmo_ref, kno_ref, kro_ref, vmo_ref,
                   *, n_heads):
    half_m = MLA_ROPE_DIM // 2
    rope_m = lambda x: _rope_slab(x, cm_ref[...], sm_ref[...], half_m, LANES)
    q_rank, kv_rank = gq_ref.shape[1], gkv_ref.shape[1]

    q = _dot(_rms_rows(lat_ref[:, :q_rank], gq_ref[...]).astype(BF16), wq_ref[...])
    kv = _dot(_rms_rows(lat_ref[:, q_rank:q_rank + kv_rank], gkv_ref[...]).astype(BF16),
              wkv_ref[...])
    kro_ref[...] = rope_m(lat_ref[:, q_rank + kv_rank:]).astype(BF16)

    for h in range(n_heads):
        base = h * MLA_QK_PAD
        nope = slice(base, base + LANES)
        rope = slice(base + LANES, base + 2 * LANES)
        head = slice(h * LANES, (h + 1) * LANES)
        qmo_ref[:, nope] = (q[:, nope] * MLA_SCALE).astype(BF16)
        qmo_ref[:, rope] = (rope_m(q[:, rope]) * MLA_SCALE).astype(BF16)
        kno_ref[:, head] = kv[:, nope].astype(BF16)
        vmo_ref[:, head] = kv[:, rope].astype(BF16)


def _latent_call(lat, tabs_m, gq, gkv, wq_pad, wkv, n_heads, tm=1024):
    t = lat.shape[0]
    q_rank = gq.shape[1]
    kv_rank = gkv.shape[1]
    qk_w = n_heads * MLA_QK_PAD
    v_w = n_heads * MLA_V_DIM
    assert q_rank % LANES == 0 and kv_rank % LANES == 0
    assert lat.shape[1] == q_rank + kv_rank + LANES
    row = lambda w, c: pl.BlockSpec((tm, w), lambda i, c=c: (i, c))
    full = lambda a: pl.BlockSpec(a.shape, lambda i: (0, 0))
    in_specs = [row(lat.shape[1], 0)]
    in_specs += [row(LANES, 0)] * len(tabs_m)
    in_specs += [full(gq), full(gkv), full(wq_pad), full(wkv)]
    kn_w = n_heads * MLA_NOPE_DIM
    out_shape = (jax.ShapeDtypeStruct((t, qk_w), BF16),
                 jax.ShapeDtypeStruct((t, kn_w), BF16),
                 jax.ShapeDtypeStruct((t, LANES), BF16),
                 jax.ShapeDtypeStruct((t, v_w), BF16))
    return pl.pallas_call(
        functools.partial(_latent_kernel, n_heads=n_heads),
        out_shape=out_shape,
        grid=(t // tm,),
        in_specs=in_specs,
        out_specs=(row(qk_w, 0), row(kn_w, 0), row(LANES, 0), row(v_w, 0)),
        compiler_params=_cparams(("arbitrary",)),
        name="latent",
    )(lat, *tabs_m, gq, gkv, wq_pad, wkv)


def _numerators(s_ref, e_ref, rows=16):
    for r in range(0, s_ref.shape[0], rows):
        s = s_ref[r:r + rows, :]
        e_ref[r:r + rows, :] = jnp.exp2(s - jnp.max(s, axis=-1, keepdims=True)).astype(BF16)


def _weighted_values(e_ref, v1):
    o = _dot(e_ref[...], v1)
    dv = v1.shape[1] // 2
    return o[:, :dv], o[:, dv:dv + 1]


SCORE_BUFS = 4
NUMER_LAG = 2


def _attn_step(q_ref, keys, v_ref, o_ref, s_refs, e_refs, make_lhs, finish):
    n, n_s = len(e_refs), len(s_refs)
    tq = o_ref.shape[1] // n
    assert n_s >= NUMER_LAG + 2 and n % n_s == 0

    @pl.when(pl.program_id(0) == 0)
    def _():
        for t in range(NUMER_LAG):
            s_refs[(t - NUMER_LAG) % n_s][...] = jnp.zeros_like(s_refs[0])
        for t in range(n - NUMER_LAG):
            e_refs[t][...] = jnp.zeros_like(e_refs[t])

    v = v_ref[0]
    v1 = jnp.concatenate([v, jnp.ones_like(v)], axis=1)
    for k in range(n):
        rows = slice(k * tq, (k + 1) * tq)
        behind = k - NUMER_LAG
        s_refs[k % n_s][...] = _dot_nt(make_lhs(q_ref[0, rows]), keys())
        _numerators(s_refs[behind % n_s], e_refs[behind % n])
        o_ref[0, rows] = finish(*_weighted_values(e_refs[k], v1)).astype(o_ref.dtype)


def _attn_specs(b, n_heads, seq, tq, n_tiles, qk_width, v_width, q_col=0, k_col=0, v_col=0):
    rows = n_tiles * tq
    n_groups = seq // rows
    total = b * n_heads * n_groups
    assert seq % rows == 0

    def decode(g):
        return g // (n_heads * n_groups), (g // n_groups) % n_heads, g % n_groups

    cur = lambda g: decode(jnp.minimum(g, total - 1))
    prev = lambda g: decode(jnp.maximum(g - 1, 0))
    q_spec = pl.BlockSpec((1, rows, qk_width),
                          lambda g: (cur(g)[0], cur(g)[2], q_col + cur(g)[1]))
    k_spec = pl.BlockSpec((1, seq, qk_width), lambda g: (cur(g)[0], 0, k_col + cur(g)[1]))
    v_spec = pl.BlockSpec((1, seq, v_width), lambda g: (prev(g)[0], 0, v_col + prev(g)[1]))
    o_spec = pl.BlockSpec((1, rows, v_width), lambda g: (prev(g)[0], prev(g)[2], prev(g)[1]))
    return total + 1, q_spec, k_spec, v_spec, o_spec, cur


def _attn_scratch(rows, seq, n_tiles):
    return ([pltpu.VMEM((rows, seq), F32)] * SCORE_BUFS
            + [pltpu.VMEM((rows, seq), BF16)] * n_tiles)


def _diffattn_kernel(lq1_ref, lk1_ref, lq2_ref, lk2_ref, g_ref, c_ref, wa_ref, ba_ref,
                     q_ref, k_ref, v_ref, o_ref, mod_ref, *scratch, lambda_init):
    mod_ref[...] = _mod_block(c_ref, wa_ref, ba_ref)

    lam = (jnp.exp(jnp.sum(lq1_ref[...] * lk1_ref[...], axis=-1, keepdims=True))
           - jnp.exp(jnp.sum(lq2_ref[...] * lk2_ref[...], axis=-1, keepdims=True))
           + lambda_init)
    gain = g_ref[...] * (1.0 - lambda_init)

    def make_lhs(q):
        lane = lax.broadcasted_iota(jnp.int32, q.shape, 1)
        zero = jnp.zeros_like(q)
        return jnp.concatenate([jnp.where(lane < DIFF_HEAD_DIM, q, zero),
                                jnp.where(lane >= DIFF_HEAD_DIM, q, zero)], axis=0)

    def finish(o, l):
        tq = o.shape[0] // 2
        out = o[:tq] / l[:tq] - lam * (o[tq:] / l[tq:])
        ms = jnp.mean(out * out, axis=-1, keepdims=True)
        return out * lax.rsqrt(ms + NORM_EPS) * gain

    _attn_step(q_ref, lambda: k_ref[0], v_ref, o_ref, scratch[:SCORE_BUFS],
               scratch[SCORE_BUFS:], make_lhs, finish)


def _diffattn_call(lams, g, qk, v, c_pad, w_ada, b_ada, mod_col0, n_heads, lambda_init,
                   tq=256, n_tiles=4):
    b, s, _ = qk.shape
    steps, q_spec, k_spec, v_spec, o_spec, _ = _attn_specs(
        b, n_heads, s, tq, n_tiles, LANES, LANES, q_col=0, k_col=n_heads, v_col=0)
    d = c_pad.shape[1]
    n_mod = w_ada.shape[1] - mod_col0
    mw = LANES * pl.cdiv(n_mod // LANES, steps - 1)
    n_blk = n_mod // mw
    assert n_mod % mw == 0 and mod_col0 % mw == 0 and n_blk <= steps
    blk = lambda g: jnp.minimum(g, n_blk - 1)
    vec = lambda a: pl.BlockSpec(a.shape, lambda g: (0, 0))
    return pl.pallas_call(
        functools.partial(_diffattn_kernel, lambda_init=lambda_init),
        out_shape=(jax.ShapeDtypeStruct((b, s, n_heads * DIFF_V_DIM), BF16),
                   jax.ShapeDtypeStruct((c_pad.shape[0], n_mod), F32)),
        grid=(steps,),
        in_specs=[vec(lams[0]), vec(lams[1]), vec(lams[2]), vec(lams[3]), vec(g), vec(c_pad),
                  pl.BlockSpec((d, mw), lambda g: (0, mod_col0 // mw + blk(g))),
                  pl.BlockSpec((1, mw), lambda g: (0, mod_col0 // mw + blk(g))),
                  q_spec, k_spec, v_spec],
        out_specs=(o_spec, pl.BlockSpec((c_pad.shape[0], mw), lambda g: (0, blk(g)))),
        scratch_shapes=_attn_scratch(2 * tq, s, n_tiles),
        compiler_params=_cparams(("arbitrary",)),
        name="diffattn",
    )(*lams, g, c_pad, w_ada, b_ada, qk, qk, v)


def _mlaattn_kernel(q_ref, kn_ref, kr_ref, v_ref, o_ref, *scratch):
    keys = lambda: jnp.concatenate([kn_ref[0], kr_ref[0]], axis=1)
    _attn_step(q_ref, keys, v_ref, o_ref, scratch[:SCORE_BUFS], scratch[SCORE_BUFS:],
               lambda q: q, lambda o, l: o / l)


def _mlaattn_call(qm, kn, kr, vm, n_heads, tq=512, n_tiles=4):
    b, s, _ = qm.shape
    steps, q_spec, _, v_spec, o_spec, cur = _attn_specs(b, n_heads, s, tq, n_tiles,
                                                        MLA_QK_PAD, MLA_V_DIM)
    kn_spec = pl.BlockSpec((1, s, MLA_NOPE_DIM), lambda g: (cur(g)[0], 0, cur(g)[1]))
    kr_spec = pl.BlockSpec((1, s, kr.shape[2]), lambda g: (cur(g)[0], 0, 0))
    assert MLA_NOPE_DIM + kr.shape[2] == MLA_QK_PAD
    return pl.pallas_call(
        _mlaattn_kernel,
        out_shape=jax.ShapeDtypeStruct(vm.shape, BF16),
        grid=(steps,),
        in_specs=[q_spec, kn_spec, kr_spec, v_spec],
        out_specs=o_spec,
        scratch_shapes=_attn_scratch(tq, s, n_tiles),
        compiler_params=_cparams(("arbitrary",)),
        name="mlaattn",
    )(qm, kn, kr, vm)


def _outproj_kernel(od_ref, om_ref, w_ref, x_ref, mod_ref, o_ref, wb_ref):
    @pl.when(pl.program_id(0) == 0)
    def _():
        wb_ref[...] = w_ref[...].astype(BF16)

    kd = od_ref.shape[1]
    acc = _dot(od_ref[...], wb_ref[:kd, :])
    acc += _dot(om_ref[...], wb_ref[kd:, :])
    o_ref[...] = x_ref[...] + mod_ref[0, 0:1, :] * acc


def _outproj_call(od, om, w_out, x2, mod3, seq, tm=512):
    t, d = x2.shape
    tpb = seq // tm
    assert seq % tm == 0 and w_out.shape[0] == od.shape[1] + om.shape[1]
    return pl.pallas_call(
        _outproj_kernel,
        out_shape=jax.ShapeDtypeStruct((t, d), F32),
        grid=(t // tm,),
        in_specs=[pl.BlockSpec((tm, od.shape[1]), lambda i: (i, 0)),
                  pl.BlockSpec((tm, om.shape[1]), lambda i: (i, 0)),
                  pl.BlockSpec(w_out.shape, lambda i: (0, 0), pipeline_mode=pl.Buffered(1)),
                  pl.BlockSpec((tm, d), lambda i: (i, 0)),
                  pl.BlockSpec((1, mod3.shape[1], d), lambda i: (i // tpb, 0, 0))],
        out_specs=pl.BlockSpec((tm, d), lambda i: (i, 0)),
        scratch_shapes=[pltpu.VMEM(w_out.shape, BF16)],
        compiler_params=_cparams(("arbitrary",)),
        name="outproj",
    )(od, om, w_out, x2, mod3)


def _ffn_kernel(x_ref, mod_ref, g_ref, w1a_ref, w1b_ref, w2p_ref, w2a_ref, gf_ref, o_ref,
                h_ref, ra_ref, rb_ref, *, rows):
    j = pl.program_id(1)
    nj = pl.num_programs(1) - 1
    act = lambda u: jnp.square(jnp.maximum(u, 0.0)).astype(BF16)

    @pl.when(j == 0)
    def _():
        _norm_modulate_rows(x_ref, g_ref, mod_ref, 1, 2, h_ref, rows)
        ra_ref[...] = act(_dot(h_ref[...], w1a_ref[...].astype(BF16)))
        u1 = _dot(h_ref[...], w1b_ref[...].astype(BF16))
        o_ref[...] = _dot(ra_ref[...], w2a_ref[...].astype(BF16))
        rb_ref[...] = act(u1)

    @pl.when((j > 0) & (j < nj))
    def _():
        u0 = _dot(h_ref[...], w1a_ref[...].astype(BF16))
        o_ref[...] += _dot(rb_ref[...], w2p_ref[...].astype(BF16))
        ra_ref[...] = act(u0)
        u1 = _dot(h_ref[...], w1b_ref[...].astype(BF16))
        o_ref[...] += _dot(ra_ref[...], w2a_ref[...].astype(BF16))
        rb_ref[...] = act(u1)

    @pl.when(j == nj)
    def _():
        o_ref[...] += _dot(rb_ref[...], w2p_ref[...].astype(BF16))
        d = x_ref.shape[1]
        gate = mod_ref[0, 3:4, :]
        gf = gf_ref[...]
        resid = lambda sl: x_ref[sl, :] + gate * o_ref[sl, :]

        def sumsq(sl):
            y = resid(sl)
            return jnp.sum(y * y, axis=-1, keepdims=True)

        def finish(sl, ss):
            o_ref[sl, :] = resid(sl) * lax.rsqrt(ss * (1.0 / d) + NORM_EPS) * gf

        _two_stage_rows(x_ref.shape[0], rows, sumsq, finish)


def _ffn_call(x1, mod3, g, w1, w2, gf, seq, tm=1024, tf=256):
    t, d = x1.shape
    n_chunks = w1.shape[1] // tf
    nj = n_chunks // 2
    tpb = seq // tm
    last_a, last_b = n_chunks - 2, n_chunks - 1
    return pl.pallas_call(
        functools.partial(_ffn_kernel, rows=64),
        out_shape=jax.ShapeDtypeStruct((t, d), F32),
        grid=(t // tm, nj + 1),
        in_specs=[pl.BlockSpec((tm, d), lambda i, j: (i, 0)),
                  pl.BlockSpec((1, mod3.shape[1], d), lambda i, j: (i // tpb, 0, 0)),
                  pl.BlockSpec((1, d), lambda i, j: (0, 0)),
                  pl.BlockSpec((d, tf), lambda i, j: (0, jnp.minimum(2 * j, last_a))),
                  pl.BlockSpec((d, tf), lambda i, j: (0, jnp.minimum(2 * j + 1, last_b))),
                  pl.BlockSpec((tf, d), lambda i, j: (jnp.maximum(2 * j - 1, 0), 0)),
                  pl.BlockSpec((tf, d), lambda i, j: (jnp.minimum(2 * j, last_a), 0)),
                  pl.BlockSpec((1, d), lambda i, j: (0, 0))],
        out_specs=pl.BlockSpec((tm, d), lambda i, j: (i, 0)),
        scratch_shapes=[pltpu.VMEM((tm, d), BF16),
                        pltpu.VMEM((tm, tf), BF16),
                        pltpu.VMEM((tm, tf), BF16)],
        compiler_params=_cparams(("arbitrary", "arbitrary")),
        name="ffn",
    )(x1, mod3, g, w1, w1, w2, w2, gf)


def _rope_tables(positions, specs):
    inv = jnp.concatenate([1.0 / (theta ** (jnp.arange(0, rot, 2, dtype=F32) / rot))
                           for rot, theta, _ in specs])
    ang = positions.astype(F32).reshape(-1, 1) * inv[None, :]
    cos_all, sin_all = jnp.cos(ang), jnp.sin(ang)
    t = ang.shape[0]
    out, start = [], 0
    for rot, _, group in specs:
        cos, sin = cos_all[:, start:start + rot // 2], sin_all[:, start:start + rot // 2]
        start += rot // 2
        widen = lambda lo, hi, fill: jnp.concatenate(
            [lo, hi, jnp.full((t, group - rot), fill, F32)], axis=1)
        out.append((widen(cos, cos, 1.0), widen(-sin, sin, 0.0)))
    return out


def kernel(x, c, positions, w_ada, b_ada, g_norm_mix, w_in, lambda_q1, lambda_k1, lambda_q2,
           lambda_k2, g_diff_sub, g_q_a, w_q_b, g_kv_a, w_kv_b, w_out, g_norm_ffn, w_ff1, w_ff2,
           g_final):
    b, s, d = x.shape
    depth = w_ada.shape[0]
    t = b * s
    q_rank = g_q_a.shape[1]
    kv_rank = g_kv_a.shape[1]
    n_mla = w_kv_b.shape[2] // (MLA_NOPE_DIM + MLA_V_DIM)
    n_diff = (w_in.shape[2] - q_rank - kv_rank - MLA_ROPE_DIM) // (3 * DIFF_V_DIM)
    assert n_diff == n_mla

    tabs_d, tabs_m = _rope_tables(positions, ((DIFF_ROT_DIM, ROPE_THETA, DIFF_HEAD_DIM),
                                             (MLA_ROPE_DIM, MLA_ROPE_THETA, LANES)))
    c_pad = jnp.pad(c, ((0, 8 - b), (0, 0)))
    x2 = x.reshape(t, d)

    for l in range(depth):
        lambda_init = 0.8 - 0.6 * float(np.exp(-0.3 * l))
        assert w_ada.shape[2] == (MOD_MIX_ROWS + MOD_REST_ROWS) * d
        mod_mix = _mod_call(c_pad, w_ada[l], b_ada[l][None, :], MOD_MIX_ROWS * d)
        mod_mix = mod_mix[:b].reshape(b, MOD_MIX_ROWS, d)

        w_t = jnp.swapaxes(w_in[l], 0, 1)
        dqk, dv, lat = _inproj_call(x2, mod_mix, g_norm_mix[l][None, :], w_t, tabs_d,
                                    n_diff * DIFF_V_DIM, q_rank, kv_rank, s)

        wq_pad = jnp.pad(w_q_b[l].reshape(q_rank, n_mla, MLA_NOPE_DIM + MLA_ROPE_DIM),
                         ((0, 0), (0, 0), (0, MLA_QK_PAD - MLA_NOPE_DIM - MLA_ROPE_DIM))
                         ).reshape(q_rank, n_mla * MLA_QK_PAD).astype(BF16)
        qm, kn, kr, vm = _latent_call(lat, tabs_m, g_q_a[l][None, :], g_kv_a[l][None, :],
                                      wq_pad, w_kv_b[l].astype(BF16), n_mla)

        shp = lambda a: a.reshape(b, s, a.shape[1])
        lams = tuple(v[l][None, :] for v in (lambda_q1, lambda_k1, lambda_q2, lambda_k2))
        o_diff, mod_rest = _diffattn_call(lams, g_diff_sub[l][None, :], shp(dqk), shp(dv), c_pad,
                                          w_ada[l], b_ada[l][None, :], MOD_MIX_ROWS * d, n_diff,
                                          lambda_init)
        mod_rest = mod_rest[:b].reshape(b, MOD_REST_ROWS, d)
        o_mla = _mlaattn_call(shp(qm), shp(kn), shp(kr), shp(vm), n_mla)

        x2 = _outproj_call(o_diff.reshape(t, -1), o_mla.reshape(t, -1), w_out[l], x2, mod_rest, s)

        last = l == depth - 1
        assert last, "final rmsnorm is fused into the last layer's MLP kernel"
        x2 = _ffn_call(x2, mod_rest, g_norm_ffn[l][None, :], w_ff1[l], w_ff2[l],
                       g_final[None, :], s)

    return x2.reshape(b, s, d)
```

```python
import functools
import math

import jax
import jax.numpy as jnp
import numpy as np
from jax import lax
from jax.experimental import pallas as pl
from jax.experimental.pallas import tpu as pltpu

F32 = jnp.float32
BF16 = jnp.bfloat16

LANES = 128
NORM_EPS = 1e-6

DIFF_HEAD_DIM = 64
DIFF_V_DIM = 2 * DIFF_HEAD_DIM
DIFF_ROT_DIM = DIFF_HEAD_DIM // 4
ROPE_THETA = 500000.0
MLA_V_DIM = 128
MLA_NOPE_DIM = 128
MLA_ROPE_DIM = 64
MLA_ROPE_THETA = 10000.0
MLA_QK_PAD = 256

DIFF_SCALE = DIFF_HEAD_DIM ** -0.5 * math.log2(math.e)
MLA_SCALE = (MLA_NOPE_DIM + MLA_ROPE_DIM) ** -0.5 * math.log2(math.e)

MIB = 1024 * 1024
VMEM_LIMIT = 56 * MIB


def _cparams(sem, vmem_limit=VMEM_LIMIT):
    return pltpu.CompilerParams(dimension_semantics=sem, vmem_limit_bytes=vmem_limit)


def _dot(a, b):
    return jnp.dot(a, b, preferred_element_type=F32)


def _dot_nt(a, b):
    return lax.dot_general(a, b, (((1,), (1,)), ((), ())), preferred_element_type=F32)


MOD_MIX_ROWS = 2
MOD_REST_ROWS = 4


def _mod_block(c_ref, w_ref, b_ref):
    c = c_ref[...]
    sc = c / (1.0 + jnp.exp(-c))
    return _dot(sc.astype(BF16), w_ref[...].astype(BF16)) + b_ref[...]


def _mod_kernel(c_ref, w_ref, b_ref, o_ref):
    o_ref[...] = _mod_block(c_ref, w_ref, b_ref)


def _mod_call(c_pad, w_ada, b_ada, n, tn=1024):
    m, d = c_pad.shape
    return pl.pallas_call(
        _mod_kernel,
        out_shape=jax.ShapeDtypeStruct((m, n), F32),
        grid=(n // tn,),
        in_specs=[pl.BlockSpec((m, d), lambda j: (0, 0)),
                  pl.BlockSpec((d, tn), lambda j: (0, j)),
                  pl.BlockSpec((1, tn), lambda j: (0, j))],
        out_specs=pl.BlockSpec((m, tn), lambda j: (0, j)),
        compiler_params=_cparams(("arbitrary",)),
        name="mod",
    )(c_pad, w_ada, b_ada)


def _two_stage_rows(n_rows, rows, sumsq, finish):
    chunk = lambda r: pl.ds(pl.multiple_of(r * rows, rows), rows)

    def body(r, ss_prev):
        ss = sumsq(chunk(r))
        finish(chunk(r - 1), ss_prev)
        return ss

    n = n_rows // rows
    ss_last = lax.fori_loop(1, n, body, sumsq(chunk(0)))
    finish(chunk(n - 1), ss_last)


def _norm_modulate_rows(x_ref, g_ref, mod_ref, shift_row, scale_row, h_ref, rows):
    tm, d = x_ref.shape
    shift = mod_ref[0, shift_row:shift_row + 1, :]
    gs = g_ref[...] * (1.0 + mod_ref[0, scale_row:scale_row + 1, :])

    def sumsq(sl):
        x = x_ref[sl, :]
        return jnp.sum(x * x, axis=-1, keepdims=True)

    def finish(sl, ss):
        rs = lax.rsqrt(ss * (1.0 / d) + NORM_EPS)
        h_ref[sl, :] = (x_ref[sl, :] * rs * gs + shift).astype(BF16)

    _two_stage_rows(tm, rows, sumsq, finish)


def _rope_slab(x, c, s, half, group):
    g = lax.broadcasted_iota(jnp.int32, (1, LANES), 1) % group
    partner = jnp.where(g >= half, pltpu.roll(x, half, 1), pltpu.roll(x, LANES - half, 1))
    return x * c + partner * s


def _rms_rows(x, g):
    ms = jnp.mean(x * x, axis=-1, keepdims=True)
    return x * lax.rsqrt(ms + NORM_EPS) * g


def _inproj_kernel(x_ref, mod_ref, g_ref, w_ref, cd_ref, sd_ref,
                   qk_ref, v_ref, lat_ref, h_ref, r_ref, *, rows, lat_cols):
    j = pl.program_id(1)
    tn = r_ref.shape[1]

    def project():
        return _dot_nt(h_ref[...], w_ref[...].astype(BF16))

    @pl.when(j == 0)
    def _():
        _norm_modulate_rows(x_ref, g_ref, mod_ref, 0, 1, h_ref, rows)
        r_ref[...] = project()

    @pl.when((j >= 1) & (j <= 4))
    def _():
        repeat = lambda ref: jnp.concatenate([ref[...]] * (LANES // ref.shape[1]), axis=1)
        c, s = repeat(cd_ref), repeat(sd_ref)
        scale = jnp.where(j <= 2, DIFF_SCALE, 1.0)
        for k in range(tn // LANES):
            sl = slice(k * LANES, (k + 1) * LANES)
            qk_ref[:, sl] = (_rope_slab(r_ref[:, sl], c, s, DIFF_ROT_DIM // 2, cd_ref.shape[1])
                             * scale).astype(BF16)
        r_ref[...] = project()

    @pl.when(j == 5)
    def _():
        v_ref[:, :tn] = r_ref[...].astype(BF16)
        r_ref[...] = project()

    @pl.when(j == 6)
    def _():
        v_ref[:, tn:] = r_ref[...].astype(BF16)
        r_ref[...] = project()

    @pl.when(j == 7)
    def _():
        lat_ref[:, :tn] = r_ref[...]
        lat_ref[:, tn:tn + lat_cols] = project()[:, :lat_cols]
        pad = lat_ref.shape[1] - (tn + lat_cols)
        lat_ref[:, tn + lat_cols:] = jnp.zeros((lat_ref.shape[0], pad), F32)


def _inproj_call(x2, mod3, g, w_t, tabs_d, dw, q_rank, kv_rank, seq, tm=1024):
    t, d = x2.shape
    tpb = seq // tm
    lat_cols = kv_rank + MLA_ROPE_DIM
    lat_w = q_rank + kv_rank + LANES
    tn = dw // 2
    assert dw % (2 * LANES) == 0 and seq % tm == 0 and q_rank == tn and lat_cols <= tn
    assert w_t.shape[0] == 3 * dw + q_rank + lat_cols
    tab = pl.BlockSpec((tm, tabs_d[0].shape[1]), lambda i, j: (i, 0))
    return pl.pallas_call(
        functools.partial(_inproj_kernel, rows=64, lat_cols=lat_cols),
        out_shape=(jax.ShapeDtypeStruct((t, 2 * dw), BF16),
                   jax.ShapeDtypeStruct((t, dw), BF16),
                   jax.ShapeDtypeStruct((t, lat_w), F32)),
        grid=(t // tm, 8),
        in_specs=[pl.BlockSpec((tm, d), lambda i, j: (i, 0)),
                  pl.BlockSpec((1, mod3.shape[1], d), lambda i, j: (i // tpb, 0, 0)),
                  pl.BlockSpec((1, d), lambda i, j: (0, 0)),
                  pl.BlockSpec((tn, d), lambda i, j: (j, 0)),
                  tab, tab],
        out_specs=(pl.BlockSpec((tm, tn), lambda i, j: (i, jnp.clip(j - 1, 0, 3))),
                   pl.BlockSpec((tm, dw), lambda i, j: (i, 0)),
                   pl.BlockSpec((tm, lat_w), lambda i, j: (i, 0))),
        scratch_shapes=[pltpu.VMEM((tm, d), BF16), pltpu.VMEM((tm, tn), F32)],
        compiler_params=_cparams(("arbitrary", "arbitrary"), 50 * MIB),
        name="inproj",
    )(x2, mod3, g, w_t, *tabs_d)


def _latent_kernel(lat_ref, cm_ref, sm_ref,
                   gq_ref, gkv_ref, wq_ref, wkv_ref, qmo_ref, kno_ref, kro_ref, vmo_ref,
                   *, n_heads):
    half_m = MLA_ROPE_DIM // 2
    rope_m = lambda x: _rope_slab(x, cm_ref[...], sm_ref[...], half_m, LANES)
    q_rank, kv_rank = gq_ref.shape[1], gkv_ref.shape[1]

    q = _dot(_rms_rows(lat_ref[:, :q_rank], gq_ref[...]).astype(BF16), wq_ref[...])
    kv = _dot(_rms_rows(lat_ref[:, q_rank:q_rank + kv_rank], gkv_ref[...]).astype(BF16),
              wkv_ref[...])
    kro_ref[...] = rope_m(lat_ref[:, q_rank + kv_rank:]).astype(BF16)

    for h in range(n_heads):
        base = h * MLA_QK_PAD
        nope = slice(base, base + LANES)
        rope = slice(base + LANES, base + 2 * LANES)
        head = slice(h * LANES, (h + 1) * LANES)
        qmo_ref[:, nope] = (q[:, nope] * MLA_SCALE).astype(BF16)
        qmo_ref[:, rope] = (rope_m(q[:, rope]) * MLA_SCALE).astype(BF16)
        kno_ref[:, head] = kv[:, nope].astype(BF16)
        vmo_ref[:, head] = kv[:, rope].astype(BF16)


def _latent_call(lat, tabs_m, gq, gkv, wq_pad, wkv, n_heads, tm=1024):
    t = lat.shape[0]
    q_rank = gq.shape[1]
    kv_rank = gkv.shape[1]
    qk_w = n_heads * MLA_QK_PAD
    v_w = n_heads * MLA_V_DIM
    assert q_rank % LANES == 0 and kv_rank % LANES == 0
    assert lat.shape[1] == q_rank + kv_rank + LANES
    row = lambda w, c: pl.BlockSpec((tm, w), lambda i, c=c: (i, c))
    full = lambda a: pl.BlockSpec(a.shape, lambda i: (0, 0))
    in_specs = [row(lat.shape[1], 0)]
    in_specs += [row(LANES, 0)] * len(tabs_m)
    in_specs += [full(gq), full(gkv), full(wq_pad), full(wkv)]
    kn_w = n_heads * MLA_NOPE_DIM
    out_shape = (jax.ShapeDtypeStruct((t, qk_w), BF16),
                 jax.ShapeDtypeStruct((t, kn_w), BF16),
                 jax.ShapeDtypeStruct((t, LANES), BF16),
                 jax.ShapeDtypeStruct((t, v_w), BF16))
    return pl.pallas_call(
        functools.partial(_latent_kernel, n_heads=n_heads),
        out_shape=out_shape,
        grid=(t // tm,),
        in_specs=in_specs,
        out_specs=(row(qk_w, 0), row(kn_w, 0), row(LANES, 0), row(v_w, 0)),
        compiler_params=_cparams(("arbitrary",)),
        name="latent",
    )(lat, *tabs_m, gq, gkv, wq_pad, wkv)


def _numerators(s_ref, e_ref, rows=16):
    for r in range(0, s_ref.shape[0], rows):
        s = s_ref[r:r + rows, :]
        e_ref[r:r + rows, :] = jnp.exp2(s - jnp.max(s, axis=-1, keepdims=True)).astype(BF16)


def _weighted_values(e_ref, v1):
    o = _dot(e_ref[...], v1)
    dv = v1.shape[1] // 2
    return o[:, :dv], o[:, dv:dv + 1]


SCORE_BUFS = 4
NUMER_LAG = 1


def _attn_step(q_ref, keys, v_ref, o_ref, s_refs, e_refs, make_lhs, finish):
    n, n_s = len(e_refs), len(s_refs)
    tq = o_ref.shape[1] // n
    assert n_s >= NUMER_LAG + 2 and n % n_s == 0

    @pl.when(pl.program_id(0) == 0)
    def _():
        for t in range(NUMER_LAG):
            s_refs[(t - NUMER_LAG) % n_s][...] = jnp.zeros_like(s_refs[0])
        for t in range(n - NUMER_LAG):
            e_refs[t][...] = jnp.zeros_like(e_refs[t])

    v = v_ref[0]
    v1 = jnp.concatenate([v, jnp.ones_like(v)], axis=1)
    for k in range(n):
        rows = slice(k * tq, (k + 1) * tq)
        behind = k - NUMER_LAG
        s_refs[k % n_s][...] = _dot_nt(make_lhs(q_ref[0, rows]), keys())
        _numerators(s_refs[behind % n_s], e_refs[behind % n])
        o_ref[0, rows] = finish(*_weighted_values(e_refs[k], v1)).astype(o_ref.dtype)


def _attn_specs(b, n_heads, seq, tq, n_tiles, qk_width, v_width, q_col=0, k_col=0, v_col=0):
    rows = n_tiles * tq
    n_groups = seq // rows
    total = b * n_heads * n_groups
    assert seq % rows == 0

    def decode(g):
        return g // (n_heads * n_groups), (g // n_groups) % n_heads, g % n_groups

    cur = lambda g: decode(jnp.minimum(g, total - 1))
    prev = lambda g: decode(jnp.maximum(g - 1, 0))
    q_spec = pl.BlockSpec((1, rows, qk_width),
                          lambda g: (cur(g)[0], cur(g)[2], q_col + cur(g)[1]))
    k_spec = pl.BlockSpec((1, seq, qk_width), lambda g: (cur(g)[0], 0, k_col + cur(g)[1]))
    v_spec = pl.BlockSpec((1, seq, v_width), lambda g: (prev(g)[0], 0, v_col + prev(g)[1]))
    o_spec = pl.BlockSpec((1, rows, v_width), lambda g: (prev(g)[0], prev(g)[2], prev(g)[1]))
    return total + 1, q_spec, k_spec, v_spec, o_spec, cur


def _attn_scratch(rows, seq, n_tiles):
    return ([pltpu.VMEM((rows, seq), F32)] * SCORE_BUFS
            + [pltpu.VMEM((rows, seq), BF16)] * n_tiles)


def _diffattn_kernel(lq1_ref, lk1_ref, lq2_ref, lk2_ref, g_ref, c_ref, wa_ref, ba_ref,
                     q_ref, k_ref, v_ref, o_ref, mod_ref, *scratch, lambda_init):
    mod_ref[...] = _mod_block(c_ref, wa_ref, ba_ref)

    lam = (jnp.exp(jnp.sum(lq1_ref[...] * lk1_ref[...], axis=-1, keepdims=True))
           - jnp.exp(jnp.sum(lq2_ref[...] * lk2_ref[...], axis=-1, keepdims=True))
           + lambda_init)
    gain = g_ref[...] * (1.0 - lambda_init)

    def make_lhs(q):
        lane = lax.broadcasted_iota(jnp.int32, q.shape, 1)
        zero = jnp.zeros_like(q)
        return jnp.concatenate([jnp.where(lane < DIFF_HEAD_DIM, q, zero),
                                jnp.where(lane >= DIFF_HEAD_DIM, q, zero)], axis=0)

    def finish(o, l):
        tq = o.shape[0] // 2
        out = o[:tq] / l[:tq] - lam * (o[tq:] / l[tq:])
        ms = jnp.mean(out * out, axis=-1, keepdims=True)
        return out * lax.rsqrt(ms + NORM_EPS) * gain

    _attn_step(q_ref, lambda: k_ref[0], v_ref, o_ref, scratch[:SCORE_BUFS],
               scratch[SCORE_BUFS:], make_lhs, finish)


def _diffattn_call(lams, g, qk, v, c_pad, w_ada, b_ada, mod_col0, n_heads, lambda_init,
                   tq=256, n_tiles=4):
    b, s, _ = qk.shape
    steps, q_spec, k_spec, v_spec, o_spec, _ = _attn_specs(
        b, n_heads, s, tq, n_tiles, LANES, LANES, q_col=0, k_col=n_heads, v_col=0)
    d = c_pad.shape[1]
    n_mod = w_ada.shape[1] - mod_col0
    mw = LANES * pl.cdiv(n_mod // LANES, steps - 1)
    n_blk = n_mod // mw
    assert n_mod % mw == 0 and mod_col0 % mw == 0 and n_blk <= steps
    blk = lambda g: jnp.minimum(g, n_blk - 1)
    vec = lambda a: pl.BlockSpec(a.shape, lambda g: (0, 0))
    return pl.pallas_call(
        functools.partial(_diffattn_kernel, lambda_init=lambda_init),
        out_shape=(jax.ShapeDtypeStruct((b, s, n_heads * DIFF_V_DIM), BF16),
                   jax.ShapeDtypeStruct((c_pad.shape[0], n_mod), F32)),
        grid=(steps,),
        in_specs=[vec(lams[0]), vec(lams[1]), vec(lams[2]), vec(lams[3]), vec(g), vec(c_pad),
                  pl.BlockSpec((d, mw), lambda g: (0, mod_col0 // mw + blk(g))),
                  pl.BlockSpec((1, mw), lambda g: (0, mod_col0 // mw + blk(g))),
                  q_spec, k_spec, v_spec],
        out_specs=(o_spec, pl.BlockSpec((c_pad.shape[0], mw), lambda g: (0, blk(g)))),
        scratch_shapes=_attn_scratch(2 * tq, s, n_tiles),
        compiler_params=_cparams(("arbitrary",)),
        name="diffattn",
    )(*lams, g, c_pad, w_ada, b_ada, qk, qk, v)


def _mlaattn_kernel(q_ref, kn_ref, kr_ref, v_ref, o_ref, *scratch):
    keys = lambda: jnp.concatenate([kn_ref[0], kr_ref[0]], axis=1)
    _attn_step(q_ref, keys, v_ref, o_ref, scratch[:SCORE_BUFS], scratch[SCORE_BUFS:],
               lambda q: q, lambda o, l: o / l)


def _mlaattn_call(qm, kn, kr, vm, n_heads, tq=512, n_tiles=4):
    b, s, _ = qm.shape
    steps, q_spec, _, v_spec, o_spec, cur = _attn_specs(b, n_heads, s, tq, n_tiles,
                                                        MLA_QK_PAD, MLA_V_DIM)
    kn_spec = pl.BlockSpec((1, s, MLA_NOPE_DIM), lambda g: (cur(g)[0], 0, cur(g)[1]))
    kr_spec = pl.BlockSpec((1, s, kr.shape[2]), lambda g: (cur(g)[0], 0, 0))
    assert MLA_NOPE_DIM + kr.shape[2] == MLA_QK_PAD
    return pl.pallas_call(
        _mlaattn_kernel,
        out_shape=jax.ShapeDtypeStruct(vm.shape, BF16),
        grid=(steps,),
        in_specs=[q_spec, kn_spec, kr_spec, v_spec],
        out_specs=o_spec,
        scratch_shapes=_attn_scratch(tq, s, n_tiles),
        compiler_params=_cparams(("arbitrary",)),
        name="mlaattn",
    )(qm, kn, kr, vm)


def _outproj_kernel(od_ref, om_ref, w_ref, x_ref, mod_ref, o_ref, wb_ref):
    @pl.when(pl.program_id(0) == 0)
    def _():
        wb_ref[...] = w_ref[...].astype(BF16)

    kd = od_ref.shape[1]
    acc = _dot(od_ref[...], wb_ref[:kd, :])
    acc += _dot(om_ref[...], wb_ref[kd:, :])
    o_ref[...] = x_ref[...] + mod_ref[0, 0:1, :] * acc


def _outproj_call(od, om, w_out, x2, mod3, seq, tm=512):
    t, d = x2.shape
    tpb = seq // tm
    assert seq % tm == 0 and w_out.shape[0] == od.shape[1] + om.shape[1]
    return pl.pallas_call(
        _outproj_kernel,
        out_shape=jax.ShapeDtypeStruct((t, d), F32),
        grid=(t // tm,),
        in_specs=[pl.BlockSpec((tm, od.shape[1]), lambda i: (i, 0)),
                  pl.BlockSpec((tm, om.shape[1]), lambda i: (i, 0)),
                  pl.BlockSpec(w_out.shape, lambda i: (0, 0), pipeline_mode=pl.Buffered(1)),
                  pl.BlockSpec((tm, d), lambda i: (i, 0)),
                  pl.BlockSpec((1, mod3.shape[1], d), lambda i: (i // tpb, 0, 0))],
        out_specs=pl.BlockSpec((tm, d), lambda i: (i, 0)),
        scratch_shapes=[pltpu.VMEM(w_out.shape, BF16)],
        compiler_params=_cparams(("arbitrary",)),
        name="outproj",
    )(od, om, w_out, x2, mod3)


def _ffn_kernel(x_ref, mod_ref, g_ref, w1a_ref, w1b_ref, w2p_ref, w2a_ref, gf_ref, o_ref,
                h_ref, ra_ref, rb_ref, *, rows):
    j = pl.program_id(1)
    nj = pl.num_programs(1) - 1
    act = lambda u: jnp.square(jnp.maximum(u, 0.0)).astype(BF16)

    @pl.when(j == 0)
    def _():
        _norm_modulate_rows(x_ref, g_ref, mod_ref, 1, 2, h_ref, rows)
        ra_ref[...] = act(_dot(h_ref[...], w1a_ref[...].astype(BF16)))
        u1 = _dot(h_ref[...], w1b_ref[...].astype(BF16))
        o_ref[...] = _dot(ra_ref[...], w2a_ref[...].astype(BF16))
        rb_ref[...] = act(u1)

    @pl.when((j > 0) & (j < nj))
    def _():
        u0 = _dot(h_ref[...], w1a_ref[...].astype(BF16))
        o_ref[...] += _dot(rb_ref[...], w2p_ref[...].astype(BF16))
        ra_ref[...] = act(u0)
        u1 = _dot(h_ref[...], w1b_ref[...].astype(BF16))
        o_ref[...] += _dot(ra_ref[...], w2a_ref[...].astype(BF16))
        rb_ref[...] = act(u1)

    @pl.when(j == nj)
    def _():
        o_ref[...] += _dot(rb_ref[...], w2p_ref[...].astype(BF16))
        d = x_ref.shape[1]
        gate = mod_ref[0, 3:4, :]
        gf = gf_ref[...]
        resid = lambda sl: x_ref[sl, :] + gate * o_ref[sl, :]

        def sumsq(sl):
            y = resid(sl)
            return jnp.sum(y * y, axis=-1, keepdims=True)

        def finish(sl, ss):
            o_ref[sl, :] = resid(sl) * lax.rsqrt(ss * (1.0 / d) + NORM_EPS) * gf

        _two_stage_rows(x_ref.shape[0], rows, sumsq, finish)


def _ffn_call(x1, mod3, g, w1, w2, gf, seq, tm=1024, tf=256):
    t, d = x1.shape
    n_chunks = w1.shape[1] // tf
    nj = n_chunks // 2
    tpb = seq // tm
    last_a, last_b = n_chunks - 2, n_chunks - 1
    return pl.pallas_call(
        functools.partial(_ffn_kernel, rows=64),
        out_shape=jax.ShapeDtypeStruct((t, d), F32),
        grid=(t // tm, nj + 1),
        in_specs=[pl.BlockSpec((tm, d), lambda i, j: (i, 0)),
                  pl.BlockSpec((1, mod3.shape[1], d), lambda i, j: (i // tpb, 0, 0)),
                  pl.BlockSpec((1, d), lambda i, j: (0, 0)),
                  pl.BlockSpec((d, tf), lambda i, j: (0, jnp.minimum(2 * j, last_a))),
                  pl.BlockSpec((d, tf), lambda i, j: (0, jnp.minimum(2 * j + 1, last_b))),
                  pl.BlockSpec((tf, d), lambda i, j: (jnp.maximum(2 * j - 1, 0), 0)),
                  pl.BlockSpec((tf, d), lambda i, j: (jnp.minimum(2 * j, last_a), 0)),
                  pl.BlockSpec((1, d), lambda i, j: (0, 0))],
        out_specs=pl.BlockSpec((tm, d), lambda i, j: (i, 0)),
        scratch_shapes=[pltpu.VMEM((tm, d), BF16),
                        pltpu.VMEM((tm, tf), BF16),
                        pltpu.VMEM((tm, tf), BF16)],
        compiler_params=_cparams(("arbitrary", "arbitrary")),
        name="ffn",
    )(x1, mod3, g, w1, w1, w2, w2, gf)


def _rope_tables(positions, specs):
    inv = jnp.concatenate([1.0 / (theta ** (jnp.arange(0, rot, 2, dtype=F32) / rot))
                           for rot, theta, _ in specs])
    ang = positions.astype(F32).reshape(-1, 1) * inv[None, :]
    cos_all, sin_all = jnp.cos(ang), jnp.sin(ang)
    t = ang.shape[0]
    out, start = [], 0
    for rot, _, group in specs:
        cos, sin = cos_all[:, start:start + rot // 2], sin_all[:, start:start + rot // 2]
        start += rot // 2
        widen = lambda lo, hi, fill: jnp.concatenate(
            [lo, hi, jnp.full((t, group - rot), fill, F32)], axis=1)
        out.append((widen(cos, cos, 1.0), widen(-sin, sin, 0.0)))
    return out


def kernel(x, c, positions, w_ada, b_ada, g_norm_mix, w_in, lambda_q1, lambda_k1, lambda_q2,
           lambda_k2, g_diff_sub, g_q_a, w_q_b, g_kv_a, w_kv_b, w_out, g_norm_ffn, w_ff1, w_ff2,
           g_final):
    b, s, d = x.shape
    depth = w_ada.shape[0]
    t = b * s
    q_rank = g_q_a.shape[1]
    kv_rank = g_kv_a.shape[1]
    n_mla = w_kv_b.shape[2] // (MLA_NOPE_DIM + MLA_V_DIM)
    n_diff = (w_in.shape[2] - q_rank - kv_rank - MLA_ROPE_DIM) // (3 * DIFF_V_DIM)
    assert n_diff == n_mla

    tabs_d, tabs_m = _rope_tables(positions, ((DIFF_ROT_DIM, ROPE_THETA, DIFF_HEAD_DIM),
                                             (MLA_ROPE_DIM, MLA_ROPE_THETA, LANES)))
    c_pad = jnp.pad(c, ((0, 8 - b), (0, 0)))
    x2 = x.reshape(t, d)

    for l in range(depth):
        lambda_init = 0.8 - 0.6 * float(np.exp(-0.3 * l))
        assert w_ada.shape[2] == (MOD_MIX_ROWS + MOD_REST_ROWS) * d
        mod_mix = _mod_call(c_pad, w_ada[l], b_ada[l][None, :], MOD_MIX_ROWS * d)
        mod_mix = mod_mix[:b].reshape(b, MOD_MIX_ROWS, d)

        w_t = jnp.swapaxes(w_in[l], 0, 1)
        dqk, dv, lat = _inproj_call(x2, mod_mix, g_norm_mix[l][None, :], w_t, tabs_d,
                                    n_diff * DIFF_V_DIM, q_rank, kv_rank, s)

        wq_pad = jnp.pad(w_q_b[l].reshape(q_rank, n_mla, MLA_NOPE_DIM + MLA_ROPE_DIM),
                         ((0, 0), (0, 0), (0, MLA_QK_PAD - MLA_NOPE_DIM - MLA_ROPE_DIM))
                         ).reshape(q_rank, n_mla * MLA_QK_PAD).astype(BF16)
        qm, kn, kr, vm = _latent_call(lat, tabs_m, g_q_a[l][None, :], g_kv_a[l][None, :],
                                      wq_pad, w_kv_b[l].astype(BF16), n_mla)

        shp = lambda a: a.reshape(b, s, a.shape[1])
        lams = tuple(v[l][None, :] for v in (lambda_q1, lambda_k1, lambda_q2, lambda_k2))
        o_diff, mod_rest = _diffattn_call(lams, g_diff_sub[l][None, :], shp(dqk), shp(dv), c_pad,
                                          w_ada[l], b_ada[l][None, :], MOD_MIX_ROWS * d, n_diff,
                                          lambda_init)
        mod_rest = mod_rest[:b].reshape(b, MOD_REST_ROWS, d)
        o_mla = _mlaattn_call(shp(qm), shp(kn), shp(kr), shp(vm), n_mla)

        x2 = _outproj_call(o_diff.reshape(t, -1), o_mla.reshape(t, -1), w_out[l], x2, mod_rest, s)

        last = l == depth - 1
        assert last, "final rmsnorm is fused into the last layer's MLP kernel"
        x2 = _ffn_call(x2, mod_rest, g_norm_ffn[l][None, :], w_ff1[l], w_ff2[l],
                       g_final[None, :], s)

    return x2.reshape(b, s, d)
```

```python
import functools
import math

import jax
import jax.numpy as jnp
import numpy as np
from jax import lax
from jax.experimental import pallas as pl
from jax.experimental.pallas import tpu as pltpu

F32 = jnp.float32
BF16 = jnp.bfloat16

LANES = 128
NORM_EPS = 1e-6

DIFF_HEAD_DIM = 64
DIFF_V_DIM = 2 * DIFF_HEAD_DIM
DIFF_ROT_DIM = DIFF_HEAD_DIM // 4
ROPE_THETA = 500000.0
MLA_V_DIM = 128
MLA_NOPE_DIM = 128
MLA_ROPE_DIM = 64
MLA_ROPE_THETA = 10000.0
MLA_QK_PAD = 256

DIFF_SCALE = DIFF_HEAD_DIM ** -0.5 * math.log2(math.e)
MLA_SCALE = (MLA_NOPE_DIM + MLA_ROPE_DIM) ** -0.5 * math.log2(math.e)

MIB = 1024 * 1024
VMEM_LIMIT = 56 * MIB


def _cparams(sem, vmem_limit=VMEM_LIMIT):
    return pltpu.CompilerParams(dimension_semantics=sem, vmem_limit_bytes=vmem_limit)


def _dot(a, b):
    return jnp.dot(a, b, preferred_element_type=F32)


def _dot_nt(a, b):
    return lax.dot_general(a, b, (((1,), (1,)), ((), ())), preferred_element_type=F32)


MOD_MIX_ROWS = 2
MOD_REST_ROWS = 4


def _mod_block(c_ref, w_ref, b_ref):
    c = c_ref[...]
    sc = c / (1.0 + jnp.exp(-c))
    return _dot(sc.astype(BF16), w_ref[...].astype(BF16)) + b_ref[...]


def _mod_kernel(c_ref, w_ref, b_ref, o_ref):
    o_ref[...] = _mod_block(c_ref, w_ref, b_ref)


def _mod_call(c_pad, w_ada, b_ada, n, tn=1024):
    m, d = c_pad.shape
    return pl.pallas_call(
        _mod_kernel,
        out_shape=jax.ShapeDtypeStruct((m, n), F32),
        grid=(n // tn,),
        in_specs=[pl.BlockSpec((m, d), lambda j: (0, 0)),
                  pl.BlockSpec((d, tn), lambda j: (0, j)),
                  pl.BlockSpec((1, tn), lambda j: (0, j))],
        out_specs=pl.BlockSpec((m, tn), lambda j: (0, j)),
        compiler_params=_cparams(("arbitrary",)),
        name="mod",
    )(c_pad, w_ada, b_ada)


def _two_stage_rows(n_rows, rows, sumsq, finish):
    chunk = lambda r: pl.ds(pl.multiple_of(r * rows, rows), rows)

    def body(r, ss_prev):
        ss = sumsq(chunk(r))
        finish(chunk(r - 1), ss_prev)
        return ss

    n = n_rows // rows
    ss_last = lax.fori_loop(1, n, body, sumsq(chunk(0)))
    finish(chunk(n - 1), ss_last)


def _norm_modulate_rows(x_ref, g_ref, mod_ref, shift_row, scale_row, h_ref, rows):
    tm, d = x_ref.shape
    shift = mod_ref[0, shift_row:shift_row + 1, :]
    gs = g_ref[...] * (1.0 + mod_ref[0, scale_row:scale_row + 1, :])

    def sumsq(sl):
        x = x_ref[sl, :]
        return jnp.sum(x * x, axis=-1, keepdims=True)

    def finish(sl, ss):
        rs = lax.rsqrt(ss * (1.0 / d) + NORM_EPS)
        h_ref[sl, :] = (x_ref[sl, :] * rs * gs + shift).astype(BF16)

    _two_stage_rows(tm, rows, sumsq, finish)


def _rope_slab(x, c, s, half, group):
    g = lax.broadcasted_iota(jnp.int32, (1, LANES), 1) % group
    partner = jnp.where(g >= half, pltpu.roll(x, half, 1), pltpu.roll(x, LANES - half, 1))
    return x * c + partner * s


def _rms_rows(x, g):
    ms = jnp.mean(x * x, axis=-1, keepdims=True)
    return x * lax.rsqrt(ms + NORM_EPS) * g


def _inproj_kernel(x_ref, mod_ref, g_ref, w_ref, cd_ref, sd_ref,
                   qk_ref, v_ref, lat_ref, h_ref, r_ref, *, rows, lat_cols):
    j = pl.program_id(1)
    tn = r_ref.shape[1]

    def project():
        return _dot_nt(h_ref[...], w_ref[...].astype(BF16))

    @pl.when(j == 0)
    def _():
        _norm_modulate_rows(x_ref, g_ref, mod_ref, 0, 1, h_ref, rows)
        r_ref[...] = project()

    @pl.when((j >= 1) & (j <= 4))
    def _():
        repeat = lambda ref: jnp.concatenate([ref[...]] * (LANES // ref.shape[1]), axis=1)
        c, s = repeat(cd_ref), repeat(sd_ref)
        scale = jnp.where(j <= 2, DIFF_SCALE, 1.0)
        for k in range(tn // LANES):
            sl = slice(k * LANES, (k + 1) * LANES)
            qk_ref[:, sl] = (_rope_slab(r_ref[:, sl], c, s, DIFF_ROT_DIM // 2, cd_ref.shape[1])
                             * scale).astype(BF16)
        r_ref[...] = project()

    @pl.when(j == 5)
    def _():
        v_ref[:, :tn] = r_ref[...].astype(BF16)
        r_ref[...] = project()

    @pl.when(j == 6)
    def _():
        v_ref[:, tn:] = r_ref[...].astype(BF16)
        r_ref[...] = project()

    @pl.when(j == 7)
    def _():
        lat_ref[:, :tn] = r_ref[...]
        lat_ref[:, tn:tn + lat_cols] = project()[:, :lat_cols]
        pad = lat_ref.shape[1] - (tn + lat_cols)
        lat_ref[:, tn + lat_cols:] = jnp.zeros((lat_ref.shape[0], pad), F32)


def _inproj_call(x2, mod3, g, w_t, tabs_d, dw, q_rank, kv_rank, seq, tm=1024):
    t, d = x2.shape
    tpb = seq // tm
    lat_cols = kv_rank + MLA_ROPE_DIM
    lat_w = q_rank + kv_rank + LANES
    tn = dw // 2
    assert dw % (2 * LANES) == 0 and seq % tm == 0 and q_rank == tn and lat_cols <= tn
    assert w_t.shape[0] == 3 * dw + q_rank + lat_cols
    tab = pl.BlockSpec((tm, tabs_d[0].shape[1]), lambda i, j: (i, 0))
    return pl.pallas_call(
        functools.partial(_inproj_kernel, rows=64, lat_cols=lat_cols),
        out_shape=(jax.ShapeDtypeStruct((t, 2 * dw), BF16),
                   jax.ShapeDtypeStruct((t, dw), BF16),
                   jax.ShapeDtypeStruct((t, lat_w), F32)),
        grid=(t // tm, 8),
        in_specs=[pl.BlockSpec((tm, d), lambda i, j: (i, 0)),
                  pl.BlockSpec((1, mod3.shape[1], d), lambda i, j: (i // tpb, 0, 0)),
                  pl.BlockSpec((1, d), lambda i, j: (0, 0)),
                  pl.BlockSpec((tn, d), lambda i, j: (j, 0)),
                  tab, tab],
        out_specs=(pl.BlockSpec((tm, tn), lambda i, j: (i, jnp.clip(j - 1, 0, 3))),
                   pl.BlockSpec((tm, dw), lambda i, j: (i, 0)),
                   pl.BlockSpec((tm, lat_w), lambda i, j: (i, 0))),
        scratch_shapes=[pltpu.VMEM((tm, d), BF16), pltpu.VMEM((tm, tn), F32)],
        compiler_params=_cparams(("arbitrary", "arbitrary"), 50 * MIB),
        name="inproj",
    )(x2, mod3, g, w_t, *tabs_d)


def _latent_kernel(lat_ref, cm_ref, sm_ref,
                   gq_ref, gkv_ref, wq_ref, wkv_ref, qmo_ref, kno_ref, kro_ref, vmo_ref,
                   *, n_heads):
    half_m = MLA_ROPE_DIM // 2
    rope_m = lambda x: _rope_slab(x, cm_ref[...], sm_ref[...], half_m, LANES)
    q_rank, kv_rank = gq_ref.shape[1], gkv_ref.shape[1]

    q = _dot(_rms_rows(lat_ref[:, :q_rank], gq_ref[...]).astype(BF16), wq_ref[...])
    kv = _dot(_rms_rows(lat_ref[:, q_rank:q_rank + kv_rank], gkv_ref[...]).astype(BF16),
              wkv_ref[...])
    kro_ref[...] = rope_m(lat_ref[:, q_rank + kv_rank:]).astype(BF16)

    for h in range(n_heads):
        base = h * MLA_QK_PAD
        nope = slice(base, base + LANES)
        rope = slice(base + LANES, base + 2 * LANES)
        head = slice(h * LANES, (h + 1) * LANES)
        qmo_ref[:, nope] = (q[:, nope] * MLA_SCALE).astype(BF16)
        qmo_ref[:, rope] = (rope_m(q[:, rope]) * MLA_SCALE).astype(BF16)
        kno_ref[:, head] = kv[:, nope].astype(BF16)
        vmo_ref[:, head] = kv[:, rope].astype(BF16)


def _latent_call(lat, tabs_m, gq, gkv, wq_pad, wkv, n_heads, tm=1024):
    t = lat.shape[0]
    q_rank = gq.shape[1]
    kv_rank = gkv.shape[1]
    qk_w = n_heads * MLA_QK_PAD
    v_w = n_heads * MLA_V_DIM
    assert q_rank % LANES == 0 and kv_rank % LANES == 0
    assert lat.shape[1] == q_rank + kv_rank + LANES
    row = lambda w, c: pl.BlockSpec((tm, w), lambda i, c=c: (i, c))
    full = lambda a: pl.BlockSpec(a.shape, lambda i: (0, 0))
    in_specs = [row(lat.shape[1], 0)]
    in_specs += [row(LANES, 0)] * len(tabs_m)
    in_specs += [full(gq), full(gkv), full(wq_pad), full(wkv)]
    kn_w = n_heads * MLA_NOPE_DIM
    out_shape = (jax.ShapeDtypeStruct((t, qk_w), BF16),
                 jax.ShapeDtypeStruct((t, kn_w), BF16),
                 jax.ShapeDtypeStruct((t, LANES), BF16),
                 jax.ShapeDtypeStruct((t, v_w), BF16))
    return pl.pallas_call(
        functools.partial(_latent_kernel, n_heads=n_heads),
        out_shape=out_shape,
        grid=(t // tm,),
        in_specs=in_specs,
        out_specs=(row(qk_w, 0), row(kn_w, 0), row(LANES, 0), row(v_w, 0)),
        compiler_params=_cparams(("arbitrary",)),
        name="latent",
    )(lat, *tabs_m, gq, gkv, wq_pad, wkv)


def _numerators(s_ref, e_ref, rows=16):
    for r in range(0, s_ref.shape[0], rows):
        s = s_ref[r:r + rows, :]
        e_ref[r:r + rows, :] = jnp.exp2(s - jnp.max(s, axis=-1, keepdims=True)).astype(BF16)


def _weighted_values(e_ref, v1):
    o = _dot(e_ref[...], v1)
    dv = v1.shape[1] // 2
    return o[:, :dv], o[:, dv:dv + 1]


SCORE_BUFS = 4
NUMER_LAG = 1


def _attn_step(q_ref, keys, v_ref, o_ref, s_refs, e_refs, make_lhs, finish):
    n, n_s = len(e_refs), len(s_refs)
    tq = o_ref.shape[1] // n
    assert n_s >= NUMER_LAG + 2 and n % n_s == 0

    @pl.when(pl.program_id(0) == 0)
    def _():
        for t in range(NUMER_LAG):
            s_refs[(t - NUMER_LAG) % n_s][...] = jnp.zeros_like(s_refs[0])
        for t in range(n - NUMER_LAG):
            e_refs[t][...] = jnp.zeros_like(e_refs[t])

    v = v_ref[0]
    v1 = jnp.concatenate([v, jnp.ones_like(v)], axis=1)
    for k in range(n):
        rows = slice(k * tq, (k + 1) * tq)
        behind = k - NUMER_LAG
        s_refs[k % n_s][...] = _dot_nt(make_lhs(q_ref[0, rows]), keys())
        _numerators(s_refs[behind % n_s], e_refs[behind % n])
        o_ref[0, rows] = finish(*_weighted_values(e_refs[k], v1)).astype(o_ref.dtype)


def _attn_specs(b, n_heads, seq, tq, n_tiles, qk_width, v_width, q_col=0, k_col=0, v_col=0):
    rows = n_tiles * tq
    n_groups = seq // rows
    total = b * n_heads * n_groups
    assert seq % rows == 0

    def decode(g):
        return g // (n_heads * n_groups), (g // n_groups) % n_heads, g % n_groups

    cur = lambda g: decode(jnp.minimum(g, total - 1))
    prev = lambda g: decode(jnp.maximum(g - 1, 0))
    q_spec = pl.BlockSpec((1, rows, qk_width),
                          lambda g: (cur(g)[0], cur(g)[2], q_col + cur(g)[1]))
    k_spec = pl.BlockSpec((1, seq, qk_width), lambda g: (cur(g)[0], 0, k_col + cur(g)[1]))
    v_spec = pl.BlockSpec((1, seq, v_width), lambda g: (prev(g)[0], 0, v_col + prev(g)[1]))
    o_spec = pl.BlockSpec((1, rows, v_width), lambda g: (prev(g)[0], prev(g)[2], prev(g)[1]))
    return total + 1, q_spec, k_spec, v_spec, o_spec, cur


def _attn_scratch(rows, seq, n_tiles):
    return ([pltpu.VMEM((rows, seq), F32)] * SCORE_BUFS
            + [pltpu.VMEM((rows, seq), BF16)] * n_tiles)


def _diffattn_kernel(lq1_ref, lk1_ref, lq2_ref, lk2_ref, g_ref, c_ref, wa_ref, ba_ref,
                     q_ref, k_ref, v_ref, o_ref, mod_ref, *scratch, lambda_init):
    mod_ref[...] = _mod_block(c_ref, wa_ref, ba_ref)

    lam = (jnp.exp(jnp.sum(lq1_ref[...] * lk1_ref[...], axis=-1, keepdims=True))
           - jnp.exp(jnp.sum(lq2_ref[...] * lk2_ref[...], axis=-1, keepdims=True))
           + lambda_init)
    gain = g_ref[...] * (1.0 - lambda_init)

    def make_lhs(q):
        lane = lax.broadcasted_iota(jnp.int32, q.shape, 1)
        zero = jnp.zeros_like(q)
        return jnp.concatenate([jnp.where(lane < DIFF_HEAD_DIM, q, zero),
                                jnp.where(lane >= DIFF_HEAD_DIM, q, zero)], axis=0)

    def finish(o, l):
        tq = o.shape[0] // 2
        out = o[:tq] / l[:tq] - lam * (o[tq:] / l[tq:])
        ms = jnp.mean(out * out, axis=-1, keepdims=True)
        return out * lax.rsqrt(ms + NORM_EPS) * gain

    _attn_step(q_ref, lambda: k_ref[0], v_ref, o_ref, scratch[:SCORE_BUFS],
               scratch[SCORE_BUFS:], make_lhs, finish)


def _diffattn_call(lams, g, qk, v, c_pad, w_ada, b_ada, mod_col0, n_heads, lambda_init,
                   tq=256, n_tiles=4):
    b, s, _ = qk.shape
    steps, q_spec, k_spec, v_spec, o_spec, _ = _attn_specs(
        b, n_heads, s, tq, n_tiles, LANES, LANES, q_col=0, k_col=n_heads, v_col=0)
    d = c_pad.shape[1]
    n_mod = w_ada.shape[1] - mod_col0
    mw = LANES * pl.cdiv(n_mod // LANES, steps - 1)
    n_blk = n_mod // mw
    assert n_mod % mw == 0 and mod_col0 % mw == 0 and n_blk <= steps
    blk = lambda g: jnp.minimum(g, n_blk - 1)
    vec = lambda a: pl.BlockSpec(a.shape, lambda g: (0, 0))
    return pl.pallas_call(
        functools.partial(_diffattn_kernel, lambda_init=lambda_init),
        out_shape=(jax.ShapeDtypeStruct((b, s, n_heads * DIFF_V_DIM), BF16),
                   jax.ShapeDtypeStruct((c_pad.shape[0], n_mod), F32)),
        grid=(steps,),
        in_specs=[vec(lams[0]), vec(lams[1]), vec(lams[2]), vec(lams[3]), vec(g), vec(c_pad),
                  pl.BlockSpec((d, mw), lambda g: (0, mod_col0 // mw + blk(g))),
                  pl.BlockSpec((1, mw), lambda g: (0, mod_col0 // mw + blk(g))),
                  q_spec, k_spec, v_spec],
        out_specs=(o_spec, pl.BlockSpec((c_pad.shape[0], mw), lambda g: (0, blk(g)))),
        scratch_shapes=_attn_scratch(2 * tq, s, n_tiles),
        compiler_params=_cparams(("arbitrary",)),
        name="diffattn",
    )(*lams, g, c_pad, w_ada, b_ada, qk, qk, v)


def _mlaattn_kernel(q_ref, kn_ref, kr_ref, v_ref, o_ref, *scratch):
    keys = lambda: jnp.concatenate([kn_ref[0], kr_ref[0]], axis=1)
    _attn_step(q_ref, keys, v_ref, o_ref, scratch[:SCORE_BUFS], scratch[SCORE_BUFS:],
               lambda q: q, lambda o, l: o / l)


def _mlaattn_call(qm, kn, kr, vm, n_heads, tq=512, n_tiles=4):
    b, s, _ = qm.shape
    steps, q_spec, _, v_spec, o_spec, cur = _attn_specs(b, n_heads, s, tq, n_tiles,
                                                        MLA_QK_PAD, MLA_V_DIM)
    kn_spec = pl.BlockSpec((1, s, MLA_NOPE_DIM), lambda g: (cur(g)[0], 0, cur(g)[1]))
    kr_spec = pl.BlockSpec((1, s, kr.shape[2]), lambda g: (cur(g)[0], 0, 0))
    assert MLA_NOPE_DIM + kr.shape[2] == MLA_QK_PAD
    return pl.pallas_call(
        _mlaattn_kernel,
        out_shape=jax.ShapeDtypeStruct(vm.shape, BF16),
        grid=(steps,),
        in_specs=[q_spec, kn_spec, kr_spec, v_spec],
        out_specs=o_spec,
        scratch_shapes=_attn_scratch(tq, s, n_tiles),
        compiler_params=_cparams(("arbitrary",)),
        name="mlaattn",
    )(qm, kn, kr, vm)


def _outproj_kernel(od_ref, om_ref, w_ref, x_ref, mod_ref, o_ref, wb_ref):
    @pl.when(pl.program_id(0) == 0)
    def _():
        wb_ref[...] = w_ref[...].astype(BF16)

    kd = od_ref.shape[1]
    acc = _dot(od_ref[...], wb_ref[:kd, :])
    acc += _dot(om_ref[...], wb_ref[kd:, :])
    o_ref[...] = x_ref[...] + mod_ref[0, 0:1, :] * acc


def _outproj_call(od, om, w_out, x2, mod3, seq, tm=512):
    t, d = x2.shape
    tpb = seq // tm
    assert seq % tm == 0 and w_out.shape[0] == od.shape[1] + om.shape[1]
    return pl.pallas_call(
        _outproj_kernel,
        out_shape=jax.ShapeDtypeStruct((t, d), F32),
        grid=(t // tm,),
        in_specs=[pl.BlockSpec((tm, od.shape[1]), lambda i: (i, 0)),
                  pl.BlockSpec((tm, om.shape[1]), lambda i: (i, 0)),
                  pl.BlockSpec(w_out.shape, lambda i: (0, 0), pipeline_mode=pl.Buffered(1)),
                  pl.BlockSpec((tm, d), lambda i: (i, 0)),
                  pl.BlockSpec((1, mod3.shape[1], d), lambda i: (i // tpb, 0, 0))],
        out_specs=pl.BlockSpec((tm, d), lambda i: (i, 0)),
        scratch_shapes=[pltpu.VMEM(w_out.shape, BF16)],
        compiler_params=_cparams(("arbitrary",)),
        name="outproj",
    )(od, om, w_out, x2, mod3)


def _ffn_kernel(x_ref, mod_ref, g_ref, w1_ref, w2p_ref, w2a_ref, gf_ref, o_ref,
                h_ref, ra_ref, rb_ref, *, rows):
    j = pl.program_id(1)
    nj = pl.num_programs(1) - 1
    tf = ra_ref.shape[1]
    act = lambda u: jnp.square(jnp.maximum(u, 0.0)).astype(BF16)

    @pl.when(j == 0)
    def _():
        _norm_modulate_rows(x_ref, g_ref, mod_ref, 1, 2, h_ref, rows)
        u = _dot(h_ref[...], w1_ref[...].astype(BF16))
        ra_ref[...] = act(u[:, :tf])
        o_ref[...] = _dot(ra_ref[...], w2a_ref[...].astype(BF16))
        rb_ref[...] = act(u[:, tf:])

    @pl.when((j > 0) & (j < nj))
    def _():
        u = _dot(h_ref[...], w1_ref[...].astype(BF16))
        o_ref[...] += _dot(rb_ref[...], w2p_ref[...].astype(BF16))
        ra_ref[...] = act(u[:, :tf])
        o_ref[...] += _dot(ra_ref[...], w2a_ref[...].astype(BF16))
        rb_ref[...] = act(u[:, tf:])

    @pl.when(j == nj)
    def _():
        o_ref[...] += _dot(rb_ref[...], w2p_ref[...].astype(BF16))
        d = x_ref.shape[1]
        gate = mod_ref[0, 3:4, :]
        gf = gf_ref[...]
        resid = lambda sl: x_ref[sl, :] + gate * o_ref[sl, :]

        def sumsq(sl):
            y = resid(sl)
            return jnp.sum(y * y, axis=-1, keepdims=True)

        def finish(sl, ss):
            o_ref[sl, :] = resid(sl) * lax.rsqrt(ss * (1.0 / d) + NORM_EPS) * gf

        _two_stage_rows(x_ref.shape[0], rows, sumsq, finish)


def _ffn_call(x1, mod3, g, w1, w2, gf, seq, tm=1024, tf=256):
    t, d = x1.shape
    n_chunks = w1.shape[1] // tf
    nj = n_chunks // 2
    tpb = seq // tm
    last_a = n_chunks - 2
    return pl.pallas_call(
        functools.partial(_ffn_kernel, rows=64),
        out_shape=jax.ShapeDtypeStruct((t, d), F32),
        grid=(t // tm, nj + 1),
        in_specs=[pl.BlockSpec((tm, d), lambda i, j: (i, 0)),
                  pl.BlockSpec((1, mod3.shape[1], d), lambda i, j: (i // tpb, 0, 0)),
                  pl.BlockSpec((1, d), lambda i, j: (0, 0)),
                  pl.BlockSpec((d, 2 * tf), lambda i, j: (0, jnp.minimum(j, nj - 1))),
                  pl.BlockSpec((tf, d), lambda i, j: (jnp.maximum(2 * j - 1, 0), 0)),
                  pl.BlockSpec((tf, d), lambda i, j: (jnp.minimum(2 * j, last_a), 0)),
                  pl.BlockSpec((1, d), lambda i, j: (0, 0))],
        out_specs=pl.BlockSpec((tm, d), lambda i, j: (i, 0)),
        scratch_shapes=[pltpu.VMEM((tm, d), BF16),
                        pltpu.VMEM((tm, tf), BF16),
                        pltpu.VMEM((tm, tf), BF16)],
        compiler_params=_cparams(("arbitrary", "arbitrary")),
        name="ffn",
    )(x1, mod3, g, w1, w2, w2, gf)


def _rope_tables(positions, specs):
    inv = jnp.concatenate([1.0 / (theta ** (jnp.arange(0, rot, 2, dtype=F32) / rot))
                           for rot, theta, _ in specs])
    ang = positions.astype(F32).reshape(-1, 1) * inv[None, :]
    cos_all, sin_all = jnp.cos(ang), jnp.sin(ang)
    t = ang.shape[0]
    out, start = [], 0
    for rot, _, group in specs:
        cos, sin = cos_all[:, start:start + rot // 2], sin_all[:, start:start + rot // 2]
        start += rot // 2
        widen = lambda lo, hi, fill: jnp.concatenate(
            [lo, hi, jnp.full((t, group - rot), fill, F32)], axis=1)
        out.append((widen(cos, cos, 1.0), widen(-sin, sin, 0.0)))
    return out


def kernel(x, c, positions, w_ada, b_ada, g_norm_mix, w_in, lambda_q1, lambda_k1, lambda_q2,
           lambda_k2, g_diff_sub, g_q_a, w_q_b, g_kv_a, w_kv_b, w_out, g_norm_ffn, w_ff1, w_ff2,
           g_final):
    b, s, d = x.shape
    depth = w_ada.shape[0]
    t = b * s
    q_rank = g_q_a.shape[1]
    kv_rank = g_kv_a.shape[1]
    n_mla = w_kv_b.shape[2] // (MLA_NOPE_DIM + MLA_V_DIM)
    n_diff = (w_in.shape[2] - q_rank - kv_rank - MLA_ROPE_DIM) // (3 * DIFF_V_DIM)
    assert n_diff == n_mla

    tabs_d, tabs_m = _rope_tables(positions, ((DIFF_ROT_DIM, ROPE_THETA, DIFF_HEAD_DIM),
                                             (MLA_ROPE_DIM, MLA_ROPE_THETA, LANES)))
    c_pad = jnp.pad(c, ((0, 8 - b), (0, 0)))
    x2 = x.reshape(t, d)

    for l in range(depth):
        lambda_init = 0.8 - 0.6 * float(np.exp(-0.3 * l))
        assert w_ada.shape[2] == (MOD_MIX_ROWS + MOD_REST_ROWS) * d
        mod_mix = _mod_call(c_pad, w_ada[l], b_ada[l][None, :], MOD_MIX_ROWS * d)
        mod_mix = mod_mix[:b].reshape(b, MOD_MIX_ROWS, d)

        w_t = jnp.swapaxes(w_in[l], 0, 1)
        dqk, dv, lat = _inproj_call(x2, mod_mix, g_norm_mix[l][None, :], w_t, tabs_d,
                                    n_diff * DIFF_V_DIM, q_rank, kv_rank, s)

        wq_pad = jnp.pad(w_q_b[l].reshape(q_rank, n_mla, MLA_NOPE_DIM + MLA_ROPE_DIM),
                         ((0, 0), (0, 0), (0, MLA_QK_PAD - MLA_NOPE_DIM - MLA_ROPE_DIM))
                         ).reshape(q_rank, n_mla * MLA_QK_PAD).astype(BF16)
        qm, kn, kr, vm = _latent_call(lat, tabs_m, g_q_a[l][None, :], g_kv_a[l][None, :],
                                      wq_pad, w_kv_b[l].astype(BF16), n_mla)

        shp = lambda a: a.reshape(b, s, a.shape[1])
        lams = tuple(v[l][None, :] for v in (lambda_q1, lambda_k1, lambda_q2, lambda_k2))
        o_diff, mod_rest = _diffattn_call(lams, g_diff_sub[l][None, :], shp(dqk), shp(dv), c_pad,
                                          w_ada[l], b_ada[l][None, :], MOD_MIX_ROWS * d, n_diff,
                                          lambda_init)
        mod_rest = mod_rest[:b].reshape(b, MOD_REST_ROWS, d)
        o_mla = _mlaattn_call(shp(qm), shp(kn), shp(kr), shp(vm), n_mla)

        x2 = _outproj_call(o_diff.reshape(t, -1), o_mla.reshape(t, -1), w_out[l], x2, mod_rest, s)

        last = l == depth - 1
        assert last, "final rmsnorm is fused into the last layer's MLP kernel"
        x2 = _ffn_call(x2, mod_rest, g_norm_ffn[l][None, :], w_ff1[l], w_ff2[l],
                       g_final[None, :], s)

    return x2.reshape(b, s, d)
```

```python
import functools
import math

import jax
import jax.numpy as jnp
import numpy as np
from jax import lax
from jax.experimental import pallas as pl
from jax.experimental.pallas import tpu as pltpu

F32 = jnp.float32
BF16 = jnp.bfloat16

LANES = 128
NORM_EPS = 1e-6

DIFF_HEAD_DIM = 64
DIFF_V_DIM = 2 * DIFF_HEAD_DIM
DIFF_ROT_DIM = DIFF_HEAD_DIM // 4
ROPE_THETA = 500000.0
MLA_V_DIM = 128
MLA_NOPE_DIM = 128
MLA_ROPE_DIM = 64
MLA_ROPE_THETA = 10000.0
MLA_QK_PAD = 256

DIFF_SCALE = DIFF_HEAD_DIM ** -0.5 * math.log2(math.e)
MLA_SCALE = (MLA_NOPE_DIM + MLA_ROPE_DIM) ** -0.5 * math.log2(math.e)

MIB = 1024 * 1024
VMEM_LIMIT = 56 * MIB


def _cparams(sem, vmem_limit=VMEM_LIMIT):
    return pltpu.CompilerParams(dimension_semantics=sem, vmem_limit_bytes=vmem_limit)


def _dot(a, b):
    return jnp.dot(a, b, preferred_element_type=F32)


def _dot_nt(a, b):
    return lax.dot_general(a, b, (((1,), (1,)), ((), ())), preferred_element_type=F32)


MOD_MIX_ROWS = 2
MOD_REST_ROWS = 4


def _mod_block(c_ref, w_ref, b_ref):
    c = c_ref[...]
    sc = c / (1.0 + jnp.exp(-c))
    return _dot(sc.astype(BF16), w_ref[...].astype(BF16)) + b_ref[...]


def _mod_kernel(c_ref, w_ref, b_ref, o_ref):
    o_ref[...] = _mod_block(c_ref, w_ref, b_ref)


def _mod_call(c_pad, w_ada, b_ada, n, tn=1024):
    m, d = c_pad.shape
    return pl.pallas_call(
        _mod_kernel,
        out_shape=jax.ShapeDtypeStruct((m, n), F32),
        grid=(n // tn,),
        in_specs=[pl.BlockSpec((m, d), lambda j: (0, 0)),
                  pl.BlockSpec((d, tn), lambda j: (0, j)),
                  pl.BlockSpec((1, tn), lambda j: (0, j))],
        out_specs=pl.BlockSpec((m, tn), lambda j: (0, j)),
        compiler_params=_cparams(("arbitrary",)),
        name="mod",
    )(c_pad, w_ada, b_ada)


def _two_stage_rows(n_rows, rows, sumsq, finish):
    chunk = lambda r: pl.ds(pl.multiple_of(r * rows, rows), rows)

    def body(r, ss_prev):
        ss = sumsq(chunk(r))
        finish(chunk(r - 1), ss_prev)
        return ss

    n = n_rows // rows
    ss_last = lax.fori_loop(1, n, body, sumsq(chunk(0)))
    finish(chunk(n - 1), ss_last)


def _norm_modulate_rows(x_ref, g_ref, mod_ref, shift_row, scale_row, h_ref, rows):
    tm, d = x_ref.shape
    shift = mod_ref[0, shift_row:shift_row + 1, :]
    gs = g_ref[...] * (1.0 + mod_ref[0, scale_row:scale_row + 1, :])

    def sumsq(sl):
        x = x_ref[sl, :]
        return jnp.sum(x * x, axis=-1, keepdims=True)

    def finish(sl, ss):
        rs = lax.rsqrt(ss * (1.0 / d) + NORM_EPS)
        h_ref[sl, :] = (x_ref[sl, :] * rs * gs + shift).astype(BF16)

    _two_stage_rows(tm, rows, sumsq, finish)


def _rope_slab(x, c, s, half, group):
    g = lax.broadcasted_iota(jnp.int32, (1, LANES), 1) % group
    partner = jnp.where(g >= half, pltpu.roll(x, half, 1), pltpu.roll(x, LANES - half, 1))
    return x * c + partner * s


def _rms_rows(x, g):
    ms = jnp.mean(x * x, axis=-1, keepdims=True)
    return x * lax.rsqrt(ms + NORM_EPS) * g


def _inproj_kernel(x_ref, mod_ref, g_ref, w_ref, cd_ref, sd_ref,
                   qk_ref, v_ref, lat_ref, h_ref, r_ref, *, rows, lat_cols):
    j = pl.program_id(1)
    tn = r_ref.shape[1]

    def project():
        return _dot_nt(h_ref[...], w_ref[...].astype(BF16))

    @pl.when(j == 0)
    def _():
        _norm_modulate_rows(x_ref, g_ref, mod_ref, 0, 1, h_ref, rows)
        r_ref[...] = project()

    @pl.when((j >= 1) & (j <= 4))
    def _():
        repeat = lambda ref: jnp.concatenate([ref[...]] * (LANES // ref.shape[1]), axis=1)
        c, s = repeat(cd_ref), repeat(sd_ref)
        scale = jnp.where(j <= 2, DIFF_SCALE, 1.0)
        for k in range(tn // LANES):
            sl = slice(k * LANES, (k + 1) * LANES)
            qk_ref[:, sl] = (_rope_slab(r_ref[:, sl], c, s, DIFF_ROT_DIM // 2, cd_ref.shape[1])
                             * scale).astype(BF16)
        r_ref[...] = project()

    @pl.when(j == 5)
    def _():
        v_ref[:, :tn] = r_ref[...].astype(BF16)
        r_ref[...] = project()

    @pl.when(j == 6)
    def _():
        v_ref[:, tn:] = r_ref[...].astype(BF16)
        r_ref[...] = project()

    @pl.when(j == 7)
    def _():
        lat_ref[:, :tn] = r_ref[...]
        lat_ref[:, tn:tn + lat_cols] = project()[:, :lat_cols]
        pad = lat_ref.shape[1] - (tn + lat_cols)
        lat_ref[:, tn + lat_cols:] = jnp.zeros((lat_ref.shape[0], pad), F32)


def _inproj_call(x2, mod3, g, w_t, tabs_d, dw, q_rank, kv_rank, seq, tm=1024):
    t, d = x2.shape
    tpb = seq // tm
    lat_cols = kv_rank + MLA_ROPE_DIM
    lat_w = q_rank + kv_rank + LANES
    tn = dw // 2
    assert dw % (2 * LANES) == 0 and seq % tm == 0 and q_rank == tn and lat_cols <= tn
    assert w_t.shape[0] == 3 * dw + q_rank + lat_cols
    tab = pl.BlockSpec((tm, tabs_d[0].shape[1]), lambda i, j: (i, 0))
    return pl.pallas_call(
        functools.partial(_inproj_kernel, rows=64, lat_cols=lat_cols),
        out_shape=(jax.ShapeDtypeStruct((t, 2 * dw), BF16),
                   jax.ShapeDtypeStruct((t, dw), BF16),
                   jax.ShapeDtypeStruct((t, lat_w), F32)),
        grid=(t // tm, 8),
        in_specs=[pl.BlockSpec((tm, d), lambda i, j: (i, 0)),
                  pl.BlockSpec((1, mod3.shape[1], d), lambda i, j: (i // tpb, 0, 0)),
                  pl.BlockSpec((1, d), lambda i, j: (0, 0)),
                  pl.BlockSpec((tn, d), lambda i, j: (j, 0)),
                  tab, tab],
        out_specs=(pl.BlockSpec((tm, tn), lambda i, j: (i, jnp.clip(j - 1, 0, 3))),
                   pl.BlockSpec((tm, dw), lambda i, j: (i, 0)),
                   pl.BlockSpec((tm, lat_w), lambda i, j: (i, 0))),
        scratch_shapes=[pltpu.VMEM((tm, d), BF16), pltpu.VMEM((tm, tn), F32)],
        compiler_params=_cparams(("arbitrary", "arbitrary"), 50 * MIB),
        name="inproj",
    )(x2, mod3, g, w_t, *tabs_d)


def _latent_kernel(lat_ref, cm_ref, sm_ref,
                   gq_ref, gkv_ref, wq_ref, wkv_ref, qmo_ref, kno_ref, kro_ref, vmo_ref,
                   *, n_heads):
    half_m = MLA_ROPE_DIM // 2
    rest = (cm_ref.shape[0], LANES - cm_ref.shape[1])
    cm = jnp.concatenate([cm_ref[...], jnp.ones(rest, F32)], axis=1)
    sm = jnp.concatenate([sm_ref[...], jnp.zeros(rest, F32)], axis=1)
    rope_m = lambda x: _rope_slab(x, cm, sm, half_m, LANES)
    q_rank, kv_rank = gq_ref.shape[1], gkv_ref.shape[1]

    q = _dot(_rms_rows(lat_ref[:, :q_rank], gq_ref[...]).astype(BF16), wq_ref[...])
    kv = _dot(_rms_rows(lat_ref[:, q_rank:q_rank + kv_rank], gkv_ref[...]).astype(BF16),
              wkv_ref[...])
    kro_ref[...] = rope_m(lat_ref[:, q_rank + kv_rank:]).astype(BF16)

    for h in range(n_heads):
        base = h * MLA_QK_PAD
        nope = slice(base, base + LANES)
        rope = slice(base + LANES, base + 2 * LANES)
        head = slice(h * LANES, (h + 1) * LANES)
        qmo_ref[:, nope] = (q[:, nope] * MLA_SCALE).astype(BF16)
        qmo_ref[:, rope] = (rope_m(q[:, rope]) * MLA_SCALE).astype(BF16)
        kno_ref[:, head] = kv[:, nope].astype(BF16)
        vmo_ref[:, head] = kv[:, rope].astype(BF16)


def _latent_call(lat, tabs_m, gq, gkv, wq_pad, wkv, n_heads, tm=1024):
    t = lat.shape[0]
    q_rank = gq.shape[1]
    kv_rank = gkv.shape[1]
    qk_w = n_heads * MLA_QK_PAD
    v_w = n_heads * MLA_V_DIM
    assert q_rank % LANES == 0 and kv_rank % LANES == 0
    assert lat.shape[1] == q_rank + kv_rank + LANES
    row = lambda w, c: pl.BlockSpec((tm, w), lambda i, c=c: (i, c))
    full = lambda a: pl.BlockSpec(a.shape, lambda i: (0, 0))
    in_specs = [row(lat.shape[1], 0)]
    in_specs += [row(tabs_m[0].shape[1], 0)] * len(tabs_m)
    in_specs += [full(gq), full(gkv), full(wq_pad), full(wkv)]
    kn_w = n_heads * MLA_NOPE_DIM
    out_shape = (jax.ShapeDtypeStruct((t, qk_w), BF16),
                 jax.ShapeDtypeStruct((t, kn_w), BF16),
                 jax.ShapeDtypeStruct((t, LANES), BF16),
                 jax.ShapeDtypeStruct((t, v_w), BF16))
    return pl.pallas_call(
        functools.partial(_latent_kernel, n_heads=n_heads),
        out_shape=out_shape,
        grid=(t // tm,),
        in_specs=in_specs,
        out_specs=(row(qk_w, 0), row(kn_w, 0), row(LANES, 0), row(v_w, 0)),
        compiler_params=_cparams(("arbitrary",)),
        name="latent",
    )(lat, *tabs_m, gq, gkv, wq_pad, wkv)


def _numerators(s_ref, e_ref, rows=16):
    for r in range(0, s_ref.shape[0], rows):
        s = s_ref[r:r + rows, :]
        e_ref[r:r + rows, :] = jnp.exp2(s - jnp.max(s, axis=-1, keepdims=True)).astype(BF16)


def _weighted_values(e_ref, v1):
    o = _dot(e_ref[...], v1)
    dv = v1.shape[1] // 2
    return o[:, :dv], o[:, dv:dv + 1]


SCORE_BUFS = 4
NUMER_LAG = 1


def _attn_step(q_ref, keys, v_ref, o_ref, s_refs, e_refs, make_lhs, finish):
    n, n_s = len(e_refs), len(s_refs)
    tq = o_ref.shape[1] // n
    assert n_s >= NUMER_LAG + 2 and n % n_s == 0

    @pl.when(pl.program_id(0) == 0)
    def _():
        for t in range(NUMER_LAG):
            s_refs[(t - NUMER_LAG) % n_s][...] = jnp.zeros_like(s_refs[0])
        for t in range(n - NUMER_LAG):
            e_refs[t][...] = jnp.zeros_like(e_refs[t])

    v = v_ref[0]
    v1 = jnp.concatenate([v, jnp.ones_like(v)], axis=1)
    for k in range(n):
        rows = slice(k * tq, (k + 1) * tq)
        behind = k - NUMER_LAG
        s_refs[k % n_s][...] = _dot_nt(make_lhs(q_ref[0, rows]), keys())
        _numerators(s_refs[behind % n_s], e_refs[behind % n])
        o_ref[0, rows] = finish(*_weighted_values(e_refs[k], v1)).astype(o_ref.dtype)


def _attn_specs(b, n_heads, seq, tq, n_tiles, qk_width, v_width, q_col=0, k_col=0, v_col=0):
    rows = n_tiles * tq
    n_groups = seq // rows
    total = b * n_heads * n_groups
    assert seq % rows == 0

    def decode(g):
        return g // (n_heads * n_groups), (g // n_groups) % n_heads, g % n_groups

    cur = lambda g: decode(jnp.minimum(g, total - 1))
    prev = lambda g: decode(jnp.maximum(g - 1, 0))
    q_spec = pl.BlockSpec((1, rows, qk_width),
                          lambda g: (cur(g)[0], cur(g)[2], q_col + cur(g)[1]))
    k_spec = pl.BlockSpec((1, seq, qk_width), lambda g: (cur(g)[0], 0, k_col + cur(g)[1]))
    v_spec = pl.BlockSpec((1, seq, v_width), lambda g: (prev(g)[0], 0, v_col + prev(g)[1]))
    o_spec = pl.BlockSpec((1, rows, v_width), lambda g: (prev(g)[0], prev(g)[2], prev(g)[1]))
    return total + 1, q_spec, k_spec, v_spec, o_spec, cur


def _attn_scratch(rows, seq, n_tiles):
    return ([pltpu.VMEM((rows, seq), F32)] * SCORE_BUFS
            + [pltpu.VMEM((rows, seq), BF16)] * n_tiles)


def _diffattn_kernel(lq1_ref, lk1_ref, lq2_ref, lk2_ref, g_ref, c_ref, wa_ref, ba_ref,
                     q_ref, k_ref, v_ref, o_ref, mod_ref, *scratch, lambda_init):
    mod_ref[...] = _mod_block(c_ref, wa_ref, ba_ref)

    lam = (jnp.exp(jnp.sum(lq1_ref[...] * lk1_ref[...], axis=-1, keepdims=True))
           - jnp.exp(jnp.sum(lq2_ref[...] * lk2_ref[...], axis=-1, keepdims=True))
           + lambda_init)
    gain = g_ref[...] * (1.0 - lambda_init)

    def make_lhs(q):
        lane = lax.broadcasted_iota(jnp.int32, q.shape, 1)
        zero = jnp.zeros_like(q)
        return jnp.concatenate([jnp.where(lane < DIFF_HEAD_DIM, q, zero),
                                jnp.where(lane >= DIFF_HEAD_DIM, q, zero)], axis=0)

    def finish(o, l):
        tq = o.shape[0] // 2
        out = o[:tq] / l[:tq] - lam * (o[tq:] / l[tq:])
        ms = jnp.mean(out * out, axis=-1, keepdims=True)
        return out * lax.rsqrt(ms + NORM_EPS) * gain

    _attn_step(q_ref, lambda: k_ref[0], v_ref, o_ref, scratch[:SCORE_BUFS],
               scratch[SCORE_BUFS:], make_lhs, finish)


def _diffattn_call(lams, g, qk, v, c_pad, w_ada, b_ada, mod_col0, n_heads, lambda_init,
                   tq=256, n_tiles=4):
    b, s, _ = qk.shape
    steps, q_spec, k_spec, v_spec, o_spec, _ = _attn_specs(
        b, n_heads, s, tq, n_tiles, LANES, LANES, q_col=0, k_col=n_heads, v_col=0)
    d = c_pad.shape[1]
    n_mod = w_ada.shape[1] - mod_col0
    mw = LANES * pl.cdiv(n_mod // LANES, steps - 1)
    n_blk = n_mod // mw
    assert n_mod % mw == 0 and mod_col0 % mw == 0 and n_blk <= steps
    blk = lambda g: jnp.minimum(g, n_blk - 1)
    vec = lambda a: pl.BlockSpec(a.shape, lambda g: (0, 0))
    return pl.pallas_call(
        functools.partial(_diffattn_kernel, lambda_init=lambda_init),
        out_shape=(jax.ShapeDtypeStruct((b, s, n_heads * DIFF_V_DIM), BF16),
                   jax.ShapeDtypeStruct((c_pad.shape[0], n_mod), F32)),
        grid=(steps,),
        in_specs=[vec(lams[0]), vec(lams[1]), vec(lams[2]), vec(lams[3]), vec(g), vec(c_pad),
                  pl.BlockSpec((d, mw), lambda g: (0, mod_col0 // mw + blk(g))),
                  pl.BlockSpec((1, mw), lambda g: (0, mod_col0 // mw + blk(g))),
                  q_spec, k_spec, v_spec],
        out_specs=(o_spec, pl.BlockSpec((c_pad.shape[0], mw), lambda g: (0, blk(g)))),
        scratch_shapes=_attn_scratch(2 * tq, s, n_tiles),
        compiler_params=_cparams(("arbitrary",)),
        name="diffattn",
    )(*lams, g, c_pad, w_ada, b_ada, qk, qk, v)


def _mlaattn_kernel(q_ref, kn_ref, kr_ref, v_ref, o_ref, *scratch):
    keys = lambda: jnp.concatenate([kn_ref[0], kr_ref[0]], axis=1)
    _attn_step(q_ref, keys, v_ref, o_ref, scratch[:SCORE_BUFS], scratch[SCORE_BUFS:],
               lambda q: q, lambda o, l: o / l)


def _mlaattn_call(qm, kn, kr, vm, n_heads, tq=512, n_tiles=4):
    b, s, _ = qm.shape
    steps, q_spec, _, v_spec, o_spec, cur = _attn_specs(b, n_heads, s, tq, n_tiles,
                                                        MLA_QK_PAD, MLA_V_DIM)
    kn_spec = pl.BlockSpec((1, s, MLA_NOPE_DIM), lambda g: (cur(g)[0], 0, cur(g)[1]))
    kr_spec = pl.BlockSpec((1, s, kr.shape[2]), lambda g: (cur(g)[0], 0, 0))
    assert MLA_NOPE_DIM + kr.shape[2] == MLA_QK_PAD
    return pl.pallas_call(
        _mlaattn_kernel,
        out_shape=jax.ShapeDtypeStruct(vm.shape, BF16),
        grid=(steps,),
        in_specs=[q_spec, kn_spec, kr_spec, v_spec],
        out_specs=o_spec,
        scratch_shapes=_attn_scratch(tq, s, n_tiles),
        compiler_params=_cparams(("arbitrary",)),
        name="mlaattn",
    )(qm, kn, kr, vm)


def _outproj_kernel(od_ref, om_ref, w_ref, x_ref, mod_ref, o_ref, wb_ref):
    @pl.when(pl.program_id(0) == 0)
    def _():
        wb_ref[...] = w_ref[...].astype(BF16)

    kd = od_ref.shape[1]
    acc = _dot(od_ref[...], wb_ref[:kd, :])
    acc += _dot(om_ref[...], wb_ref[kd:, :])
    o_ref[...] = x_ref[...] + mod_ref[0, 0:1, :] * acc


def _outproj_call(od, om, w_out, x2, mod3, seq, tm=512):
    t, d = x2.shape
    tpb = seq // tm
    assert seq % tm == 0 and w_out.shape[0] == od.shape[1] + om.shape[1]
    return pl.pallas_call(
        _outproj_kernel,
        out_shape=jax.ShapeDtypeStruct((t, d), F32),
        grid=(t // tm,),
        in_specs=[pl.BlockSpec((tm, od.shape[1]), lambda i: (i, 0)),
                  pl.BlockSpec((tm, om.shape[1]), lambda i: (i, 0)),
                  pl.BlockSpec(w_out.shape, lambda i: (0, 0), pipeline_mode=pl.Buffered(1)),
                  pl.BlockSpec((tm, d), lambda i: (i, 0)),
                  pl.BlockSpec((1, mod3.shape[1], d), lambda i: (i // tpb, 0, 0))],
        out_specs=pl.BlockSpec((tm, d), lambda i: (i, 0)),
        scratch_shapes=[pltpu.VMEM(w_out.shape, BF16)],
        compiler_params=_cparams(("arbitrary",)),
        name="outproj",
    )(od, om, w_out, x2, mod3)


def _ffn_kernel(x_ref, mod_ref, g_ref, w1_ref, w2p_ref, w2a_ref, gf_ref, o_ref,
                h_ref, ra_ref, rb_ref, *, rows):
    j = pl.program_id(1)
    nj = pl.num_programs(1) - 1
    tf = ra_ref.shape[1]
    act = lambda u: jnp.square(jnp.maximum(u, 0.0)).astype(BF16)

    @pl.when(j == 0)
    def _():
        _norm_modulate_rows(x_ref, g_ref, mod_ref, 1, 2, h_ref, rows)
        u = _dot(h_ref[...], w1_ref[...].astype(BF16))
        ra_ref[...] = act(u[:, :tf])
        o_ref[...] = _dot(ra_ref[...], w2a_ref[...].astype(BF16))
        rb_ref[...] = act(u[:, tf:])

    @pl.when((j > 0) & (j < nj))
    def _():
        u = _dot(h_ref[...], w1_ref[...].astype(BF16))
        o_ref[...] += _dot(rb_ref[...], w2p_ref[...].astype(BF16))
        ra_ref[...] = act(u[:, :tf])
        o_ref[...] += _dot(ra_ref[...], w2a_ref[...].astype(BF16))
        rb_ref[...] = act(u[:, tf:])

    @pl.when(j == nj)
    def _():
        o_ref[...] += _dot(rb_ref[...], w2p_ref[...].astype(BF16))
        d = x_ref.shape[1]
        gate = mod_ref[0, 3:4, :]
        gf = gf_ref[...]
        resid = lambda sl: x_ref[sl, :] + gate * o_ref[sl, :]

        def sumsq(sl):
            y = resid(sl)
            return jnp.sum(y * y, axis=-1, keepdims=True)

        def finish(sl, ss):
            o_ref[sl, :] = resid(sl) * lax.rsqrt(ss * (1.0 / d) + NORM_EPS) * gf

        _two_stage_rows(x_ref.shape[0], rows, sumsq, finish)


def _ffn_call(x1, mod3, g, w1, w2, gf, seq, tm=1024, tf=256):
    t, d = x1.shape
    n_chunks = w1.shape[1] // tf
    nj = n_chunks // 2
    tpb = seq // tm
    last_a = n_chunks - 2
    return pl.pallas_call(
        functools.partial(_ffn_kernel, rows=64),
        out_shape=jax.ShapeDtypeStruct((t, d), F32),
        grid=(t // tm, nj + 1),
        in_specs=[pl.BlockSpec((tm, d), lambda i, j: (i, 0)),
                  pl.BlockSpec((1, mod3.shape[1], d), lambda i, j: (i // tpb, 0, 0)),
                  pl.BlockSpec((1, d), lambda i, j: (0, 0)),
                  pl.BlockSpec((d, 2 * tf), lambda i, j: (0, jnp.minimum(j, nj - 1))),
                  pl.BlockSpec((tf, d), lambda i, j: (jnp.maximum(2 * j - 1, 0), 0)),
                  pl.BlockSpec((tf, d), lambda i, j: (jnp.minimum(2 * j, last_a), 0)),
                  pl.BlockSpec((1, d), lambda i, j: (0, 0))],
        out_specs=pl.BlockSpec((tm, d), lambda i, j: (i, 0)),
        scratch_shapes=[pltpu.VMEM((tm, d), BF16),
                        pltpu.VMEM((tm, tf), BF16),
                        pltpu.VMEM((tm, tf), BF16)],
        compiler_params=_cparams(("arbitrary", "arbitrary")),
        name="ffn",
    )(x1, mod3, g, w1, w2, w2, gf)


def _rope_tables(positions, specs):
    inv = jnp.concatenate([1.0 / (theta ** (jnp.arange(0, rot, 2, dtype=F32) / rot))
                           for rot, theta, _ in specs])
    ang = positions.astype(F32).reshape(-1, 1) * inv[None, :]
    cos_all, sin_all = jnp.cos(ang), jnp.sin(ang)
    t = ang.shape[0]
    out, start = [], 0
    for rot, _, group in specs:
        cos, sin = cos_all[:, start:start + rot // 2], sin_all[:, start:start + rot // 2]
        start += rot // 2
        widen = lambda lo, hi, fill: jnp.concatenate(
            [lo, hi, jnp.full((t, group - rot), fill, F32)], axis=1)
        out.append((widen(cos, cos, 1.0), widen(-sin, sin, 0.0)))
    return out


def kernel(x, c, positions, w_ada, b_ada, g_norm_mix, w_in, lambda_q1, lambda_k1, lambda_q2,
           lambda_k2, g_diff_sub, g_q_a, w_q_b, g_kv_a, w_kv_b, w_out, g_norm_ffn, w_ff1, w_ff2,
           g_final):
    b, s, d = x.shape
    depth = w_ada.shape[0]
    t = b * s
    q_rank = g_q_a.shape[1]
    kv_rank = g_kv_a.shape[1]
    n_mla = w_kv_b.shape[2] // (MLA_NOPE_DIM + MLA_V_DIM)
    n_diff = (w_in.shape[2] - q_rank - kv_rank - MLA_ROPE_DIM) // (3 * DIFF_V_DIM)
    assert n_diff == n_mla

    tabs_d, tabs_m = _rope_tables(positions, ((DIFF_ROT_DIM, ROPE_THETA, DIFF_HEAD_DIM),
                                             (MLA_ROPE_DIM, MLA_ROPE_THETA, MLA_ROPE_DIM)))
    c_pad = jnp.pad(c, ((0, 8 - b), (0, 0)))
    x2 = x.reshape(t, d)

    for l in range(depth):
        lambda_init = 0.8 - 0.6 * float(np.exp(-0.3 * l))
        assert w_ada.shape[2] == (MOD_MIX_ROWS + MOD_REST_ROWS) * d
        mod_mix = _mod_call(c_pad, w_ada[l], b_ada[l][None, :], MOD_MIX_ROWS * d)
        mod_mix = mod_mix[:b].reshape(b, MOD_MIX_ROWS, d)

        w_t = jnp.swapaxes(w_in[l], 0, 1)
        dqk, dv, lat = _inproj_call(x2, mod_mix, g_norm_mix[l][None, :], w_t, tabs_d,
                                    n_diff * DIFF_V_DIM, q_rank, kv_rank, s)

        wq_pad = jnp.pad(w_q_b[l].reshape(q_rank, n_mla, MLA_NOPE_DIM + MLA_ROPE_DIM),
                         ((0, 0), (0, 0), (0, MLA_QK_PAD - MLA_NOPE_DIM - MLA_ROPE_DIM))
                         ).reshape(q_rank, n_mla * MLA_QK_PAD).astype(BF16)
        qm, kn, kr, vm = _latent_call(lat, tabs_m, g_q_a[l][None, :], g_kv_a[l][None, :],
                                      wq_pad, w_kv_b[l].astype(BF16), n_mla)

        shp = lambda a: a.reshape(b, s, a.shape[1])
        lams = tuple(v[l][None, :] for v in (lambda_q1, lambda_k1, lambda_q2, lambda_k2))
        o_diff, mod_rest = _diffattn_call(lams, g_diff_sub[l][None, :], shp(dqk), shp(dv), c_pad,
                                          w_ada[l], b_ada[l][None, :], MOD_MIX_ROWS * d, n_diff,
                                          lambda_init)
        mod_rest = mod_rest[:b].reshape(b, MOD_REST_ROWS, d)
        o_mla = _mlaattn_call(shp(qm), shp(kn), shp(kr), shp(vm), n_mla)

        x2 = _outproj_call(o_diff.reshape(t, -1), o_mla.reshape(t, -1), w_out[l], x2, mod_rest, s)

        last = l == depth - 1
        assert last, "final rmsnorm is fused into the last layer's MLP kernel"
        x2 = _ffn_call(x2, mod_rest, g_norm_ffn[l][None, :], w_ff1[l], w_ff2[l],
                       g_final[None, :], s)

    return x2.reshape(b, s, d)
```

```python
import functools
import math

import jax
import jax.numpy as jnp
import numpy as np
from jax import lax
from jax.experimental import pallas as pl
from jax.experimental.pallas import tpu as pltpu

F32 = jnp.float32
BF16 = jnp.bfloat16

LANES = 128
NORM_EPS = 1e-6

DIFF_HEAD_DIM = 64
DIFF_V_DIM = 2 * DIFF_HEAD_DIM
DIFF_ROT_DIM = DIFF_HEAD_DIM // 4
ROPE_THETA = 500000.0
MLA_V_DIM = 128
MLA_NOPE_DIM = 128
MLA_ROPE_DIM = 64
MLA_ROPE_THETA = 10000.0
MLA_QK_PAD = 256

DIFF_SCALE = DIFF_HEAD_DIM ** -0.5 * math.log2(math.e)
MLA_SCALE = (MLA_NOPE_DIM + MLA_ROPE_DIM) ** -0.5 * math.log2(math.e)

MIB = 1024 * 1024
VMEM_LIMIT = 56 * MIB


def _cparams(sem, vmem_limit=VMEM_LIMIT):
    return pltpu.CompilerParams(dimension_semantics=sem, vmem_limit_bytes=vmem_limit)


def _dot(a, b):
    return jnp.dot(a, b, preferred_element_type=F32)


def _dot_nt(a, b):
    return lax.dot_general(a, b, (((1,), (1,)), ((), ())), preferred_element_type=F32)


MOD_MIX_ROWS = 2
MOD_REST_ROWS = 4


def _mod_block(c_ref, w_ref, b_ref):
    c = c_ref[...]
    sc = c / (1.0 + jnp.exp(-c))
    return _dot(sc.astype(BF16), w_ref[...].astype(BF16)) + b_ref[...]


def _mod_kernel(c_ref, w_ref, b_ref, o_ref):
    o_ref[...] = _mod_block(c_ref, w_ref, b_ref)


def _mod_call(c_pad, w_ada, b_ada, n, tn=1024):
    m, d = c_pad.shape
    return pl.pallas_call(
        _mod_kernel,
        out_shape=jax.ShapeDtypeStruct((m, n), F32),
        grid=(n // tn,),
        in_specs=[pl.BlockSpec((m, d), lambda j: (0, 0)),
                  pl.BlockSpec((d, tn), lambda j: (0, j)),
                  pl.BlockSpec((1, tn), lambda j: (0, j))],
        out_specs=pl.BlockSpec((m, tn), lambda j: (0, j)),
        compiler_params=_cparams(("arbitrary",), 24 * MIB),
        name="mod",
    )(c_pad, w_ada, b_ada)


def _two_stage_rows(n_rows, rows, sumsq, finish):
    chunk = lambda r: pl.ds(pl.multiple_of(r * rows, rows), rows)

    def body(r, ss_prev):
        ss = sumsq(chunk(r))
        finish(chunk(r - 1), ss_prev)
        return ss

    n = n_rows // rows
    ss_last = lax.fori_loop(1, n, body, sumsq(chunk(0)))
    finish(chunk(n - 1), ss_last)


def _norm_modulate_rows(x_ref, g_ref, mod_ref, shift_row, scale_row, h_ref, rows):
    tm, d = x_ref.shape
    shift = mod_ref[0, shift_row:shift_row + 1, :]
    gs = g_ref[...] * (1.0 + mod_ref[0, scale_row:scale_row + 1, :])

    def sumsq(sl):
        x = x_ref[sl, :]
        return jnp.sum(x * x, axis=-1, keepdims=True)

    def finish(sl, ss):
        rs = lax.rsqrt(ss * (1.0 / d) + NORM_EPS)
        h_ref[sl, :] = (x_ref[sl, :] * rs * gs + shift).astype(BF16)

    _two_stage_rows(tm, rows, sumsq, finish)


def _rope_slab(x, c, s, half, group):
    g = lax.broadcasted_iota(jnp.int32, (1, LANES), 1) % group
    partner = jnp.where(g >= half, pltpu.roll(x, half, 1), pltpu.roll(x, LANES - half, 1))
    return x * c + partner * s


def _rms_rows(x, g):
    ms = jnp.mean(x * x, axis=-1, keepdims=True)
    return x * lax.rsqrt(ms + NORM_EPS) * g


def _inproj_kernel(x_ref, mod_ref, g_ref, w_ref, cd_ref, sd_ref,
                   qk_ref, v_ref, lat_ref, h_ref, r_ref, *, rows, lat_cols):
    j = pl.program_id(1)
    tn = r_ref.shape[1]

    def project():
        return _dot_nt(h_ref[...], w_ref[...].astype(BF16))

    @pl.when(j == 0)
    def _():
        _norm_modulate_rows(x_ref, g_ref, mod_ref, 0, 1, h_ref, rows)
        r_ref[...] = project()

    @pl.when((j >= 1) & (j <= 4))
    def _():
        repeat = lambda ref: jnp.concatenate([ref[...]] * (LANES // ref.shape[1]), axis=1)
        c, s = repeat(cd_ref), repeat(sd_ref)
        scale = jnp.where(j <= 2, DIFF_SCALE, 1.0)
        for k in range(tn // LANES):
            sl = slice(k * LANES, (k + 1) * LANES)
            qk_ref[:, sl] = (_rope_slab(r_ref[:, sl], c, s, DIFF_ROT_DIM // 2, cd_ref.shape[1])
                             * scale).astype(BF16)
        r_ref[...] = project()

    @pl.when(j == 5)
    def _():
        v_ref[:, :tn] = r_ref[...].astype(BF16)
        r_ref[...] = project()

    @pl.when(j == 6)
    def _():
        v_ref[:, tn:] = r_ref[...].astype(BF16)
        r_ref[...] = project()

    @pl.when(j == 7)
    def _():
        lat_ref[:, :tn] = r_ref[...]
        lat_ref[:, tn:tn + lat_cols] = project()[:, :lat_cols]
        pad = lat_ref.shape[1] - (tn + lat_cols)
        lat_ref[:, tn + lat_cols:] = jnp.zeros((lat_ref.shape[0], pad), F32)


def _inproj_call(x2, mod3, g, w_t, tabs_d, dw, q_rank, kv_rank, seq, tm=1024):
    t, d = x2.shape
    tpb = seq // tm
    lat_cols = kv_rank + MLA_ROPE_DIM
    lat_w = q_rank + kv_rank + LANES
    tn = dw // 2
    assert dw % (2 * LANES) == 0 and seq % tm == 0 and q_rank == tn and lat_cols <= tn
    assert w_t.shape[0] == 3 * dw + q_rank + lat_cols
    tab = pl.BlockSpec((tm, tabs_d[0].shape[1]), lambda i, j: (i, 0))
    return pl.pallas_call(
        functools.partial(_inproj_kernel, rows=64, lat_cols=lat_cols),
        out_shape=(jax.ShapeDtypeStruct((t, 2 * dw), BF16),
                   jax.ShapeDtypeStruct((t, dw), BF16),
                   jax.ShapeDtypeStruct((t, lat_w), F32)),
        grid=(t // tm, 8),
        in_specs=[pl.BlockSpec((tm, d), lambda i, j: (i, 0)),
                  pl.BlockSpec((1, mod3.shape[1], d), lambda i, j: (i // tpb, 0, 0)),
                  pl.BlockSpec((1, d), lambda i, j: (0, 0)),
                  pl.BlockSpec((tn, d), lambda i, j: (j, 0)),
                  tab, tab],
        out_specs=(pl.BlockSpec((tm, tn), lambda i, j: (i, jnp.clip(j - 1, 0, 3))),
                   pl.BlockSpec((tm, dw), lambda i, j: (i, 0)),
                   pl.BlockSpec((tm, lat_w), lambda i, j: (i, 0))),
        scratch_shapes=[pltpu.VMEM((tm, d), BF16), pltpu.VMEM((tm, tn), F32)],
        compiler_params=_cparams(("arbitrary", "arbitrary"), 50 * MIB),
        name="inproj",
    )(x2, mod3, g, w_t, *tabs_d)


def _latent_kernel(lat_ref, cm_ref, sm_ref,
                   gq_ref, gkv_ref, wq_ref, wkv_ref, qmo_ref, kno_ref, kro_ref, vmo_ref,
                   *, n_heads):
    half_m = MLA_ROPE_DIM // 2
    rope_m = lambda x: _rope_slab(x, cm_ref[...], sm_ref[...], half_m, LANES)
    q_rank, kv_rank = gq_ref.shape[1], gkv_ref.shape[1]

    q = _dot(_rms_rows(lat_ref[:, :q_rank], gq_ref[...]).astype(BF16), wq_ref[...])
    kv = _dot(_rms_rows(lat_ref[:, q_rank:q_rank + kv_rank], gkv_ref[...]).astype(BF16),
              wkv_ref[...])
    kro_ref[...] = rope_m(lat_ref[:, q_rank + kv_rank:]).astype(BF16)

    for h in range(n_heads):
        base = h * MLA_QK_PAD
        nope = slice(base, base + LANES)
        rope = slice(base + LANES, base + 2 * LANES)
        head = slice(h * LANES, (h + 1) * LANES)
        qmo_ref[:, nope] = (q[:, nope] * MLA_SCALE).astype(BF16)
        qmo_ref[:, rope] = (rope_m(q[:, rope]) * MLA_SCALE).astype(BF16)
        kno_ref[:, head] = kv[:, nope].astype(BF16)
        vmo_ref[:, head] = kv[:, rope].astype(BF16)


def _latent_call(lat, tabs_m, gq, gkv, wq_pad, wkv, n_heads, tm=1024):
    t = lat.shape[0]
    q_rank = gq.shape[1]
    kv_rank = gkv.shape[1]
    qk_w = n_heads * MLA_QK_PAD
    v_w = n_heads * MLA_V_DIM
    assert q_rank % LANES == 0 and kv_rank % LANES == 0
    assert lat.shape[1] == q_rank + kv_rank + LANES
    row = lambda w, c: pl.BlockSpec((tm, w), lambda i, c=c: (i, c))
    full = lambda a: pl.BlockSpec(a.shape, lambda i: (0, 0))
    in_specs = [row(lat.shape[1], 0)]
    in_specs += [row(LANES, 0)] * len(tabs_m)
    in_specs += [full(gq), full(gkv), full(wq_pad), full(wkv)]
    kn_w = n_heads * MLA_NOPE_DIM
    out_shape = (jax.ShapeDtypeStruct((t, qk_w), BF16),
                 jax.ShapeDtypeStruct((t, kn_w), BF16),
                 jax.ShapeDtypeStruct((t, LANES), BF16),
                 jax.ShapeDtypeStruct((t, v_w), BF16))
    return pl.pallas_call(
        functools.partial(_latent_kernel, n_heads=n_heads),
        out_shape=out_shape,
        grid=(t // tm,),
        in_specs=in_specs,
        out_specs=(row(qk_w, 0), row(kn_w, 0), row(LANES, 0), row(v_w, 0)),
        compiler_params=_cparams(("arbitrary",), 40 * MIB),
        name="latent",
    )(lat, *tabs_m, gq, gkv, wq_pad, wkv)


def _numerators(s_ref, e_ref, rows=16):
    for r in range(0, s_ref.shape[0], rows):
        s = s_ref[r:r + rows, :]
        e_ref[r:r + rows, :] = jnp.exp2(s - jnp.max(s, axis=-1, keepdims=True)).astype(BF16)


def _weighted_values(e_ref, v1):
    o = _dot(e_ref[...], v1)
    dv = v1.shape[1] // 2
    return o[:, :dv], o[:, dv:dv + 1]


SCORE_BUFS = 4
NUMER_LAG = 1


def _attn_step(q_ref, keys, v_ref, o_ref, s_refs, e_refs, make_lhs, finish):
    n, n_s = len(e_refs), len(s_refs)
    tq = o_ref.shape[1] // n
    assert n_s >= NUMER_LAG + 2 and n % n_s == 0

    @pl.when(pl.program_id(0) == 0)
    def _():
        for t in range(NUMER_LAG):
            s_refs[(t - NUMER_LAG) % n_s][...] = jnp.zeros_like(s_refs[0])
        for t in range(n - NUMER_LAG):
            e_refs[t][...] = jnp.zeros_like(e_refs[t])

    v = v_ref[0]
    v1 = jnp.concatenate([v, jnp.ones_like(v)], axis=1)
    for k in range(n):
        rows = slice(k * tq, (k + 1) * tq)
        behind = k - NUMER_LAG
        s_refs[k % n_s][...] = _dot_nt(make_lhs(q_ref[0, rows]), keys())
        _numerators(s_refs[behind % n_s], e_refs[behind % n])
        o_ref[0, rows] = finish(*_weighted_values(e_refs[k], v1)).astype(o_ref.dtype)


def _attn_specs(b, n_heads, seq, tq, n_tiles, qk_width, v_width, q_col=0, k_col=0, v_col=0):
    rows = n_tiles * tq
    n_groups = seq // rows
    total = b * n_heads * n_groups
    assert seq % rows == 0

    def decode(g):
        return g // (n_heads * n_groups), (g // n_groups) % n_heads, g % n_groups

    cur = lambda g: decode(jnp.minimum(g, total - 1))
    prev = lambda g: decode(jnp.maximum(g - 1, 0))
    q_spec = pl.BlockSpec((1, rows, qk_width),
                          lambda g: (cur(g)[0], cur(g)[2], q_col + cur(g)[1]))
    k_spec = pl.BlockSpec((1, seq, qk_width), lambda g: (cur(g)[0], 0, k_col + cur(g)[1]))
    v_spec = pl.BlockSpec((1, seq, v_width), lambda g: (prev(g)[0], 0, v_col + prev(g)[1]))
    o_spec = pl.BlockSpec((1, rows, v_width), lambda g: (prev(g)[0], prev(g)[2], prev(g)[1]))
    return total + 1, q_spec, k_spec, v_spec, o_spec, cur


def _attn_scratch(rows, seq, n_tiles):
    return ([pltpu.VMEM((rows, seq), F32)] * SCORE_BUFS
            + [pltpu.VMEM((rows, seq), BF16)] * n_tiles)


def _diffattn_kernel(lq1_ref, lk1_ref, lq2_ref, lk2_ref, g_ref, c_ref, wa_ref, ba_ref,
                     q_ref, k_ref, v_ref, o_ref, mod_ref, *scratch, lambda_init):
    mod_ref[...] = _mod_block(c_ref, wa_ref, ba_ref)

    lam = (jnp.exp(jnp.sum(lq1_ref[...] * lk1_ref[...], axis=-1, keepdims=True))
           - jnp.exp(jnp.sum(lq2_ref[...] * lk2_ref[...], axis=-1, keepdims=True))
           + lambda_init)
    gain = g_ref[...] * (1.0 - lambda_init)

    def make_lhs(q):
        lane = lax.broadcasted_iota(jnp.int32, q.shape, 1)
        zero = jnp.zeros_like(q)
        return jnp.concatenate([jnp.where(lane < DIFF_HEAD_DIM, q, zero),
                                jnp.where(lane >= DIFF_HEAD_DIM, q, zero)], axis=0)

    def finish(o, l):
        tq = o.shape[0] // 2
        out = o[:tq] / l[:tq] - lam * (o[tq:] / l[tq:])
        ms = jnp.mean(out * out, axis=-1, keepdims=True)
        return out * lax.rsqrt(ms + NORM_EPS) * gain

    _attn_step(q_ref, lambda: k_ref[0], v_ref, o_ref, scratch[:SCORE_BUFS],
               scratch[SCORE_BUFS:], make_lhs, finish)


def _diffattn_call(lams, g, qk, v, c_pad, w_ada, b_ada, mod_col0, n_heads, lambda_init,
                   tq=256, n_tiles=4):
    b, s, _ = qk.shape
    steps, q_spec, k_spec, v_spec, o_spec, _ = _attn_specs(
        b, n_heads, s, tq, n_tiles, LANES, LANES, q_col=0, k_col=n_heads, v_col=0)
    d = c_pad.shape[1]
    n_mod = w_ada.shape[1] - mod_col0
    mw = LANES * pl.cdiv(n_mod // LANES, steps - 1)
    n_blk = n_mod // mw
    assert n_mod % mw == 0 and mod_col0 % mw == 0 and n_blk <= steps
    blk = lambda g: jnp.minimum(g, n_blk - 1)
    vec = lambda a: pl.BlockSpec(a.shape, lambda g: (0, 0))
    return pl.pallas_call(
        functools.partial(_diffattn_kernel, lambda_init=lambda_init),
        out_shape=(jax.ShapeDtypeStruct((b, s, n_heads * DIFF_V_DIM), BF16),
                   jax.ShapeDtypeStruct((c_pad.shape[0], n_mod), F32)),
        grid=(steps,),
        in_specs=[vec(lams[0]), vec(lams[1]), vec(lams[2]), vec(lams[3]), vec(g), vec(c_pad),
                  pl.BlockSpec((d, mw), lambda g: (0, mod_col0 // mw + blk(g))),
                  pl.BlockSpec((1, mw), lambda g: (0, mod_col0 // mw + blk(g))),
                  q_spec, k_spec, v_spec],
        out_specs=(o_spec, pl.BlockSpec((c_pad.shape[0], mw), lambda g: (0, blk(g)))),
        scratch_shapes=_attn_scratch(2 * tq, s, n_tiles),
        compiler_params=_cparams(("arbitrary",), 32 * MIB),
        name="diffattn",
    )(*lams, g, c_pad, w_ada, b_ada, qk, qk, v)


def _mlaattn_kernel(q_ref, kn_ref, kr_ref, v_ref, o_ref, *scratch):
    keys = lambda: jnp.concatenate([kn_ref[0], kr_ref[0]], axis=1)
    _attn_step(q_ref, keys, v_ref, o_ref, scratch[:SCORE_BUFS], scratch[SCORE_BUFS:],
               lambda q: q, lambda o, l: o / l)


def _mlaattn_call(qm, kn, kr, vm, n_heads, tq=512, n_tiles=4):
    b, s, _ = qm.shape
    steps, q_spec, _, v_spec, o_spec, cur = _attn_specs(b, n_heads, s, tq, n_tiles,
                                                        MLA_QK_PAD, MLA_V_DIM)
    kn_spec = pl.BlockSpec((1, s, MLA_NOPE_DIM), lambda g: (cur(g)[0], 0, cur(g)[1]))
    kr_spec = pl.BlockSpec((1, s, kr.shape[2]), lambda g: (cur(g)[0], 0, 0))
    assert MLA_NOPE_DIM + kr.shape[2] == MLA_QK_PAD
    return pl.pallas_call(
        _mlaattn_kernel,
        out_shape=jax.ShapeDtypeStruct(vm.shape, BF16),
        grid=(steps,),
        in_specs=[q_spec, kn_spec, kr_spec, v_spec],
        out_specs=o_spec,
        scratch_shapes=_attn_scratch(tq, s, n_tiles),
        compiler_params=_cparams(("arbitrary",), 32 * MIB),
        name="mlaattn",
    )(qm, kn, kr, vm)


def _outproj_kernel(od_ref, om_ref, w_ref, x_ref, mod_ref, o_ref, wb_ref):
    @pl.when(pl.program_id(0) == 0)
    def _():
        wb_ref[...] = w_ref[...].astype(BF16)

    kd = od_ref.shape[1]
    acc = _dot(od_ref[...], wb_ref[:kd, :])
    acc += _dot(om_ref[...], wb_ref[kd:, :])
    o_ref[...] = x_ref[...] + mod_ref[0, 0:1, :] * acc


def _outproj_call(od, om, w_out, x2, mod3, seq, tm=512):
    t, d = x2.shape
    tpb = seq // tm
    assert seq % tm == 0 and w_out.shape[0] == od.shape[1] + om.shape[1]
    return pl.pallas_call(
        _outproj_kernel,
        out_shape=jax.ShapeDtypeStruct((t, d), F32),
        grid=(t // tm,),
        in_specs=[pl.BlockSpec((tm, od.shape[1]), lambda i: (i, 0)),
                  pl.BlockSpec((tm, om.shape[1]), lambda i: (i, 0)),
                  pl.BlockSpec(w_out.shape, lambda i: (0, 0), pipeline_mode=pl.Buffered(1)),
                  pl.BlockSpec((tm, d), lambda i: (i, 0)),
                  pl.BlockSpec((1, mod3.shape[1], d), lambda i: (i // tpb, 0, 0))],
        out_specs=pl.BlockSpec((tm, d), lambda i: (i, 0)),
        scratch_shapes=[pltpu.VMEM(w_out.shape, BF16)],
        compiler_params=_cparams(("arbitrary",)),
        name="outproj",
    )(od, om, w_out, x2, mod3)


def _ffn_kernel(x_ref, mod_ref, g_ref, w1_ref, w2p_ref, w2a_ref, gf_ref, o_ref,
                h_ref, ra_ref, rb_ref, *, rows):
    j = pl.program_id(1)
    nj = pl.num_programs(1) - 1
    tf = ra_ref.shape[1]
    act = lambda u: jnp.square(jnp.maximum(u, 0.0)).astype(BF16)

    @pl.when(j == 0)
    def _():
        _norm_modulate_rows(x_ref, g_ref, mod_ref, 1, 2, h_ref, rows)
        u = _dot(h_ref[...], w1_ref[...].astype(BF16))
        ra_ref[...] = act(u[:, :tf])
        o_ref[...] = _dot(ra_ref[...], w2a_ref[...].astype(BF16))
        rb_ref[...] = act(u[:, tf:])

    @pl.when((j > 0) & (j < nj))
    def _():
        u = _dot(h_ref[...], w1_ref[...].astype(BF16))
        o_ref[...] += _dot(rb_ref[...], w2p_ref[...].astype(BF16))
        ra_ref[...] = act(u[:, :tf])
        o_ref[...] += _dot(ra_ref[...], w2a_ref[...].astype(BF16))
        rb_ref[...] = act(u[:, tf:])

    @pl.when(j == nj)
    def _():
        o_ref[...] += _dot(rb_ref[...], w2p_ref[...].astype(BF16))
        d = x_ref.shape[1]
        gate = mod_ref[0, 3:4, :]
        gf = gf_ref[...]
        resid = lambda sl: x_ref[sl, :] + gate * o_ref[sl, :]

        def sumsq(sl):
            y = resid(sl)
            return jnp.sum(y * y, axis=-1, keepdims=True)

        def finish(sl, ss):
            o_ref[sl, :] = resid(sl) * lax.rsqrt(ss * (1.0 / d) + NORM_EPS) * gf

        _two_stage_rows(x_ref.shape[0], rows, sumsq, finish)


def _ffn_call(x1, mod3, g, w1, w2, gf, seq, tm=1024, tf=256):
    t, d = x1.shape
    n_chunks = w1.shape[1] // tf
    nj = n_chunks // 2
    tpb = seq // tm
    last_a = n_chunks - 2
    return pl.pallas_call(
        functools.partial(_ffn_kernel, rows=64),
        out_shape=jax.ShapeDtypeStruct((t, d), F32),
        grid=(t // tm, nj + 1),
        in_specs=[pl.BlockSpec((tm, d), lambda i, j: (i, 0)),
                  pl.BlockSpec((1, mod3.shape[1], d), lambda i, j: (i // tpb, 0, 0)),
                  pl.BlockSpec((1, d), lambda i, j: (0, 0)),
                  pl.BlockSpec((d, 2 * tf), lambda i, j: (0, jnp.minimum(j, nj - 1))),
                  pl.BlockSpec((tf, d), lambda i, j: (jnp.maximum(2 * j - 1, 0), 0)),
                  pl.BlockSpec((tf, d), lambda i, j: (jnp.minimum(2 * j, last_a), 0)),
                  pl.BlockSpec((1, d), lambda i, j: (0, 0))],
        out_specs=pl.BlockSpec((tm, d), lambda i, j: (i, 0)),
        scratch_shapes=[pltpu.VMEM((tm, d), BF16),
                        pltpu.VMEM((tm, tf), BF16),
                        pltpu.VMEM((tm, tf), BF16)],
        compiler_params=_cparams(("arbitrary", "arbitrary")),
        name="ffn",
    )(x1, mod3, g, w1, w2, w2, gf)


def _rope_tables(positions, specs):
    inv = jnp.concatenate([1.0 / (theta ** (jnp.arange(0, rot, 2, dtype=F32) / rot))
                           for rot, theta, _ in specs])
    ang = positions.astype(F32).reshape(-1, 1) * inv[None, :]
    cos_all, sin_all = jnp.cos(ang), jnp.sin(ang)
    t = ang.shape[0]
    out, start = [], 0
    for rot, _, group in specs:
        cos, sin = cos_all[:, start:start + rot // 2], sin_all[:, start:start + rot // 2]
        start += rot // 2
        widen = lambda lo, hi, fill: jnp.concatenate(
            [lo, hi, jnp.full((t, group - rot), fill, F32)], axis=1)
        out.append((widen(cos, cos, 1.0), widen(-sin, sin, 0.0)))
    return out


def kernel(x, c, positions, w_ada, b_ada, g_norm_mix, w_in, lambda_q1, lambda_k1, lambda_q2,
           lambda_k2, g_diff_sub, g_q_a, w_q_b, g_kv_a, w_kv_b, w_out, g_norm_ffn, w_ff1, w_ff2,
           g_final):
    b, s, d = x.shape
    depth = w_ada.shape[0]
    t = b * s
    q_rank = g_q_a.shape[1]
    kv_rank = g_kv_a.shape[1]
    n_mla = w_kv_b.shape[2] // (MLA_NOPE_DIM + MLA_V_DIM)
    n_diff = (w_in.shape[2] - q_rank - kv_rank - MLA_ROPE_DIM) // (3 * DIFF_V_DIM)
    assert n_diff == n_mla

    tabs_d, tabs_m = _rope_tables(positions, ((DIFF_ROT_DIM, ROPE_THETA, DIFF_HEAD_DIM),
                                             (MLA_ROPE_DIM, MLA_ROPE_THETA, LANES)))
    c_pad = jnp.pad(c, ((0, 8 - b), (0, 0)))
    x2 = x.reshape(t, d)

    for l in range(depth):
        lambda_init = 0.8 - 0.6 * float(np.exp(-0.3 * l))
        assert w_ada.shape[2] == (MOD_MIX_ROWS + MOD_REST_ROWS) * d
        mod_mix = _mod_call(c_pad, w_ada[l], b_ada[l][None, :], MOD_MIX_ROWS * d)
        mod_mix = mod_mix[:b].reshape(b, MOD_MIX_ROWS, d)

        w_t = jnp.swapaxes(w_in[l], 0, 1)
        dqk, dv, lat = _inproj_call(x2, mod_mix, g_norm_mix[l][None, :], w_t, tabs_d,
                                    n_diff * DIFF_V_DIM, q_rank, kv_rank, s)

        wq_pad = jnp.pad(w_q_b[l].reshape(q_rank, n_mla, MLA_NOPE_DIM + MLA_ROPE_DIM),
                         ((0, 0), (0, 0), (0, MLA_QK_PAD - MLA_NOPE_DIM - MLA_ROPE_DIM))
                         ).reshape(q_rank, n_mla * MLA_QK_PAD).astype(BF16)
        qm, kn, kr, vm = _latent_call(lat, tabs_m, g_q_a[l][None, :], g_kv_a[l][None, :],
                                      wq_pad, w_kv_b[l].astype(BF16), n_mla)

        shp = lambda a: a.reshape(b, s, a.shape[1])
        lams = tuple(v[l][None, :] for v in (lambda_q1, lambda_k1, lambda_q2, lambda_k2))
        o_diff, mod_rest = _diffattn_call(lams, g_diff_sub[l][None, :], shp(dqk), shp(dv), c_pad,
                                          w_ada[l], b_ada[l][None, :], MOD_MIX_ROWS * d, n_diff,
                                          lambda_init)
        mod_rest = mod_rest[:b].reshape(b, MOD_REST_ROWS, d)
        o_mla = _mlaattn_call(shp(qm), shp(kn), shp(kr), shp(vm), n_mla)

        x2 = _outproj_call(o_diff.reshape(t, -1), o_mla.reshape(t, -1), w_out[l], x2, mod_rest, s)

        last = l == depth - 1
        assert last, "final rmsnorm is fused into the last layer's MLP kernel"
        x2 = _ffn_call(x2, mod_rest, g_norm_ffn[l][None, :], w_ff1[l], w_ff2[l],
                       g_final[None, :], s)

    return x2.reshape(b, s, d)
```

```python
import functools
import math

import jax
import jax.numpy as jnp
import numpy as np
from jax import lax
from jax.experimental import pallas as pl
from jax.experimental.pallas import tpu as pltpu

F32 = jnp.float32
BF16 = jnp.bfloat16

LANES = 128
NORM_EPS = 1e-6

DIFF_HEAD_DIM = 64
DIFF_V_DIM = 2 * DIFF_HEAD_DIM
DIFF_ROT_DIM = DIFF_HEAD_DIM // 4
ROPE_THETA = 500000.0
MLA_V_DIM = 128
MLA_NOPE_DIM = 128
MLA_ROPE_DIM = 64
MLA_ROPE_THETA = 10000.0
MLA_QK_PAD = 256

DIFF_SCALE = DIFF_HEAD_DIM ** -0.5 * math.log2(math.e)
MLA_SCALE = (MLA_NOPE_DIM + MLA_ROPE_DIM) ** -0.5 * math.log2(math.e)

MIB = 1024 * 1024
VMEM_LIMIT = 56 * MIB


def _cparams(sem, vmem_limit=VMEM_LIMIT):
    return pltpu.CompilerParams(dimension_semantics=sem, vmem_limit_bytes=vmem_limit)


def _dot(a, b):
    return jnp.dot(a, b, preferred_element_type=F32)


def _dot_nt(a, b):
    return lax.dot_general(a, b, (((1,), (1,)), ((), ())), preferred_element_type=F32)


MOD_MIX_ROWS = 2
MOD_REST_ROWS = 4


def _mod_block(c_ref, w_ref, b_ref):
    c = c_ref[...]
    sc = c / (1.0 + jnp.exp(-c))
    return _dot(sc.astype(BF16), w_ref[...].astype(BF16)) + b_ref[...]


def _mod_kernel(c_ref, w_ref, b_ref, o_ref):
    o_ref[...] = _mod_block(c_ref, w_ref, b_ref)


def _mod_call(c_pad, w_ada, b_ada, n, tn=1024):
    m, d = c_pad.shape
    return pl.pallas_call(
        _mod_kernel,
        out_shape=jax.ShapeDtypeStruct((m, n), F32),
        grid=(n // tn,),
        in_specs=[pl.BlockSpec((m, d), lambda j: (0, 0)),
                  pl.BlockSpec((d, tn), lambda j: (0, j)),
                  pl.BlockSpec((1, tn), lambda j: (0, j))],
        out_specs=pl.BlockSpec((m, tn), lambda j: (0, j)),
        compiler_params=_cparams(("arbitrary",), 24 * MIB),
        name="mod",
    )(c_pad, w_ada, b_ada)


def _two_stage_rows(n_rows, rows, sumsq, finish):
    chunk = lambda r: pl.ds(pl.multiple_of(r * rows, rows), rows)

    def body(r, ss_prev):
        ss = sumsq(chunk(r))
        finish(chunk(r - 1), ss_prev)
        return ss

    n = n_rows // rows
    ss_last = lax.fori_loop(1, n, body, sumsq(chunk(0)))
    finish(chunk(n - 1), ss_last)


def _norm_modulate_rows(x_ref, g_ref, mod_ref, shift_row, scale_row, h_ref, rows):
    tm, d = x_ref.shape
    shift = mod_ref[0, shift_row:shift_row + 1, :]
    gs = g_ref[...] * (1.0 + mod_ref[0, scale_row:scale_row + 1, :])

    def sumsq(sl):
        x = x_ref[sl, :]
        return jnp.sum(x * x, axis=-1, keepdims=True)

    def finish(sl, ss):
        rs = lax.rsqrt(ss * (1.0 / d) + NORM_EPS)
        h_ref[sl, :] = (x_ref[sl, :] * rs * gs + shift).astype(BF16)

    _two_stage_rows(tm, rows, sumsq, finish)


def _rope_slab(x, c, s, half, group):
    g = lax.broadcasted_iota(jnp.int32, (1, LANES), 1) % group
    partner = jnp.where(g >= half, pltpu.roll(x, half, 1), pltpu.roll(x, LANES - half, 1))
    return x * c + partner * s


def _rms_rows(x, g):
    ms = jnp.mean(x * x, axis=-1, keepdims=True)
    return x * lax.rsqrt(ms + NORM_EPS) * g


def _inproj_kernel(x_ref, mod_ref, g_ref, w_ref, cd_ref, sd_ref,
                   qk_ref, v_ref, lat_ref, h_ref, r_ref, *, rows, lat_cols):
    j = pl.program_id(1)
    tn = r_ref.shape[1]

    def project():
        return _dot_nt(h_ref[...], w_ref[...].astype(BF16))

    @pl.when(j == 0)
    def _():
        _norm_modulate_rows(x_ref, g_ref, mod_ref, 0, 1, h_ref, rows)
        r_ref[...] = project()

    @pl.when((j >= 1) & (j <= 4))
    def _():
        repeat = lambda ref: jnp.concatenate([ref[...]] * (LANES // ref.shape[1]), axis=1)
        c, s = repeat(cd_ref), repeat(sd_ref)
        scale = jnp.where(j <= 2, DIFF_SCALE, 1.0)
        for k in range(tn // LANES):
            sl = slice(k * LANES, (k + 1) * LANES)
            qk_ref[:, sl] = (_rope_slab(r_ref[:, sl], c, s, DIFF_ROT_DIM // 2, cd_ref.shape[1])
                             * scale).astype(BF16)
        r_ref[...] = project()

    @pl.when(j == 5)
    def _():
        v_ref[:, :tn] = r_ref[...].astype(BF16)
        r_ref[...] = project()

    @pl.when(j == 6)
    def _():
        v_ref[:, tn:] = r_ref[...].astype(BF16)
        r_ref[...] = project()

    @pl.when(j == 7)
    def _():
        lat_ref[:, :tn] = r_ref[...]
        lat_ref[:, tn:tn + lat_cols] = project()[:, :lat_cols]
        pad = lat_ref.shape[1] - (tn + lat_cols)
        lat_ref[:, tn + lat_cols:] = jnp.zeros((lat_ref.shape[0], pad), F32)


def _inproj_call(x2, mod3, g, w_t, tabs_d, dw, q_rank, kv_rank, seq, tm=1024):
    t, d = x2.shape
    tpb = seq // tm
    lat_cols = kv_rank + MLA_ROPE_DIM
    lat_w = q_rank + kv_rank + LANES
    tn = dw // 2
    assert dw % (2 * LANES) == 0 and seq % tm == 0 and q_rank == tn and lat_cols <= tn
    assert w_t.shape[0] == 3 * dw + q_rank + lat_cols
    tab = pl.BlockSpec((tm, tabs_d[0].shape[1]), lambda i, j: (i, 0))
    return pl.pallas_call(
        functools.partial(_inproj_kernel, rows=64, lat_cols=lat_cols),
        out_shape=(jax.ShapeDtypeStruct((t, 2 * dw), BF16),
                   jax.ShapeDtypeStruct((t, dw), BF16),
                   jax.ShapeDtypeStruct((t, lat_w), F32)),
        grid=(t // tm, 8),
        in_specs=[pl.BlockSpec((tm, d), lambda i, j: (i, 0)),
                  pl.BlockSpec((1, mod3.shape[1], d), lambda i, j: (i // tpb, 0, 0)),
                  pl.BlockSpec((1, d), lambda i, j: (0, 0)),
                  pl.BlockSpec((tn, d), lambda i, j: (j, 0)),
                  tab, tab],
        out_specs=(pl.BlockSpec((tm, tn), lambda i, j: (i, jnp.clip(j - 1, 0, 3))),
                   pl.BlockSpec((tm, dw), lambda i, j: (i, 0)),
                   pl.BlockSpec((tm, lat_w), lambda i, j: (i, 0))),
        scratch_shapes=[pltpu.VMEM((tm, d), BF16), pltpu.VMEM((tm, tn), F32)],
        compiler_params=_cparams(("arbitrary", "arbitrary"), 50 * MIB),
        name="inproj",
    )(x2, mod3, g, w_t, *tabs_d)


def _latent_kernel(lat_ref, cm_ref, sm_ref,
                   gq_ref, gkv_ref, wq_ref, wkv_ref, qmo_ref, kno_ref, kro_ref, vmo_ref,
                   *, n_heads):
    half_m = MLA_ROPE_DIM // 2
    rope_m = lambda x: _rope_slab(x, cm_ref[...], sm_ref[...], half_m, LANES)
    q_rank, kv_rank = gq_ref.shape[1], gkv_ref.shape[1]

    q = _dot(_rms_rows(lat_ref[:, :q_rank], gq_ref[...]).astype(BF16), wq_ref[...])
    kv = _dot(_rms_rows(lat_ref[:, q_rank:q_rank + kv_rank], gkv_ref[...]).astype(BF16),
              wkv_ref[...])
    kro_ref[...] = rope_m(lat_ref[:, q_rank + kv_rank:]).astype(BF16)

    for h in range(n_heads):
        base = h * MLA_QK_PAD
        nope = slice(base, base + LANES)
        rope = slice(base + LANES, base + 2 * LANES)
        head = slice(h * LANES, (h + 1) * LANES)
        qmo_ref[:, nope] = (q[:, nope] * MLA_SCALE).astype(BF16)
        qmo_ref[:, rope] = (rope_m(q[:, rope]) * MLA_SCALE).astype(BF16)
        kno_ref[:, head] = kv[:, nope].astype(BF16)
        vmo_ref[:, head] = kv[:, rope].astype(BF16)


def _latent_call(lat, tabs_m, gq, gkv, wq_pad, wkv, n_heads, tm=1024):
    t = lat.shape[0]
    q_rank = gq.shape[1]
    kv_rank = gkv.shape[1]
    qk_w = n_heads * MLA_QK_PAD
    v_w = n_heads * MLA_V_DIM
    assert q_rank % LANES == 0 and kv_rank % LANES == 0
    assert lat.shape[1] == q_rank + kv_rank + LANES
    row = lambda w, c: pl.BlockSpec((tm, w), lambda i, c=c: (i, c))
    full = lambda a: pl.BlockSpec(a.shape, lambda i: (0, 0))
    in_specs = [row(lat.shape[1], 0)]
    in_specs += [row(LANES, 0)] * len(tabs_m)
    in_specs += [full(gq), full(gkv), full(wq_pad), full(wkv)]
    kn_w = n_heads * MLA_NOPE_DIM
    out_shape = (jax.ShapeDtypeStruct((t, qk_w), BF16),
                 jax.ShapeDtypeStruct((t, kn_w), BF16),
                 jax.ShapeDtypeStruct((t, LANES), BF16),
                 jax.ShapeDtypeStruct((t, v_w), BF16))
    return pl.pallas_call(
        functools.partial(_latent_kernel, n_heads=n_heads),
        out_shape=out_shape,
        grid=(t // tm,),
        in_specs=in_specs,
        out_specs=(row(qk_w, 0), row(kn_w, 0), row(LANES, 0), row(v_w, 0)),
        compiler_params=_cparams(("arbitrary",), 40 * MIB),
        name="latent",
    )(lat, *tabs_m, gq, gkv, wq_pad, wkv)


def _numerators(s_ref, e_ref, rows=16):
    for r in range(0, s_ref.shape[0], rows):
        s = s_ref[r:r + rows, :]
        e_ref[r:r + rows, :] = jnp.exp2(s - jnp.max(s, axis=-1, keepdims=True)).astype(BF16)


def _weighted_values(e_ref, v1):
    o = _dot(e_ref[...], v1)
    dv = v1.shape[1] // 2
    return o[:, :dv], o[:, dv:dv + 1]


SCORE_BUFS = 4
NUMER_LAG = 1


def _attn_step(q_ref, keys, v_ref, o_ref, s_refs, e_refs, make_lhs, finish):
    n, n_s = len(e_refs), len(s_refs)
    tq = o_ref.shape[1] // n
    assert n_s >= NUMER_LAG + 2 and n % n_s == 0

    @pl.when(pl.program_id(0) == 0)
    def _():
        for t in range(NUMER_LAG):
            s_refs[(t - NUMER_LAG) % n_s][...] = jnp.zeros_like(s_refs[0])
        for t in range(n - NUMER_LAG):
            e_refs[t][...] = jnp.zeros_like(e_refs[t])

    v = v_ref[0]
    v1 = jnp.concatenate([v, jnp.ones_like(v)], axis=1)
    for k in range(n):
        rows = slice(k * tq, (k + 1) * tq)
        behind = k - NUMER_LAG
        s_refs[k % n_s][...] = _dot_nt(make_lhs(q_ref[0, rows]), keys())
        _numerators(s_refs[behind % n_s], e_refs[behind % n])
        o_ref[0, rows] = finish(*_weighted_values(e_refs[k], v1)).astype(o_ref.dtype)


def _attn_specs(b, n_heads, seq, tq, n_tiles, qk_width, v_width, q_col=0, k_col=0, v_col=0):
    rows = n_tiles * tq
    n_groups = seq // rows
    total = b * n_heads * n_groups
    assert seq % rows == 0

    def decode(g):
        return g // (n_heads * n_groups), (g // n_groups) % n_heads, g % n_groups

    cur = lambda g: decode(jnp.minimum(g, total - 1))
    prev = lambda g: decode(jnp.maximum(g - 1, 0))
    q_spec = pl.BlockSpec((1, rows, qk_width),
                          lambda g: (cur(g)[0], cur(g)[2], q_col + cur(g)[1]))
    k_spec = pl.BlockSpec((1, seq, qk_width), lambda g: (cur(g)[0], 0, k_col + cur(g)[1]))
    v_spec = pl.BlockSpec((1, seq, v_width), lambda g: (prev(g)[0], 0, v_col + prev(g)[1]))
    o_spec = pl.BlockSpec((1, rows, v_width), lambda g: (prev(g)[0], prev(g)[2], prev(g)[1]))
    return total + 1, q_spec, k_spec, v_spec, o_spec, cur


def _attn_scratch(rows, seq, n_tiles):
    return ([pltpu.VMEM((rows, seq), F32)] * SCORE_BUFS
            + [pltpu.VMEM((rows, seq), BF16)] * n_tiles)


def _diffattn_kernel(lq1_ref, lk1_ref, lq2_ref, lk2_ref, g_ref, c_ref, wa_ref, ba_ref,
                     q_ref, k_ref, v_ref, o_ref, mod_ref, *scratch, lambda_init):
    mod_ref[...] = _mod_block(c_ref, wa_ref, ba_ref)

    lam = (jnp.exp(jnp.sum(lq1_ref[...] * lk1_ref[...], axis=-1, keepdims=True))
           - jnp.exp(jnp.sum(lq2_ref[...] * lk2_ref[...], axis=-1, keepdims=True))
           + lambda_init)
    gain = g_ref[...] * (1.0 - lambda_init)

    def make_lhs(q):
        lane = lax.broadcasted_iota(jnp.int32, q.shape, 1)
        zero = jnp.zeros_like(q)
        return jnp.concatenate([jnp.where(lane < DIFF_HEAD_DIM, q, zero),
                                jnp.where(lane >= DIFF_HEAD_DIM, q, zero)], axis=0)

    def finish(o, l):
        tq = o.shape[0] // 2
        out = o[:tq] / l[:tq] - lam * (o[tq:] / l[tq:])
        ms = jnp.mean(out * out, axis=-1, keepdims=True)
        return out * lax.rsqrt(ms + NORM_EPS) * gain

    _attn_step(q_ref, lambda: k_ref[0], v_ref, o_ref, scratch[:SCORE_BUFS],
               scratch[SCORE_BUFS:], make_lhs, finish)


def _diffattn_call(lams, g, qk, v, c_pad, w_ada, b_ada, mod_col0, n_heads, lambda_init,
                   tq=256, n_tiles=4):
    b, s, _ = qk.shape
    steps, q_spec, k_spec, v_spec, o_spec, _ = _attn_specs(
        b, n_heads, s, tq, n_tiles, LANES, LANES, q_col=0, k_col=n_heads, v_col=0)
    d = c_pad.shape[1]
    n_mod = w_ada.shape[1] - mod_col0
    mw = LANES * pl.cdiv(n_mod // LANES, steps - 1)
    n_blk = n_mod // mw
    assert n_mod % mw == 0 and mod_col0 % mw == 0 and n_blk <= steps
    blk = lambda g: jnp.minimum(g, n_blk - 1)
    vec = lambda a: pl.BlockSpec(a.shape, lambda g: (0, 0))
    return pl.pallas_call(
        functools.partial(_diffattn_kernel, lambda_init=lambda_init),
        out_shape=(jax.ShapeDtypeStruct((b, s, n_heads * DIFF_V_DIM), BF16),
                   jax.ShapeDtypeStruct((c_pad.shape[0], n_mod), F32)),
        grid=(steps,),
        in_specs=[vec(lams[0]), vec(lams[1]), vec(lams[2]), vec(lams[3]), vec(g), vec(c_pad),
                  pl.BlockSpec((d, mw), lambda g: (0, mod_col0 // mw + blk(g))),
                  pl.BlockSpec((1, mw), lambda g: (0, mod_col0 // mw + blk(g))),
                  q_spec, k_spec, v_spec],
        out_specs=(o_spec, pl.BlockSpec((c_pad.shape[0], mw), lambda g: (0, blk(g)))),
        scratch_shapes=_attn_scratch(2 * tq, s, n_tiles),
        compiler_params=_cparams(("arbitrary",)),
        name="diffattn",
    )(*lams, g, c_pad, w_ada, b_ada, qk, qk, v)


def _mlaattn_kernel(q_ref, kn_ref, kr_ref, v_ref, o_ref, *scratch):
    keys = lambda: jnp.concatenate([kn_ref[0], kr_ref[0]], axis=1)
    _attn_step(q_ref, keys, v_ref, o_ref, scratch[:SCORE_BUFS], scratch[SCORE_BUFS:],
               lambda q: q, lambda o, l: o / l)


def _mlaattn_call(qm, kn, kr, vm, n_heads, tq=512, n_tiles=4):
    b, s, _ = qm.shape
    steps, q_spec, _, v_spec, o_spec, cur = _attn_specs(b, n_heads, s, tq, n_tiles,
                                                        MLA_QK_PAD, MLA_V_DIM)
    kn_spec = pl.BlockSpec((1, s, MLA_NOPE_DIM), lambda g: (cur(g)[0], 0, cur(g)[1]))
    kr_spec = pl.BlockSpec((1, s, kr.shape[2]), lambda g: (cur(g)[0], 0, 0))
    assert MLA_NOPE_DIM + kr.shape[2] == MLA_QK_PAD
    return pl.pallas_call(
        _mlaattn_kernel,
        out_shape=jax.ShapeDtypeStruct(vm.shape, BF16),
        grid=(steps,),
        in_specs=[q_spec, kn_spec, kr_spec, v_spec],
        out_specs=o_spec,
        scratch_shapes=_attn_scratch(tq, s, n_tiles),
        compiler_params=_cparams(("arbitrary",)),
        name="mlaattn",
    )(qm, kn, kr, vm)


def _outproj_kernel(od_ref, om_ref, w_ref, x_ref, mod_ref, o_ref, wb_ref):
    @pl.when(pl.program_id(0) == 0)
    def _():
        wb_ref[...] = w_ref[...].astype(BF16)

    kd = od_ref.shape[1]
    acc = _dot(od_ref[...], wb_ref[:kd, :])
    acc += _dot(om_ref[...], wb_ref[kd:, :])
    o_ref[...] = x_ref[...] + mod_ref[0, 0:1, :] * acc


def _outproj_call(od, om, w_out, x2, mod3, seq, tm=512):
    t, d = x2.shape
    tpb = seq // tm
    assert seq % tm == 0 and w_out.shape[0] == od.shape[1] + om.shape[1]
    return pl.pallas_call(
        _outproj_kernel,
        out_shape=jax.ShapeDtypeStruct((t, d), F32),
        grid=(t // tm,),
        in_specs=[pl.BlockSpec((tm, od.shape[1]), lambda i: (i, 0)),
                  pl.BlockSpec((tm, om.shape[1]), lambda i: (i, 0)),
                  pl.BlockSpec(w_out.shape, lambda i: (0, 0), pipeline_mode=pl.Buffered(1)),
                  pl.BlockSpec((tm, d), lambda i: (i, 0)),
                  pl.BlockSpec((1, mod3.shape[1], d), lambda i: (i // tpb, 0, 0))],
        out_specs=pl.BlockSpec((tm, d), lambda i: (i, 0)),
        scratch_shapes=[pltpu.VMEM(w_out.shape, BF16)],
        compiler_params=_cparams(("arbitrary",)),
        name="outproj",
    )(od, om, w_out, x2, mod3)


def _ffn_kernel(x_ref, mod_ref, g_ref, w1_ref, w2p_ref, w2a_ref, gf_ref, o_ref,
                h_ref, ra_ref, rb_ref, *, rows):
    j = pl.program_id(1)
    nj = pl.num_programs(1) - 1
    tf = ra_ref.shape[1]
    act = lambda u: jnp.square(jnp.maximum(u, 0.0)).astype(BF16)

    @pl.when(j == 0)
    def _():
        _norm_modulate_rows(x_ref, g_ref, mod_ref, 1, 2, h_ref, rows)
        u = _dot(h_ref[...], w1_ref[...].astype(BF16))
        ra_ref[...] = act(u[:, :tf])
        o_ref[...] = _dot(ra_ref[...], w2a_ref[...].astype(BF16))
        rb_ref[...] = act(u[:, tf:])

    @pl.when((j > 0) & (j < nj))
    def _():
        u = _dot(h_ref[...], w1_ref[...].astype(BF16))
        o_ref[...] += _dot(rb_ref[...], w2p_ref[...].astype(BF16))
        ra_ref[...] = act(u[:, :tf])
        o_ref[...] += _dot(ra_ref[...], w2a_ref[...].astype(BF16))
        rb_ref[...] = act(u[:, tf:])

    @pl.when(j == nj)
    def _():
        o_ref[...] += _dot(rb_ref[...], w2p_ref[...].astype(BF16))
        d = x_ref.shape[1]
        gate = mod_ref[0, 3:4, :]
        gf = gf_ref[...]
        resid = lambda sl: x_ref[sl, :] + gate * o_ref[sl, :]

        def sumsq(sl):
            y = resid(sl)
            return jnp.sum(y * y, axis=-1, keepdims=True)

        def finish(sl, ss):
            o_ref[sl, :] = resid(sl) * lax.rsqrt(ss * (1.0 / d) + NORM_EPS) * gf

        _two_stage_rows(x_ref.shape[0], rows, sumsq, finish)


def _ffn_call(x1, mod3, g, w1, w2, gf, seq, tm=1024, tf=256):
    t, d = x1.shape
    n_chunks = w1.shape[1] // tf
    nj = n_chunks // 2
    tpb = seq // tm
    last_a = n_chunks - 2
    return pl.pallas_call(
        functools.partial(_ffn_kernel, rows=64),
        out_shape=jax.ShapeDtypeStruct((t, d), F32),
        grid=(t // tm, nj + 1),
        in_specs=[pl.BlockSpec((tm, d), lambda i, j: (i, 0)),
                  pl.BlockSpec((1, mod3.shape[1], d), lambda i, j: (i // tpb, 0, 0)),
                  pl.BlockSpec((1, d), lambda i, j: (0, 0)),
                  pl.BlockSpec((d, 2 * tf), lambda i, j: (0, jnp.minimum(j, nj - 1))),
                  pl.BlockSpec((tf, d), lambda i, j: (jnp.maximum(2 * j - 1, 0), 0)),
                  pl.BlockSpec((tf, d), lambda i, j: (jnp.minimum(2 * j, last_a), 0)),
                  pl.BlockSpec((1, d), lambda i, j: (0, 0))],
        out_specs=pl.BlockSpec((tm, d), lambda i, j: (i, 0)),
        scratch_shapes=[pltpu.VMEM((tm, d), BF16),
                        pltpu.VMEM((tm, tf), BF16),
                        pltpu.VMEM((tm, tf), BF16)],
        compiler_params=_cparams(("arbitrary", "arbitrary")),
        name="ffn",
    )(x1, mod3, g, w1, w2, w2, gf)


def _rope_tables(positions, specs):
    inv = jnp.concatenate([1.0 / (theta ** (jnp.arange(0, rot, 2, dtype=F32) / rot))
                           for rot, theta, _ in specs])
    ang = positions.astype(F32).reshape(-1, 1) * inv[None, :]
    cos_all, sin_all = jnp.cos(ang), jnp.sin(ang)
    t = ang.shape[0]
    out, start = [], 0
    for rot, _, group in specs:
        cos, sin = cos_all[:, start:start + rot // 2], sin_all[:, start:start + rot // 2]
        start += rot // 2
        widen = lambda lo, hi, fill: jnp.concatenate(
            [lo, hi, jnp.full((t, group - rot), fill, F32)], axis=1)
        out.append((widen(cos, cos, 1.0), widen(-sin, sin, 0.0)))
    return out


def kernel(x, c, positions, w_ada, b_ada, g_norm_mix, w_in, lambda_q1, lambda_k1, lambda_q2,
           lambda_k2, g_diff_sub, g_q_a, w_q_b, g_kv_a, w_kv_b, w_out, g_norm_ffn, w_ff1, w_ff2,
           g_final):
    b, s, d = x.shape
    depth = w_ada.shape[0]
    t = b * s
    q_rank = g_q_a.shape[1]
    kv_rank = g_kv_a.shape[1]
    n_mla = w_kv_b.shape[2] // (MLA_NOPE_DIM + MLA_V_DIM)
    n_diff = (w_in.shape[2] - q_rank - kv_rank - MLA_ROPE_DIM) // (3 * DIFF_V_DIM)
    assert n_diff == n_mla

    tabs_d, tabs_m = _rope_tables(positions, ((DIFF_ROT_DIM, ROPE_THETA, DIFF_HEAD_DIM),
                                             (MLA_ROPE_DIM, MLA_ROPE_THETA, LANES)))
    c_pad = jnp.pad(c, ((0, 8 - b), (0, 0)))
    x2 = x.reshape(t, d)

    for l in range(depth):
        lambda_init = 0.8 - 0.6 * float(np.exp(-0.3 * l))
        assert w_ada.shape[2] == (MOD_MIX_ROWS + MOD_REST_ROWS) * d
        mod_mix = _mod_call(c_pad, w_ada[l], b_ada[l][None, :], MOD_MIX_ROWS * d)
        mod_mix = mod_mix[:b].reshape(b, MOD_MIX_ROWS, d)

        w_t = jnp.swapaxes(w_in[l], 0, 1)
        dqk, dv, lat = _inproj_call(x2, mod_mix, g_norm_mix[l][None, :], w_t, tabs_d,
                                    n_diff * DIFF_V_DIM, q_rank, kv_rank, s)

        wq_pad = jnp.pad(w_q_b[l].reshape(q_rank, n_mla, MLA_NOPE_DIM + MLA_ROPE_DIM),
                         ((0, 0), (0, 0), (0, MLA_QK_PAD - MLA_NOPE_DIM - MLA_ROPE_DIM))
                         ).reshape(q_rank, n_mla * MLA_QK_PAD).astype(BF16)
        qm, kn, kr, vm = _latent_call(lat, tabs_m, g_q_a[l][None, :], g_kv_a[l][None, :],
                                      wq_pad, w_kv_b[l].astype(BF16), n_mla)

        shp = lambda a: a.reshape(b, s, a.shape[1])
        lams = tuple(v[l][None, :] for v in (lambda_q1, lambda_k1, lambda_q2, lambda_k2))
        o_diff, mod_rest = _diffattn_call(lams, g_diff_sub[l][None, :], shp(dqk), shp(dv), c_pad,
                                          w_ada[l], b_ada[l][None, :], MOD_MIX_ROWS * d, n_diff,
                                          lambda_init)
        mod_rest = mod_rest[:b].reshape(b, MOD_REST_ROWS, d)
        o_mla = _mlaattn_call(shp(qm), shp(kn), shp(kr), shp(vm), n_mla)

        x2 = _outproj_call(o_diff.reshape(t, -1), o_mla.reshape(t, -1), w_out[l], x2, mod_rest, s)

        last = l == depth - 1
        assert last, "final rmsnorm is fused into the last layer's MLP kernel"
        x2 = _ffn_call(x2, mod_rest, g_norm_ffn[l][None, :], w_ff1[l], w_ff2[l],
                       g_final[None, :], s)

    return x2.reshape(b, s, d)
```

```python
import functools
import math

import jax
import jax.numpy as jnp
import numpy as np
from jax import lax
from jax.experimental import pallas as pl
from jax.experimental.pallas import tpu as pltpu

F32 = jnp.float32
BF16 = jnp.bfloat16

LANES = 128
NORM_EPS = 1e-6

DIFF_HEAD_DIM = 64
DIFF_V_DIM = 2 * DIFF_HEAD_DIM
DIFF_ROT_DIM = DIFF_HEAD_DIM // 4
ROPE_THETA = 500000.0
MLA_V_DIM = 128
MLA_NOPE_DIM = 128
MLA_ROPE_DIM = 64
MLA_ROPE_THETA = 10000.0
MLA_QK_PAD = 256

DIFF_SCALE = DIFF_HEAD_DIM ** -0.5 * math.log2(math.e)
MLA_SCALE = (MLA_NOPE_DIM + MLA_ROPE_DIM) ** -0.5 * math.log2(math.e)

MIB = 1024 * 1024
VMEM_LIMIT = 56 * MIB


def _cparams(sem, vmem_limit=VMEM_LIMIT):
    return pltpu.CompilerParams(dimension_semantics=sem, vmem_limit_bytes=vmem_limit)


def _dot(a, b):
    return jnp.dot(a, b, preferred_element_type=F32)


def _dot_nt(a, b):
    return lax.dot_general(a, b, (((1,), (1,)), ((), ())), preferred_element_type=F32)


MOD_MIX_ROWS = 2
MOD_REST_ROWS = 4


def _mod_block(c_ref, w_ref, b_ref):
    c = c_ref[...]
    sc = c / (1.0 + jnp.exp(-c))
    return _dot(sc.astype(BF16), w_ref[...].astype(BF16)) + b_ref[...]


def _mod_kernel(c_ref, w_ref, b_ref, o_ref):
    o_ref[...] = _mod_block(c_ref, w_ref, b_ref)


def _mod_call(c_pad, w_ada, b_ada, n, tn=1024):
    m, d = c_pad.shape
    return pl.pallas_call(
        _mod_kernel,
        out_shape=jax.ShapeDtypeStruct((m, n), F32),
        grid=(n // tn,),
        in_specs=[pl.BlockSpec((m, d), lambda j: (0, 0)),
                  pl.BlockSpec((d, tn), lambda j: (0, j)),
                  pl.BlockSpec((1, tn), lambda j: (0, j))],
        out_specs=pl.BlockSpec((m, tn), lambda j: (0, j)),
        compiler_params=_cparams(("arbitrary",), 24 * MIB),
        name="mod",
    )(c_pad, w_ada, b_ada)


def _two_stage_rows(n_rows, rows, sumsq, finish):
    chunk = lambda r: pl.ds(pl.multiple_of(r * rows, rows), rows)

    def body(r, ss_prev):
        ss = sumsq(chunk(r))
        finish(chunk(r - 1), ss_prev)
        return ss

    n = n_rows // rows
    ss_last = lax.fori_loop(1, n, body, sumsq(chunk(0)))
    finish(chunk(n - 1), ss_last)


def _norm_modulate_rows(x_ref, g_ref, mod_ref, shift_row, scale_row, h_ref, rows):
    tm, d = x_ref.shape
    shift = mod_ref[0, shift_row:shift_row + 1, :]
    gs = g_ref[...] * (1.0 + mod_ref[0, scale_row:scale_row + 1, :])

    def sumsq(sl):
        x = x_ref[sl, :]
        return jnp.sum(x * x, axis=-1, keepdims=True)

    def finish(sl, ss):
        rs = lax.rsqrt(ss * (1.0 / d) + NORM_EPS)
        h_ref[sl, :] = (x_ref[sl, :] * rs * gs + shift).astype(BF16)

    _two_stage_rows(tm, rows, sumsq, finish)


def _rope_slab(x, c, s, half, group):
    g = lax.broadcasted_iota(jnp.int32, (1, LANES), 1) % group
    partner = jnp.where(g >= half, pltpu.roll(x, half, 1), pltpu.roll(x, LANES - half, 1))
    return x * c + partner * s


def _rms_rows(x, g):
    ms = jnp.mean(x * x, axis=-1, keepdims=True)
    return x * lax.rsqrt(ms + NORM_EPS) * g


def _inproj_kernel(x_ref, mod_ref, g_ref, w_ref, cd_ref, sd_ref,
                   qk_ref, v_ref, lat_ref, h_ref, r_ref, *, rows, lat_cols):
    j = pl.program_id(1)
    tn = r_ref.shape[1]

    def project():
        return _dot_nt(h_ref[...], w_ref[...].astype(BF16))

    @pl.when(j == 0)
    def _():
        _norm_modulate_rows(x_ref, g_ref, mod_ref, 0, 1, h_ref, rows)
        r_ref[...] = project()

    @pl.when((j >= 1) & (j <= 4))
    def _():
        repeat = lambda ref: jnp.concatenate([ref[...]] * (LANES // ref.shape[1]), axis=1)
        c, s = repeat(cd_ref), repeat(sd_ref)
        scale = jnp.where(j <= 2, DIFF_SCALE, 1.0)
        for k in range(tn // LANES):
            sl = slice(k * LANES, (k + 1) * LANES)
            qk_ref[:, sl] = (_rope_slab(r_ref[:, sl], c, s, DIFF_ROT_DIM // 2, cd_ref.shape[1])
                             * scale).astype(BF16)
        r_ref[...] = project()

    @pl.when(j == 5)
    def _():
        v_ref[:, :tn] = r_ref[...].astype(BF16)
        r_ref[...] = project()

    @pl.when(j == 6)
    def _():
        v_ref[:, tn:] = r_ref[...].astype(BF16)
        r_ref[...] = project()

    @pl.when(j == 7)
    def _():
        lat_ref[:, :tn] = r_ref[...]
        lat_ref[:, tn:tn + lat_cols] = project()[:, :lat_cols]
        pad = lat_ref.shape[1] - (tn + lat_cols)
        lat_ref[:, tn + lat_cols:] = jnp.zeros((lat_ref.shape[0], pad), F32)


def _inproj_call(x2, mod3, g, w_t, tabs_d, dw, q_rank, kv_rank, seq, tm=1024):
    t, d = x2.shape
    tpb = seq // tm
    lat_cols = kv_rank + MLA_ROPE_DIM
    lat_w = q_rank + kv_rank + LANES
    tn = dw // 2
    assert dw % (2 * LANES) == 0 and seq % tm == 0 and q_rank == tn and lat_cols <= tn
    assert w_t.shape[0] == 3 * dw + q_rank + lat_cols
    tab = pl.BlockSpec((tm, tabs_d[0].shape[1]), lambda i, j: (i, 0))
    return pl.pallas_call(
        functools.partial(_inproj_kernel, rows=64, lat_cols=lat_cols),
        out_shape=(jax.ShapeDtypeStruct((t, 2 * dw), BF16),
                   jax.ShapeDtypeStruct((t, dw), BF16),
                   jax.ShapeDtypeStruct((t, lat_w), F32)),
        grid=(t // tm, 8),
        in_specs=[pl.BlockSpec((tm, d), lambda i, j: (i, 0)),
                  pl.BlockSpec((1, mod3.shape[1], d), lambda i, j: (i // tpb, 0, 0)),
                  pl.BlockSpec((1, d), lambda i, j: (0, 0)),
                  pl.BlockSpec((tn, d), lambda i, j: (j, 0)),
                  tab, tab],
        out_specs=(pl.BlockSpec((tm, tn), lambda i, j: (i, jnp.clip(j - 1, 0, 3))),
                   pl.BlockSpec((tm, dw), lambda i, j: (i, 0)),
                   pl.BlockSpec((tm, lat_w), lambda i, j: (i, 0))),
        scratch_shapes=[pltpu.VMEM((tm, d), BF16), pltpu.VMEM((tm, tn), F32)],
        compiler_params=_cparams(("arbitrary", "arbitrary"), 50 * MIB),
        name="inproj",
    )(x2, mod3, g, w_t, *tabs_d)


def _latent_kernel(lat_ref, cm_ref, sm_ref,
                   gq_ref, gkv_ref, wq_ref, wkv_ref, qmo_ref, kno_ref, kro_ref, vmo_ref,
                   *, n_heads):
    half_m = MLA_ROPE_DIM // 2
    rope_m = lambda x: _rope_slab(x, cm_ref[...], sm_ref[...], half_m, LANES)
    q_rank, kv_rank = gq_ref.shape[1], gkv_ref.shape[1]

    q = _dot(_rms_rows(lat_ref[:, :q_rank], gq_ref[...]).astype(BF16), wq_ref[...])
    kv = _dot(_rms_rows(lat_ref[:, q_rank:q_rank + kv_rank], gkv_ref[...]).astype(BF16),
              wkv_ref[...])
    kro_ref[...] = rope_m(lat_ref[:, q_rank + kv_rank:]).astype(BF16)

    for h in range(n_heads):
        base = h * MLA_QK_PAD
        nope = slice(base, base + LANES)
        rope = slice(base + LANES, base + 2 * LANES)
        head = slice(h * LANES, (h + 1) * LANES)
        qmo_ref[:, nope] = (q[:, nope] * MLA_SCALE).astype(BF16)
        qmo_ref[:, rope] = (rope_m(q[:, rope]) * MLA_SCALE).astype(BF16)
        kno_ref[:, head] = kv[:, nope].astype(BF16)
        vmo_ref[:, head] = kv[:, rope].astype(BF16)


def _latent_call(lat, tabs_m, gq, gkv, wq_pad, wkv, n_heads, tm=1024):
    t = lat.shape[0]
    q_rank = gq.shape[1]
    kv_rank = gkv.shape[1]
    qk_w = n_heads * MLA_QK_PAD
    v_w = n_heads * MLA_V_DIM
    assert q_rank % LANES == 0 and kv_rank % LANES == 0
    assert lat.shape[1] == q_rank + kv_rank + LANES
    row = lambda w, c: pl.BlockSpec((tm, w), lambda i, c=c: (i, c))
    full = lambda a: pl.BlockSpec(a.shape, lambda i: (0, 0))
    in_specs = [row(lat.shape[1], 0)]
    in_specs += [row(LANES, 0)] * len(tabs_m)
    in_specs += [full(gq), full(gkv), full(wq_pad), full(wkv)]
    kn_w = n_heads * MLA_NOPE_DIM
    out_shape = (jax.ShapeDtypeStruct((t, qk_w), BF16),
                 jax.ShapeDtypeStruct((t, kn_w), BF16),
                 jax.ShapeDtypeStruct((t, LANES), BF16),
                 jax.ShapeDtypeStruct((t, v_w), BF16))
    return pl.pallas_call(
        functools.partial(_latent_kernel, n_heads=n_heads),
        out_shape=out_shape,
        grid=(t // tm,),
        in_specs=in_specs,
        out_specs=(row(qk_w, 0), row(kn_w, 0), row(LANES, 0), row(v_w, 0)),
        compiler_params=_cparams(("arbitrary",), 40 * MIB),
        name="latent",
    )(lat, *tabs_m, gq, gkv, wq_pad, wkv)


def _numerators(s_ref, e_ref, rows=16):
    for r in range(0, s_ref.shape[0], rows):
        s = s_ref[r:r + rows, :]
        e_ref[r:r + rows, :] = jnp.exp2(s - jnp.max(s, axis=-1, keepdims=True)).astype(BF16)


def _weighted_values(e_ref, v1):
    o = _dot(e_ref[...], v1)
    dv = v1.shape[1] // 2
    return o[:, :dv], o[:, dv:dv + 1]


SCORE_BUFS = 4
NUMER_LAG = 1


def _attn_step(q_ref, keys, v_ref, o_ref, s_refs, e_refs, make_lhs, finish):
    n, n_s = len(e_refs), len(s_refs)
    tq = o_ref.shape[1] // n
    assert n_s >= NUMER_LAG + 2 and n % n_s == 0

    @pl.when(pl.program_id(0) == 0)
    def _():
        for t in range(NUMER_LAG):
            s_refs[(t - NUMER_LAG) % n_s][...] = jnp.zeros_like(s_refs[0])
        for t in range(n - NUMER_LAG):
            e_refs[t][...] = jnp.zeros_like(e_refs[t])

    v = v_ref[0]
    v1 = jnp.concatenate([v, jnp.ones_like(v)], axis=1)
    for k in range(n):
        rows = slice(k * tq, (k + 1) * tq)
        behind = k - NUMER_LAG
        s_refs[k % n_s][...] = _dot_nt(make_lhs(q_ref[0, rows]), keys())
        _numerators(s_refs[behind % n_s], e_refs[behind % n])
        o_ref[0, rows] = finish(*_weighted_values(e_refs[k], v1)).astype(o_ref.dtype)


def _attn_specs(b, n_heads, seq, tq, n_tiles, qk_width, v_width, q_col=0, k_col=0, v_col=0):
    rows = n_tiles * tq
    n_groups = seq // rows
    total = b * n_heads * n_groups
    assert seq % rows == 0

    def decode(g):
        return g // (n_heads * n_groups), (g // n_groups) % n_heads, g % n_groups

    cur = lambda g: decode(jnp.minimum(g, total - 1))
    prev = lambda g: decode(jnp.maximum(g - 1, 0))
    q_spec = pl.BlockSpec((1, rows, qk_width),
                          lambda g: (cur(g)[0], cur(g)[2], q_col + cur(g)[1]))
    k_spec = pl.BlockSpec((1, seq, qk_width), lambda g: (cur(g)[0], 0, k_col + cur(g)[1]))
    v_spec = pl.BlockSpec((1, seq, v_width), lambda g: (prev(g)[0], 0, v_col + prev(g)[1]))
    o_spec = pl.BlockSpec((1, rows, v_width), lambda g: (prev(g)[0], prev(g)[2], prev(g)[1]))
    return total + 1, q_spec, k_spec, v_spec, o_spec, cur


def _attn_scratch(rows, seq, n_tiles):
    return ([pltpu.VMEM((rows, seq), F32)] * SCORE_BUFS
            + [pltpu.VMEM((rows, seq), BF16)] * n_tiles)


def _diffattn_kernel(lq1_ref, lk1_ref, lq2_ref, lk2_ref, g_ref, c_ref, wa_ref, ba_ref,
                     q_ref, k_ref, v_ref, o_ref, mod_ref, *scratch, lambda_init):
    mod_ref[...] = _mod_block(c_ref, wa_ref, ba_ref)

    lam = (jnp.exp(jnp.sum(lq1_ref[...] * lk1_ref[...], axis=-1, keepdims=True))
           - jnp.exp(jnp.sum(lq2_ref[...] * lk2_ref[...], axis=-1, keepdims=True))
           + lambda_init)
    gain = g_ref[...] * (1.0 - lambda_init)

    def make_lhs(q):
        lane = lax.broadcasted_iota(jnp.int32, q.shape, 1)
        zero = jnp.zeros_like(q)
        return jnp.concatenate([jnp.where(lane < DIFF_HEAD_DIM, q, zero),
                                jnp.where(lane >= DIFF_HEAD_DIM, q, zero)], axis=0)

    def finish(o, l):
        tq = o.shape[0] // 2
        out = o[:tq] / l[:tq] - lam * (o[tq:] / l[tq:])
        ms = jnp.mean(out * out, axis=-1, keepdims=True)
        return out * lax.rsqrt(ms + NORM_EPS) * gain

    _attn_step(q_ref, lambda: k_ref[0], v_ref, o_ref, scratch[:SCORE_BUFS],
               scratch[SCORE_BUFS:], make_lhs, finish)


def _diffattn_call(lams, g, qk, v, c_pad, w_ada, b_ada, mod_col0, n_heads, lambda_init,
                   tq=256, n_tiles=4):
    b, s, _ = qk.shape
    steps, q_spec, k_spec, v_spec, o_spec, _ = _attn_specs(
        b, n_heads, s, tq, n_tiles, LANES, LANES, q_col=0, k_col=n_heads, v_col=0)
    d = c_pad.shape[1]
    n_mod = w_ada.shape[1] - mod_col0
    mw = LANES * pl.cdiv(n_mod // LANES, steps - 1)
    n_blk = n_mod // mw
    assert n_mod % mw == 0 and mod_col0 % mw == 0 and n_blk <= steps
    blk = lambda g: jnp.minimum(g, n_blk - 1)
    vec = lambda a: pl.BlockSpec(a.shape, lambda g: (0, 0))
    return pl.pallas_call(
        functools.partial(_diffattn_kernel, lambda_init=lambda_init),
        out_shape=(jax.ShapeDtypeStruct((b, s, n_heads * DIFF_V_DIM), BF16),
                   jax.ShapeDtypeStruct((c_pad.shape[0], n_mod), F32)),
        grid=(steps,),
        in_specs=[vec(lams[0]), vec(lams[1]), vec(lams[2]), vec(lams[3]), vec(g), vec(c_pad),
                  pl.BlockSpec((d, mw), lambda g: (0, mod_col0 // mw + blk(g))),
                  pl.BlockSpec((1, mw), lambda g: (0, mod_col0 // mw + blk(g))),
                  q_spec, k_spec, v_spec],
        out_specs=(o_spec, pl.BlockSpec((c_pad.shape[0], mw), lambda g: (0, blk(g)))),
        scratch_shapes=_attn_scratch(2 * tq, s, n_tiles),
        compiler_params=_cparams(("arbitrary",)),
        name="diffattn",
    )(*lams, g, c_pad, w_ada, b_ada, qk, qk, v)


def _mlaattn_kernel(q_ref, kn_ref, kr_ref, v_ref, o_ref, *scratch):
    keys = lambda: jnp.concatenate([kn_ref[0], kr_ref[0]], axis=1)
    _attn_step(q_ref, keys, v_ref, o_ref, scratch[:SCORE_BUFS], scratch[SCORE_BUFS:],
               lambda q: q, lambda o, l: o / l)


def _mlaattn_call(qm, kn, kr, vm, n_heads, tq=512, n_tiles=4):
    b, s, _ = qm.shape
    steps, q_spec, _, v_spec, o_spec, cur = _attn_specs(b, n_heads, s, tq, n_tiles,
                                                        MLA_QK_PAD, MLA_V_DIM)
    kn_spec = pl.BlockSpec((1, s, MLA_NOPE_DIM), lambda g: (cur(g)[0], 0, cur(g)[1]))
    kr_spec = pl.BlockSpec((1, s, kr.shape[2]), lambda g: (cur(g)[0], 0, 0))
    assert MLA_NOPE_DIM + kr.shape[2] == MLA_QK_PAD
    return pl.pallas_call(
        _mlaattn_kernel,
        out_shape=jax.ShapeDtypeStruct(vm.shape, BF16),
        grid=(steps,),
        in_specs=[q_spec, kn_spec, kr_spec, v_spec],
        out_specs=o_spec,
        scratch_shapes=_attn_scratch(tq, s, n_tiles),
        compiler_params=_cparams(("arbitrary",)),
        name="mlaattn",
    )(qm, kn, kr, vm)


def _outproj_kernel(od_ref, om_ref, w_ref, x_ref, mod_ref, o_ref, wb_ref):
    @pl.when(pl.program_id(0) == 0)
    def _():
        wb_ref[...] = w_ref[...].astype(BF16)

    kd = od_ref.shape[1]
    acc = _dot(od_ref[...], wb_ref[:kd, :])
    acc += _dot(om_ref[...], wb_ref[kd:, :])
    o_ref[...] = x_ref[...] + mod_ref[0, 0:1, :] * acc


def _outproj_call(od, om, w_out, x2, mod3, seq, tm=512):
    t, d = x2.shape
    tpb = seq // tm
    assert seq % tm == 0 and w_out.shape[0] == od.shape[1] + om.shape[1]
    return pl.pallas_call(
        _outproj_kernel,
        out_shape=jax.ShapeDtypeStruct((t, d), F32),
        grid=(t // tm,),
        in_specs=[pl.BlockSpec((tm, od.shape[1]), lambda i: (i, 0)),
                  pl.BlockSpec((tm, om.shape[1]), lambda i: (i, 0)),
                  pl.BlockSpec(w_out.shape, lambda i: (0, 0), pipeline_mode=pl.Buffered(1)),
                  pl.BlockSpec((tm, d), lambda i: (i, 0)),
                  pl.BlockSpec((1, mod3.shape[1], d), lambda i: (i // tpb, 0, 0))],
        out_specs=pl.BlockSpec((tm, d), lambda i: (i, 0)),
        scratch_shapes=[pltpu.VMEM(w_out.shape, BF16)],
        compiler_params=_cparams(("arbitrary",), 50 * MIB),
        name="outproj",
    )(od, om, w_out, x2, mod3)


def _ffn_kernel(x_ref, mod_ref, g_ref, w1_ref, w2p_ref, w2a_ref, gf_ref, o_ref,
                h_ref, ra_ref, rb_ref, *, rows):
    j = pl.program_id(1)
    nj = pl.num_programs(1) - 1
    tf = ra_ref.shape[1]
    act = lambda u: jnp.square(jnp.maximum(u, 0.0)).astype(BF16)

    @pl.when(j == 0)
    def _():
        _norm_modulate_rows(x_ref, g_ref, mod_ref, 1, 2, h_ref, rows)
        u = _dot(h_ref[...], w1_ref[...].astype(BF16))
        ra_ref[...] = act(u[:, :tf])
        o_ref[...] = _dot(ra_ref[...], w2a_ref[...].astype(BF16))
        rb_ref[...] = act(u[:, tf:])

    @pl.when((j > 0) & (j < nj))
    def _():
        u = _dot(h_ref[...], w1_ref[...].astype(BF16))
        o_ref[...] += _dot(rb_ref[...], w2p_ref[...].astype(BF16))
        ra_ref[...] = act(u[:, :tf])
        o_ref[...] += _dot(ra_ref[...], w2a_ref[...].astype(BF16))
        rb_ref[...] = act(u[:, tf:])

    @pl.when(j == nj)
    def _():
        o_ref[...] += _dot(rb_ref[...], w2p_ref[...].astype(BF16))
        d = x_ref.shape[1]
        gate = mod_ref[0, 3:4, :]
        gf = gf_ref[...]
        resid = lambda sl: x_ref[sl, :] + gate * o_ref[sl, :]

        def sumsq(sl):
            y = resid(sl)
            return jnp.sum(y * y, axis=-1, keepdims=True)

        def finish(sl, ss):
            o_ref[sl, :] = resid(sl) * lax.rsqrt(ss * (1.0 / d) + NORM_EPS) * gf

        _two_stage_rows(x_ref.shape[0], rows, sumsq, finish)


def _ffn_call(x1, mod3, g, w1, w2, gf, seq, tm=1024, tf=256):
    t, d = x1.shape
    n_chunks = w1.shape[1] // tf
    nj = n_chunks // 2
    tpb = seq // tm
    last_a = n_chunks - 2
    return pl.pallas_call(
        functools.partial(_ffn_kernel, rows=64),
        out_shape=jax.ShapeDtypeStruct((t, d), F32),
        grid=(t // tm, nj + 1),
        in_specs=[pl.BlockSpec((tm, d), lambda i, j: (i, 0)),
                  pl.BlockSpec((1, mod3.shape[1], d), lambda i, j: (i // tpb, 0, 0)),
                  pl.BlockSpec((1, d), lambda i, j: (0, 0)),
                  pl.BlockSpec((d, 2 * tf), lambda i, j: (0, jnp.minimum(j, nj - 1))),
                  pl.BlockSpec((tf, d), lambda i, j: (jnp.maximum(2 * j - 1, 0), 0)),
                  pl.BlockSpec((tf, d), lambda i, j: (jnp.minimum(2 * j, last_a), 0)),
                  pl.BlockSpec((1, d), lambda i, j: (0, 0))],
        out_specs=pl.BlockSpec((tm, d), lambda i, j: (i, 0)),
        scratch_shapes=[pltpu.VMEM((tm, d), BF16),
                        pltpu.VMEM((tm, tf), BF16),
                        pltpu.VMEM((tm, tf), BF16)],
        compiler_params=_cparams(("arbitrary", "arbitrary")),
        name="ffn",
    )(x1, mod3, g, w1, w2, w2, gf)


def _rope_tables(positions, specs):
    inv = jnp.concatenate([1.0 / (theta ** (jnp.arange(0, rot, 2, dtype=F32) / rot))
                           for rot, theta, _ in specs])
    ang = positions.astype(F32).reshape(-1, 1) * inv[None, :]
    cos_all, sin_all = jnp.cos(ang), jnp.sin(ang)
    t = ang.shape[0]
    out, start = [], 0
    for rot, _, group in specs:
        cos, sin = cos_all[:, start:start + rot // 2], sin_all[:, start:start + rot // 2]
        start += rot // 2
        widen = lambda lo, hi, fill: jnp.concatenate(
            [lo, hi, jnp.full((t, group - rot), fill, F32)], axis=1)
        out.append((widen(cos, cos, 1.0), widen(-sin, sin, 0.0)))
    return out


def kernel(x, c, positions, w_ada, b_ada, g_norm_mix, w_in, lambda_q1, lambda_k1, lambda_q2,
           lambda_k2, g_diff_sub, g_q_a, w_q_b, g_kv_a, w_kv_b, w_out, g_norm_ffn, w_ff1, w_ff2,
           g_final):
    b, s, d = x.shape
    depth = w_ada.shape[0]
    t = b * s
    q_rank = g_q_a.shape[1]
    kv_rank = g_kv_a.shape[1]
    n_mla = w_kv_b.shape[2] // (MLA_NOPE_DIM + MLA_V_DIM)
    n_diff = (w_in.shape[2] - q_rank - kv_rank - MLA_ROPE_DIM) // (3 * DIFF_V_DIM)
    assert n_diff == n_mla

    tabs_d, tabs_m = _rope_tables(positions, ((DIFF_ROT_DIM, ROPE_THETA, DIFF_HEAD_DIM),
                                             (MLA_ROPE_DIM, MLA_ROPE_THETA, LANES)))
    c_pad = jnp.pad(c, ((0, 8 - b), (0, 0)))
    x2 = x.reshape(t, d)

    for l in range(depth):
        lambda_init = 0.8 - 0.6 * float(np.exp(-0.3 * l))
        assert w_ada.shape[2] == (MOD_MIX_ROWS + MOD_REST_ROWS) * d
        mod_mix = _mod_call(c_pad, w_ada[l], b_ada[l][None, :], MOD_MIX_ROWS * d)
        mod_mix = mod_mix[:b].reshape(b, MOD_MIX_ROWS, d)

        w_t = jnp.swapaxes(w_in[l], 0, 1)
        dqk, dv, lat = _inproj_call(x2, mod_mix, g_norm_mix[l][None, :], w_t, tabs_d,
                                    n_diff * DIFF_V_DIM, q_rank, kv_rank, s)

        wq_pad = jnp.pad(w_q_b[l].reshape(q_rank, n_mla, MLA_NOPE_DIM + MLA_ROPE_DIM),
                         ((0, 0), (0, 0), (0, MLA_QK_PAD - MLA_NOPE_DIM - MLA_ROPE_DIM))
                         ).reshape(q_rank, n_mla * MLA_QK_PAD).astype(BF16)
        qm, kn, kr, vm = _latent_call(lat, tabs_m, g_q_a[l][None, :], g_kv_a[l][None, :],
                                      wq_pad, w_kv_b[l].astype(BF16), n_mla)

        shp = lambda a: a.reshape(b, s, a.shape[1])
        lams = tuple(v[l][None, :] for v in (lambda_q1, lambda_k1, lambda_q2, lambda_k2))
        o_diff, mod_rest = _diffattn_call(lams, g_diff_sub[l][None, :], shp(dqk), shp(dv), c_pad,
                                          w_ada[l], b_ada[l][None, :], MOD_MIX_ROWS * d, n_diff,
                                          lambda_init)
        mod_rest = mod_rest[:b].reshape(b, MOD_REST_ROWS, d)
        o_mla = _mlaattn_call(shp(qm), shp(kn), shp(kr), shp(vm), n_mla)

        x2 = _outproj_call(o_diff.reshape(t, -1), o_mla.reshape(t, -1), w_out[l], x2, mod_rest, s)

        last = l == depth - 1
        assert last, "final rmsnorm is fused into the last layer's MLP kernel"
        x2 = _ffn_call(x2, mod_rest, g_norm_ffn[l][None, :], w_ff1[l], w_ff2[l],
                       g_final[None, :], s)

    return x2.reshape(b, s, d)
```

```python
import functools
import math

import jax
import jax.numpy as jnp
import numpy as np
from jax import lax
from jax.experimental import pallas as pl
from jax.experimental.pallas import tpu as pltpu

F32 = jnp.float32
BF16 = jnp.bfloat16

LANES = 128
NORM_EPS = 1e-6

DIFF_HEAD_DIM = 64
DIFF_V_DIM = 2 * DIFF_HEAD_DIM
DIFF_ROT_DIM = DIFF_HEAD_DIM // 4
ROPE_THETA = 500000.0
MLA_V_DIM = 128
MLA_NOPE_DIM = 128
MLA_ROPE_DIM = 64
MLA_ROPE_THETA = 10000.0
MLA_QK_PAD = 256

DIFF_SCALE = DIFF_HEAD_DIM ** -0.5 * math.log2(math.e)
MLA_SCALE = (MLA_NOPE_DIM + MLA_ROPE_DIM) ** -0.5 * math.log2(math.e)

MIB = 1024 * 1024
VMEM_LIMIT = 56 * MIB


def _cparams(sem, vmem_limit=VMEM_LIMIT):
    return pltpu.CompilerParams(dimension_semantics=sem, vmem_limit_bytes=vmem_limit)


def _dot(a, b):
    return jnp.dot(a, b, preferred_element_type=F32)


def _dot_nt(a, b):
    return lax.dot_general(a, b, (((1,), (1,)), ((), ())), preferred_element_type=F32)


MOD_MIX_ROWS = 2
MOD_REST_ROWS = 4


def _mod_block(c_ref, w_ref, b_ref):
    c = c_ref[...]
    sc = c / (1.0 + jnp.exp(-c))
    return _dot(sc.astype(BF16), w_ref[...].astype(BF16)) + b_ref[...]


def _mod_kernel(c_ref, w_ref, b_ref, o_ref):
    o_ref[...] = _mod_block(c_ref, w_ref, b_ref)


def _mod_call(c_pad, w_ada, b_ada, n, tn=1024):
    m, d = c_pad.shape
    return pl.pallas_call(
        _mod_kernel,
        out_shape=jax.ShapeDtypeStruct((m, n), F32),
        grid=(n // tn,),
        in_specs=[pl.BlockSpec((m, d), lambda j: (0, 0)),
                  pl.BlockSpec((d, tn), lambda j: (0, j)),
                  pl.BlockSpec((1, tn), lambda j: (0, j))],
        out_specs=pl.BlockSpec((m, tn), lambda j: (0, j)),
        compiler_params=_cparams(("arbitrary",), 24 * MIB),
        name="mod",
    )(c_pad, w_ada, b_ada)


def _two_stage_rows(n_rows, rows, sumsq, finish):
    chunk = lambda r: pl.ds(pl.multiple_of(r * rows, rows), rows)

    def body(r, ss_prev):
        ss = sumsq(chunk(r))
        finish(chunk(r - 1), ss_prev)
        return ss

    n = n_rows // rows
    ss_last = lax.fori_loop(1, n, body, sumsq(chunk(0)))
    finish(chunk(n - 1), ss_last)


def _norm_modulate_rows(x_ref, g_ref, mod_ref, shift_row, scale_row, h_ref, rows):
    tm, d = x_ref.shape
    shift = mod_ref[0, shift_row:shift_row + 1, :]
    gs = g_ref[...] * (1.0 + mod_ref[0, scale_row:scale_row + 1, :])

    def sumsq(sl):
        x = x_ref[sl, :]
        return jnp.sum(x * x, axis=-1, keepdims=True)

    def finish(sl, ss):
        rs = lax.rsqrt(ss * (1.0 / d) + NORM_EPS)
        h_ref[sl, :] = (x_ref[sl, :] * rs * gs + shift).astype(BF16)

    _two_stage_rows(tm, rows, sumsq, finish)


def _rope_slab(x, c, s, half, group):
    g = lax.broadcasted_iota(jnp.int32, (1, LANES), 1) % group
    partner = jnp.where(g >= half, pltpu.roll(x, half, 1), pltpu.roll(x, LANES - half, 1))
    return x * c + partner * s


def _rms_rows(x, g):
    ms = jnp.mean(x * x, axis=-1, keepdims=True)
    return x * lax.rsqrt(ms + NORM_EPS) * g


def _inproj_kernel(x_ref, mod_ref, g_ref, w_ref, cd_ref, sd_ref,
                   qk_ref, v_ref, lat_ref, h_ref, r_ref, *, rows, lat_cols):
    j = pl.program_id(1)
    tn = r_ref.shape[1]

    def project():
        return _dot_nt(h_ref[...], w_ref[...].astype(BF16))

    @pl.when(j == 0)
    def _():
        _norm_modulate_rows(x_ref, g_ref, mod_ref, 0, 1, h_ref, rows)
        r_ref[...] = project()

    @pl.when((j >= 1) & (j <= 4))
    def _():
        repeat = lambda ref: jnp.concatenate([ref[...]] * (LANES // ref.shape[1]), axis=1)
        c, s = repeat(cd_ref), repeat(sd_ref)
        scale = jnp.where(j <= 2, DIFF_SCALE, 1.0)
        for k in range(tn // LANES):
            sl = slice(k * LANES, (k + 1) * LANES)
            qk_ref[:, sl] = (_rope_slab(r_ref[:, sl], c, s, DIFF_ROT_DIM // 2, cd_ref.shape[1])
                             * scale).astype(BF16)
        r_ref[...] = project()

    @pl.when(j == 5)
    def _():
        v_ref[:, :tn] = r_ref[...].astype(BF16)
        r_ref[...] = project()

    @pl.when(j == 6)
    def _():
        v_ref[:, tn:] = r_ref[...].astype(BF16)
        r_ref[...] = project()

    @pl.when(j == 7)
    def _():
        lat_ref[:, :tn] = r_ref[...]
        lat_ref[:, tn:tn + lat_cols] = project()[:, :lat_cols]
        pad = lat_ref.shape[1] - (tn + lat_cols)
        lat_ref[:, tn + lat_cols:] = jnp.zeros((lat_ref.shape[0], pad), F32)


def _inproj_call(x2, mod3, g, w_t, tabs_d, dw, q_rank, kv_rank, seq, tm=1024):
    t, d = x2.shape
    tpb = seq // tm
    lat_cols = kv_rank + MLA_ROPE_DIM
    lat_w = q_rank + kv_rank + LANES
    tn = dw // 2
    assert dw % (2 * LANES) == 0 and seq % tm == 0 and q_rank == tn and lat_cols <= tn
    assert w_t.shape[0] == 3 * dw + q_rank + lat_cols
    tab = pl.BlockSpec((tm, tabs_d[0].shape[1]), lambda i, j: (i, 0))
    return pl.pallas_call(
        functools.partial(_inproj_kernel, rows=64, lat_cols=lat_cols),
        out_shape=(jax.ShapeDtypeStruct((t, 2 * dw), BF16),
                   jax.ShapeDtypeStruct((t, dw), BF16),
                   jax.ShapeDtypeStruct((t, lat_w), F32)),
        grid=(t // tm, 8),
        in_specs=[pl.BlockSpec((tm, d), lambda i, j: (i, 0)),
                  pl.BlockSpec((1, mod3.shape[1], d), lambda i, j: (i // tpb, 0, 0)),
                  pl.BlockSpec((1, d), lambda i, j: (0, 0)),
                  pl.BlockSpec((tn, d), lambda i, j: (j, 0)),
                  tab, tab],
        out_specs=(pl.BlockSpec((tm, tn), lambda i, j: (i, jnp.clip(j - 1, 0, 3))),
                   pl.BlockSpec((tm, dw), lambda i, j: (i, 0)),
                   pl.BlockSpec((tm, lat_w), lambda i, j: (i, 0))),
        scratch_shapes=[pltpu.VMEM((tm, d), BF16), pltpu.VMEM((tm, tn), F32)],
        compiler_params=_cparams(("arbitrary", "arbitrary"), 50 * MIB),
        name="inproj",
    )(x2, mod3, g, w_t, *tabs_d)


def _latent_kernel(lat_ref, cm_ref, sm_ref,
                   gq_ref, gkv_ref, wq_ref, wkv_ref, qmo_ref, kno_ref, kro_ref, vmo_ref,
                   *, n_heads):
    half_m = MLA_ROPE_DIM // 2
    rope_m = lambda x: _rope_slab(x, cm_ref[...], sm_ref[...], half_m, LANES)
    q_rank, kv_rank = gq_ref.shape[1], gkv_ref.shape[1]

    q = _dot(_rms_rows(lat_ref[:, :q_rank], gq_ref[...]).astype(BF16), wq_ref[...])
    kv = _dot(_rms_rows(lat_ref[:, q_rank:q_rank + kv_rank], gkv_ref[...]).astype(BF16),
              wkv_ref[...])
    kro_ref[...] = rope_m(lat_ref[:, q_rank + kv_rank:]).astype(BF16)

    for h in range(n_heads):
        base = h * MLA_QK_PAD
        nope = slice(base, base + LANES)
        rope = slice(base + LANES, base + 2 * LANES)
        head = slice(h * LANES, (h + 1) * LANES)
        qmo_ref[:, nope] = (q[:, nope] * MLA_SCALE).astype(BF16)
        qmo_ref[:, rope] = (rope_m(q[:, rope]) * MLA_SCALE).astype(BF16)
        kno_ref[:, head] = kv[:, nope].astype(BF16)
        vmo_ref[:, head] = kv[:, rope].astype(BF16)


def _latent_call(lat, tabs_m, gq, gkv, wq_pad, wkv, n_heads, tm=1024):
    t = lat.shape[0]
    q_rank = gq.shape[1]
    kv_rank = gkv.shape[1]
    qk_w = n_heads * MLA_QK_PAD
    v_w = n_heads * MLA_V_DIM
    assert q_rank % LANES == 0 and kv_rank % LANES == 0
    assert lat.shape[1] == q_rank + kv_rank + LANES
    row = lambda w, c: pl.BlockSpec((tm, w), lambda i, c=c: (i, c))
    full = lambda a: pl.BlockSpec(a.shape, lambda i: (0, 0))
    in_specs = [row(lat.shape[1], 0)]
    in_specs += [row(LANES, 0)] * len(tabs_m)
    in_specs += [full(gq), full(gkv), full(wq_pad), full(wkv)]
    kn_w = n_heads * MLA_NOPE_DIM
    out_shape = (jax.ShapeDtypeStruct((t, qk_w), BF16),
                 jax.ShapeDtypeStruct((t, kn_w), BF16),
                 jax.ShapeDtypeStruct((t, LANES), BF16),
                 jax.ShapeDtypeStruct((t, v_w), BF16))
    return pl.pallas_call(
        functools.partial(_latent_kernel, n_heads=n_heads),
        out_shape=out_shape,
        grid=(t // tm,),
        in_specs=in_specs,
        out_specs=(row(qk_w, 0), row(kn_w, 0), row(LANES, 0), row(v_w, 0)),
        compiler_params=_cparams(("arbitrary",), 40 * MIB),
        name="latent",
    )(lat, *tabs_m, gq, gkv, wq_pad, wkv)


def _numerators(s_ref, e_ref, rows=16):
    for r in range(0, s_ref.shape[0], rows):
        s = s_ref[r:r + rows, :]
        e_ref[r:r + rows, :] = jnp.exp2(s - jnp.max(s, axis=-1, keepdims=True)).astype(BF16)


def _weighted_values(e_ref, v1):
    o = _dot(e_ref[...], v1)
    dv = v1.shape[1] // 2
    return o[:, :dv], o[:, dv:dv + 1]


SCORE_BUFS = 4
NUMER_LAG = 1


def _attn_step(q_ref, keys, v_ref, o_ref, s_refs, e_refs, make_lhs, finish):
    n, n_s = len(e_refs), len(s_refs)
    tq = o_ref.shape[1] // n
    assert n_s >= NUMER_LAG + 2 and n % n_s == 0

    @pl.when(pl.program_id(0) == 0)
    def _():
        for t in range(NUMER_LAG):
            s_refs[(t - NUMER_LAG) % n_s][...] = jnp.zeros_like(s_refs[0])
        for t in range(n - NUMER_LAG):
            e_refs[t][...] = jnp.zeros_like(e_refs[t])

    v = v_ref[0]
    v1 = jnp.concatenate([v, jnp.ones_like(v)], axis=1)
    for k in range(n):
        rows = slice(k * tq, (k + 1) * tq)
        behind = k - NUMER_LAG
        s_refs[k % n_s][...] = _dot_nt(make_lhs(q_ref[0, rows]), keys())
        _numerators(s_refs[behind % n_s], e_refs[behind % n])
        o_ref[0, rows] = finish(*_weighted_values(e_refs[k], v1)).astype(o_ref.dtype)


def _attn_specs(b, n_heads, seq, tq, n_tiles, qk_width, v_width, q_col=0, k_col=0, v_col=0):
    rows = n_tiles * tq
    n_groups = seq // rows
    total = b * n_heads * n_groups
    assert seq % rows == 0

    def decode(g):
        return g // (n_heads * n_groups), (g // n_groups) % n_heads, g % n_groups

    cur = lambda g: decode(jnp.minimum(g, total - 1))
    prev = lambda g: decode(jnp.maximum(g - 1, 0))
    q_spec = pl.BlockSpec((1, rows, qk_width),
                          lambda g: (cur(g)[0], cur(g)[2], q_col + cur(g)[1]))
    k_spec = pl.BlockSpec((1, seq, qk_width), lambda g: (cur(g)[0], 0, k_col + cur(g)[1]))
    v_spec = pl.BlockSpec((1, seq, v_width), lambda g: (prev(g)[0], 0, v_col + prev(g)[1]))
    o_spec = pl.BlockSpec((1, rows, v_width), lambda g: (prev(g)[0], prev(g)[2], prev(g)[1]))
    return total + 1, q_spec, k_spec, v_spec, o_spec, cur


def _attn_scratch(rows, seq, n_tiles):
    return ([pltpu.VMEM((rows, seq), F32)] * SCORE_BUFS
            + [pltpu.VMEM((rows, seq), BF16)] * n_tiles)


def _diffattn_kernel(lq1_ref, lk1_ref, lq2_ref, lk2_ref, g_ref, c_ref, wa_ref, ba_ref,
                     q_ref, k_ref, v_ref, o_ref, mod_ref, *scratch, lambda_init):
    mod_ref[...] = _mod_block(c_ref, wa_ref, ba_ref)

    lam = (jnp.exp(jnp.sum(lq1_ref[...] * lk1_ref[...], axis=-1, keepdims=True))
           - jnp.exp(jnp.sum(lq2_ref[...] * lk2_ref[...], axis=-1, keepdims=True))
           + lambda_init)
    gain = g_ref[...] * (1.0 - lambda_init)

    def make_lhs(q):
        lane = lax.broadcasted_iota(jnp.int32, q.shape, 1)
        zero = jnp.zeros_like(q)
        return jnp.concatenate([jnp.where(lane < DIFF_HEAD_DIM, q, zero),
                                jnp.where(lane >= DIFF_HEAD_DIM, q, zero)], axis=0)

    def finish(o, l):
        tq = o.shape[0] // 2
        out = o[:tq] / l[:tq] - lam * (o[tq:] / l[tq:])
        ms = jnp.mean(out * out, axis=-1, keepdims=True)
        return out * lax.rsqrt(ms + NORM_EPS) * gain

    _attn_step(q_ref, lambda: k_ref[0], v_ref, o_ref, scratch[:SCORE_BUFS],
               scratch[SCORE_BUFS:], make_lhs, finish)


def _diffattn_call(lams, g, qk, v, c_pad, w_ada, b_ada, mod_col0, n_heads, lambda_init,
                   tq=256, n_tiles=4):
    b, s, _ = qk.shape
    steps, q_spec, k_spec, v_spec, o_spec, _ = _attn_specs(
        b, n_heads, s, tq, n_tiles, LANES, LANES, q_col=0, k_col=n_heads, v_col=0)
    d = c_pad.shape[1]
    n_mod = w_ada.shape[1] - mod_col0
    mw = LANES * pl.cdiv(n_mod // LANES, steps - 1)
    n_blk = n_mod // mw
    assert n_mod % mw == 0 and mod_col0 % mw == 0 and n_blk <= steps
    blk = lambda g: jnp.minimum(g, n_blk - 1)
    vec = lambda a: pl.BlockSpec(a.shape, lambda g: (0, 0))
    return pl.pallas_call(
        functools.partial(_diffattn_kernel, lambda_init=lambda_init),
        out_shape=(jax.ShapeDtypeStruct((b, s, n_heads * DIFF_V_DIM), BF16),
                   jax.ShapeDtypeStruct((c_pad.shape[0], n_mod), F32)),
        grid=(steps,),
        in_specs=[vec(lams[0]), vec(lams[1]), vec(lams[2]), vec(lams[3]), vec(g), vec(c_pad),
                  pl.BlockSpec((d, mw), lambda g: (0, mod_col0 // mw + blk(g))),
                  pl.BlockSpec((1, mw), lambda g: (0, mod_col0 // mw + blk(g))),
                  q_spec, k_spec, v_spec],
        out_specs=(o_spec, pl.BlockSpec((c_pad.shape[0], mw), lambda g: (0, blk(g)))),
        scratch_shapes=_attn_scratch(2 * tq, s, n_tiles),
        compiler_params=_cparams(("arbitrary",)),
        name="diffattn",
    )(*lams, g, c_pad, w_ada, b_ada, qk, qk, v)


def _mlaattn_kernel(q_ref, kn_ref, kr_ref, v_ref, o_ref, *scratch):
    keys = lambda: jnp.concatenate([kn_ref[0], kr_ref[0]], axis=1)
    _attn_step(q_ref, keys, v_ref, o_ref, scratch[:SCORE_BUFS], scratch[SCORE_BUFS:],
               lambda q: q, lambda o, l: o / l)


def _mlaattn_call(qm, kn, kr, vm, n_heads, tq=512, n_tiles=4):
    b, s, _ = qm.shape
    steps, q_spec, _, v_spec, o_spec, cur = _attn_specs(b, n_heads, s, tq, n_tiles,
                                                        MLA_QK_PAD, MLA_V_DIM)
    kn_spec = pl.BlockSpec((1, s, MLA_NOPE_DIM), lambda g: (cur(g)[0], 0, cur(g)[1]))
    kr_spec = pl.BlockSpec((1, s, kr.shape[2]), lambda g: (cur(g)[0], 0, 0))
    assert MLA_NOPE_DIM + kr.shape[2] == MLA_QK_PAD
    return pl.pallas_call(
        _mlaattn_kernel,
        out_shape=jax.ShapeDtypeStruct(vm.shape, BF16),
        grid=(steps,),
        in_specs=[q_spec, kn_spec, kr_spec, v_spec],
        out_specs=o_spec,
        scratch_shapes=_attn_scratch(tq, s, n_tiles),
        compiler_params=_cparams(("arbitrary",)),
        name="mlaattn",
    )(qm, kn, kr, vm)


def _outproj_kernel(od_ref, om_ref, w_ref, x_ref, mod_ref, o_ref, ss_ref, wb_ref):
    @pl.when(pl.program_id(0) == 0)
    def _():
        wb_ref[...] = w_ref[...].astype(BF16)

    kd = od_ref.shape[1]
    acc = _dot(od_ref[...], wb_ref[:kd, :])
    acc += _dot(om_ref[...], wb_ref[kd:, :])
    x1 = x_ref[...] + mod_ref[0, 0:1, :] * acc
    o_ref[...] = x1
    ss_ref[...] = jnp.broadcast_to(jnp.sum(x1 * x1, axis=-1, keepdims=True), ss_ref.shape)


def _outproj_call(od, om, w_out, x2, mod3, seq, tm=512):
    t, d = x2.shape
    tpb = seq // tm
    assert seq % tm == 0 and w_out.shape[0] == od.shape[1] + om.shape[1]
    return pl.pallas_call(
        _outproj_kernel,
        out_shape=(jax.ShapeDtypeStruct((t, d), F32), jax.ShapeDtypeStruct((t, LANES), F32)),
        grid=(t // tm,),
        in_specs=[pl.BlockSpec((tm, od.shape[1]), lambda i: (i, 0)),
                  pl.BlockSpec((tm, om.shape[1]), lambda i: (i, 0)),
                  pl.BlockSpec(w_out.shape, lambda i: (0, 0), pipeline_mode=pl.Buffered(1)),
                  pl.BlockSpec((tm, d), lambda i: (i, 0)),
                  pl.BlockSpec((1, mod3.shape[1], d), lambda i: (i // tpb, 0, 0))],
        out_specs=(pl.BlockSpec((tm, d), lambda i: (i, 0)),
                   pl.BlockSpec((tm, LANES), lambda i: (i, 0))),
        scratch_shapes=[pltpu.VMEM(w_out.shape, BF16)],
        compiler_params=_cparams(("arbitrary",)),
        name="outproj",
    )(od, om, w_out, x2, mod3)


def _ffn_kernel(x_ref, ss_ref, mod_ref, g_ref, w1_ref, w2p_ref, w2a_ref, gf_ref, o_ref,
                h_ref, ra_ref, rb_ref, *, rows):
    j = pl.program_id(1)
    nj = pl.num_programs(1) - 1
    tf = ra_ref.shape[1]
    act = lambda u: jnp.square(jnp.maximum(u, 0.0)).astype(BF16)

    @pl.when(j == 0)
    def _():
        d = x_ref.shape[1]
        shift = mod_ref[0, 1:2, :]
        gs = g_ref[...] * (1.0 + mod_ref[0, 2:3, :])

        def modulate(r, carry):
            sl = pl.ds(pl.multiple_of(r * rows, rows), rows)
            rs = lax.rsqrt(ss_ref[sl, 0:1] * (1.0 / d) + NORM_EPS)
            h_ref[sl, :] = (x_ref[sl, :] * rs * gs + shift).astype(BF16)
            return carry

        lax.fori_loop(0, x_ref.shape[0] // rows, modulate, 0)
        u = _dot(h_ref[...], w1_ref[...].astype(BF16))
        ra_ref[...] = act(u[:, :tf])
        o_ref[...] = _dot(ra_ref[...], w2a_ref[...].astype(BF16))
        rb_ref[...] = act(u[:, tf:])

    @pl.when((j > 0) & (j < nj))
    def _():
        u = _dot(h_ref[...], w1_ref[...].astype(BF16))
        o_ref[...] += _dot(rb_ref[...], w2p_ref[...].astype(BF16))
        ra_ref[...] = act(u[:, :tf])
        o_ref[...] += _dot(ra_ref[...], w2a_ref[...].astype(BF16))
        rb_ref[...] = act(u[:, tf:])

    @pl.when(j == nj)
    def _():
        o_ref[...] += _dot(rb_ref[...], w2p_ref[...].astype(BF16))
        d = x_ref.shape[1]
        gate = mod_ref[0, 3:4, :]
        gf = gf_ref[...]
        resid = lambda sl: x_ref[sl, :] + gate * o_ref[sl, :]

        def sumsq(sl):
            y = resid(sl)
            return jnp.sum(y * y, axis=-1, keepdims=True)

        def finish(sl, ss):
            o_ref[sl, :] = resid(sl) * lax.rsqrt(ss * (1.0 / d) + NORM_EPS) * gf

        _two_stage_rows(x_ref.shape[0], rows, sumsq, finish)


def _ffn_call(x1, ss, mod3, g, w1, w2, gf, seq, tm=1024, tf=256):
    t, d = x1.shape
    n_chunks = w1.shape[1] // tf
    nj = n_chunks // 2
    tpb = seq // tm
    last_a = n_chunks - 2
    return pl.pallas_call(
        functools.partial(_ffn_kernel, rows=64),
        out_shape=jax.ShapeDtypeStruct((t, d), F32),
        grid=(t // tm, nj + 1),
        in_specs=[pl.BlockSpec((tm, d), lambda i, j: (i, 0)),
                  pl.BlockSpec((tm, ss.shape[1]), lambda i, j: (i, 0),
                               pipeline_mode=pl.Buffered(1)),
                  pl.BlockSpec((1, mod3.shape[1], d), lambda i, j: (i // tpb, 0, 0)),
                  pl.BlockSpec((1, d), lambda i, j: (0, 0)),
                  pl.BlockSpec((d, 2 * tf), lambda i, j: (0, jnp.minimum(j, nj - 1))),
                  pl.BlockSpec((tf, d), lambda i, j: (jnp.maximum(2 * j - 1, 0), 0)),
                  pl.BlockSpec((tf, d), lambda i, j: (jnp.minimum(2 * j, last_a), 0)),
                  pl.BlockSpec((1, d), lambda i, j: (0, 0))],
        out_specs=pl.BlockSpec((tm, d), lambda i, j: (i, 0)),
        scratch_shapes=[pltpu.VMEM((tm, d), BF16),
                        pltpu.VMEM((tm, tf), BF16),
                        pltpu.VMEM((tm, tf), BF16)],
        compiler_params=_cparams(("arbitrary", "arbitrary")),
        name="ffn",
    )(x1, ss, mod3, g, w1, w2, w2, gf)


def _rope_tables(positions, specs):
    inv = jnp.concatenate([1.0 / (theta ** (jnp.arange(0, rot, 2, dtype=F32) / rot))
                           for rot, theta, _ in specs])
    ang = positions.astype(F32).reshape(-1, 1) * inv[None, :]
    cos_all, sin_all = jnp.cos(ang), jnp.sin(ang)
    t = ang.shape[0]
    out, start = [], 0
    for rot, _, group in specs:
        cos, sin = cos_all[:, start:start + rot // 2], sin_all[:, start:start + rot // 2]
        start += rot // 2
        widen = lambda lo, hi, fill: jnp.concatenate(
            [lo, hi, jnp.full((t, group - rot), fill, F32)], axis=1)
        out.append((widen(cos, cos, 1.0), widen(-sin, sin, 0.0)))
    return out


def kernel(x, c, positions, w_ada, b_ada, g_norm_mix, w_in, lambda_q1, lambda_k1, lambda_q2,
           lambda_k2, g_diff_sub, g_q_a, w_q_b, g_kv_a, w_kv_b, w_out, g_norm_ffn, w_ff1, w_ff2,
           g_final):
    b, s, d = x.shape
    depth = w_ada.shape[0]
    t = b * s
    q_rank = g_q_a.shape[1]
    kv_rank = g_kv_a.shape[1]
    n_mla = w_kv_b.shape[2] // (MLA_NOPE_DIM + MLA_V_DIM)
    n_diff = (w_in.shape[2] - q_rank - kv_rank - MLA_ROPE_DIM) // (3 * DIFF_V_DIM)
    assert n_diff == n_mla

    tabs_d, tabs_m = _rope_tables(positions, ((DIFF_ROT_DIM, ROPE_THETA, DIFF_HEAD_DIM),
                                             (MLA_ROPE_DIM, MLA_ROPE_THETA, LANES)))
    c_pad = jnp.pad(c, ((0, 8 - b), (0, 0)))
    x2 = x.reshape(t, d)

    for l in range(depth):
        lambda_init = 0.8 - 0.6 * float(np.exp(-0.3 * l))
        assert w_ada.shape[2] == (MOD_MIX_ROWS + MOD_REST_ROWS) * d
        mod_mix = _mod_call(c_pad, w_ada[l], b_ada[l][None, :], MOD_MIX_ROWS * d)
        mod_mix = mod_mix[:b].reshape(b, MOD_MIX_ROWS, d)

        w_t = jnp.swapaxes(w_in[l], 0, 1)
        dqk, dv, lat = _inproj_call(x2, mod_mix, g_norm_mix[l][None, :], w_t, tabs_d,
                                    n_diff * DIFF_V_DIM, q_rank, kv_rank, s)

        wq_pad = jnp.pad(w_q_b[l].reshape(q_rank, n_mla, MLA_NOPE_DIM + MLA_ROPE_DIM),
                         ((0, 0), (0, 0), (0, MLA_QK_PAD - MLA_NOPE_DIM - MLA_ROPE_DIM))
                         ).reshape(q_rank, n_mla * MLA_QK_PAD).astype(BF16)
        qm, kn, kr, vm = _latent_call(lat, tabs_m, g_q_a[l][None, :], g_kv_a[l][None, :],
                                      wq_pad, w_kv_b[l].astype(BF16), n_mla)

        shp = lambda a: a.reshape(b, s, a.shape[1])
        lams = tuple(v[l][None, :] for v in (lambda_q1, lambda_k1, lambda_q2, lambda_k2))
        o_diff, mod_rest = _diffattn_call(lams, g_diff_sub[l][None, :], shp(dqk), shp(dv), c_pad,
                                          w_ada[l], b_ada[l][None, :], MOD_MIX_ROWS * d, n_diff,
                                          lambda_init)
        mod_rest = mod_rest[:b].reshape(b, MOD_REST_ROWS, d)
        o_mla = _mlaattn_call(shp(qm), shp(kn), shp(kr), shp(vm), n_mla)

        x2, ss = _outproj_call(o_diff.reshape(t, -1), o_mla.reshape(t, -1), w_out[l], x2,
                               mod_rest, s)

        last = l == depth - 1
        assert last, "final rmsnorm is fused into the last layer's MLP kernel"
        x2 = _ffn_call(x2, ss, mod_rest, g_norm_ffn[l][None, :], w_ff1[l], w_ff2[l],
                       g_final[None, :], s)

    return x2.reshape(b, s, d)
```
